```python
import math
import jax, jax.numpy as jnp
from jax import lax
import numpy as np

D_MODEL = 1024
BATCH = 2
SEQ = 8192
DEPTH = 4
DEC_BATCH = 128
DEC_SEQ = 4
PAST_LEN = 8192
PAGE_SIZE = 128

N_MIXERS = 2
N_SWA = (DEPTH + 1) // 2
N_GDN = DEPTH // 2
SWA_HEADS = 16
SWA_KV_HEADS = 4
SWA_HEAD_DIM = 64
SWA_GROUP = SWA_HEADS // SWA_KV_HEADS
WINDOW = 128
SWA_BLOCK = WINDOW
SWA_Q_W = SWA_HEADS * SWA_HEAD_DIM
SWA_KV_W = SWA_KV_HEADS * SWA_HEAD_DIM
SWA_IN_W = 2 * SWA_Q_W + 2 * SWA_KV_W
SWA_SCALE = SWA_HEAD_DIM ** -0.5
GDN_QK_HEADS = 8
GDN_V_HEADS = 16
GDN_HEAD_DIM = 128
GDN_K_W = GDN_QK_HEADS * GDN_HEAD_DIM
GDN_V_W = GDN_V_HEADS * GDN_HEAD_DIM
GDN_CONV = 4
GDN_CONV_CH = 2 * GDN_K_W + GDN_V_W
GDN_IN_W = GDN_CONV_CH + GDN_V_W + 2 * GDN_V_HEADS
GDN_CHUNK = 64
EPS = 1e-6

kernel_name = 'hybrid_swa_sink_gdn_adaln_step'


def rms_norm(x, g):
    xf = x.astype(jnp.float32)
    y = xf * lax.rsqrt(jnp.mean(xf * xf, axis=-1, keepdims=True) + EPS)
    return (y * g.astype(jnp.float32)).astype(x.dtype)


def l2_norm(x):
    xf = x.astype(jnp.float32)
    return xf * lax.rsqrt(jnp.sum(xf * xf, axis=-1, keepdims=True) + EPS)


def modulate(x, c, norm_g, w_mod, b_mod):
    shift, scale, gate = jnp.split(jax.nn.silu(c) @ w_mod + b_mod, 3, axis=-1)
    h = rms_norm(x, norm_g) * (1 + scale[:, None]) + shift[:, None]
    return h, gate[:, None]


def sink_softmax(logits, mask, sinks):
    logits = jnp.where(mask, logits, -jnp.inf)
    sink = sinks.astype(jnp.float32).reshape(SWA_KV_HEADS, SWA_GROUP, 1, 1)
    m = jnp.maximum(jnp.max(logits, axis=-1, keepdims=True), sink)
    e = jnp.exp(logits - m)
    return e / (jnp.sum(e, axis=-1, keepdims=True) + jnp.exp(sink - m))


def swa_prompt_core(q, k, v, sinks):
    B, T = q.shape[:2]
    nb = T // SWA_BLOCK
    qb = q.reshape(B, nb, SWA_BLOCK, SWA_KV_HEADS, SWA_GROUP, SWA_HEAD_DIM)
    kb = k.reshape(B, nb, SWA_BLOCK, SWA_KV_HEADS, SWA_HEAD_DIM)
    vb = v.reshape(B, nb, SWA_BLOCK, SWA_KV_HEADS, SWA_HEAD_DIM)

    def band(x):
        prev = jnp.pad(x, ((0, 0), (1, 0), (0, 0), (0, 0), (0, 0)))[:, :-1]
        return jnp.concatenate([prev, x], axis=2)

    logits = jnp.einsum('bnqkgd,bnskd->bnkgqs', qb, band(kb),
                        preferred_element_type=jnp.float32) * SWA_SCALE
    qi = jnp.arange(SWA_BLOCK)[:, None]
    si = jnp.arange(2 * SWA_BLOCK)[None, :]
    dist = SWA_BLOCK + qi - si
    in_window = (dist >= 0) & (dist <= WINDOW)
    first = (jnp.arange(nb) == 0)[:, None, None]
    mask = in_window[None] & ~(first & (si < SWA_BLOCK)[None])
    p = sink_softmax(logits, mask[None, :, None, None], sinks)
    o = jnp.einsum('bnkgqs,bnskd->bnqkgd', p.astype(v.dtype), band(vb))
    return o.reshape(B, T, SWA_Q_W)


def swa_sample_core(q, k, v, k_buf, v_buf, sinks):
    B, T = q.shape[:2]
    P = k_buf.shape[1]
    k_all = jnp.concatenate([k_buf.astype(k.dtype), k], axis=1)
    v_all = jnp.concatenate([v_buf.astype(v.dtype), v], axis=1)
    logits = jnp.einsum('bqkgd,bskd->bkgqs', q, k_all,
                        preferred_element_type=jnp.float32) * SWA_SCALE
    dist = (P + jnp.arange(T))[:, None] - jnp.arange(P + T)[None, :]
    mask = (dist >= 0) & (dist <= WINDOW)
    p = sink_softmax(logits, mask, sinks)
    o = jnp.einsum('bkgqs,bskd->bqkgd', p.astype(v.dtype), v_all).reshape(B, T, SWA_Q_W)
    return o, k_all[:, -P:], v_all[:, -P:]


def swa_branch(h, w_in, sinks, w_out, k_buf=None, v_buf=None):
    B, T, _ = h.shape
    q, k, v, z = jnp.split(h @ w_in, [SWA_Q_W, SWA_Q_W + SWA_KV_W, SWA_Q_W + 2 * SWA_KV_W], axis=-1)
    q = q.reshape(B, T, SWA_KV_HEADS, SWA_GROUP, SWA_HEAD_DIM)
    k = k.reshape(B, T, SWA_KV_HEADS, SWA_HEAD_DIM)
    v = v.reshape(B, T, SWA_KV_HEADS, SWA_HEAD_DIM)
    if k_buf is None:
        attn = swa_prompt_core(q, k, v, sinks)
        keep = min(WINDOW, T)
        new_k, new_v = k[:, -keep:], v[:, -keep:]
    else:
        attn, new_k, new_v = swa_sample_core(q, k, v, k_buf, v_buf, sinks)
    out = (attn * jax.nn.silu(z)) @ w_out
    return out, new_k, new_v


def gated_delta_chunked(q, k, v, g, beta, s0, chunk):
    B, T, H, dk = q.shape
    dv = v.shape[-1]
    n = T // chunk

    def blocks(x):
        x = x.reshape((B, n, chunk, H) + x.shape[3:])
        return jnp.moveaxis(x, (1, 3), (0, 2))

    qc = blocks(q) * dk ** -0.5
    kc = blocks(k)
    vc = blocks(v)
    bc = blocks(beta)
    gc = jnp.cumsum(blocks(g), axis=-1)
    idx = jnp.arange(chunk)
    incl = idx[:, None] >= idx[None, :]
    strict = idx[:, None] > idx[None, :]
    decay = jnp.exp(jnp.where(incl, gc[..., :, None] - gc[..., None, :], -jnp.inf))
    kb = kc * bc[..., None]
    lower = jnp.where(strict, jnp.einsum('nbhik,nbhjk->nbhij', kb, kc) * decay, 0.0)
    rhs = jnp.concatenate([vc * bc[..., None], kb * jnp.exp(gc)[..., None]], axis=-1)
    sol = lax.linalg.triangular_solve(lower + jnp.eye(chunk, dtype=lower.dtype), rhs,
                                      left_side=True, lower=True, unit_diagonal=True)
    w_v, w_k = sol[..., :dv], sol[..., dv:]
    qk = jnp.einsum('nbhik,nbhjk->nbhij', qc, kc) * decay
    q_dec = qc * jnp.exp(gc)[..., None]
    k_dec = kc * jnp.exp(gc[..., -1:] - gc)[..., None]
    g_tot = jnp.exp(gc[..., -1])

    def step(S, xs):
        w_v_c, w_k_c, qk_c, q_dec_c, k_dec_c, g_tot_c = xs
        u = w_v_c - jnp.einsum('bhck,bhkv->bhcv', w_k_c, S)
        o = jnp.einsum('bhck,bhkv->bhcv', q_dec_c, S) + jnp.einsum('bhij,bhjv->bhiv', qk_c, u)
        S = S * g_tot_c[..., None, None] + jnp.einsum('bhck,bhcv->bhkv', k_dec_c, u)
        return S, o

    s_final, o = lax.scan(step, s0, (w_v, w_k, qk, q_dec, k_dec, g_tot))
    o = jnp.moveaxis(o, (0, 2), (1, 3)).reshape(B, T, H, dv)
    return o, s_final


def gdn_branch(h, conv_buf, s0, w_in, conv_w, a_log, dt_bias, o_norm_g, w_out):
    B, T, _ = h.shape
    xqkv, z, a, b = jnp.split(h @ w_in, [GDN_CONV_CH, GDN_CONV_CH + GDN_V_W,
                                         GDN_CONV_CH + GDN_V_W + GDN_V_HEADS], axis=-1)
    xpad = jnp.concatenate([conv_buf.astype(xqkv.dtype), xqkv], axis=1)
    conv = lax.conv_general_dilated(xpad, conv_w[:, None, :].astype(xpad.dtype), window_strides=(1,),
                                    padding='VALID', dimension_numbers=('NWC', 'WIO', 'NWC'),
                                    feature_group_count=GDN_CONV_CH)
    conv = jax.nn.silu(conv)
    q, k, v = jnp.split(conv, [GDN_K_W, 2 * GDN_K_W], axis=-1)
    rep = GDN_V_HEADS // GDN_QK_HEADS
    q = jnp.repeat(l2_norm(q.reshape(B, T, GDN_QK_HEADS, GDN_HEAD_DIM)), rep, axis=2)
    k = jnp.repeat(l2_norm(k.reshape(B, T, GDN_QK_HEADS, GDN_HEAD_DIM)), rep, axis=2)
    v = v.reshape(B, T, GDN_V_HEADS, GDN_HEAD_DIM).astype(jnp.float32)
    beta = jax.nn.sigmoid(b.astype(jnp.float32))
    g = -jnp.exp(a_log.astype(jnp.float32)) * jax.nn.softplus(a.astype(jnp.float32) + dt_bias.astype(jnp.float32))
    chunk = math.gcd(T, GDN_CHUNK)
    o, s_new = gated_delta_chunked(q, k, v, g, beta, s0.astype(jnp.float32), chunk)
    o = rms_norm(o, o_norm_g).astype(h.dtype)
    o = (o * jax.nn.silu(z).reshape(B, T, GDN_V_HEADS, GDN_HEAD_DIM)).reshape(B, T, GDN_V_W)
    return o @ w_out, xpad[:, -(GDN_CONV - 1):], s_new.astype(s0.dtype)


def setup_inputs(seed: int = 0) -> dict:
    key = jax.random.key(seed)
    ks = jax.random.split(key, 24)
    f32 = jnp.float32

    def nrm(k, shape, s):
        return s * jax.random.normal(k, shape, f32)

    w_buf = min(WINDOW, PAST_LEN)
    dt = jnp.exp(jax.random.uniform(ks[17], (N_GDN, GDN_V_HEADS), f32, math.log(1e-3), math.log(1e-1)))
    return {
        'x_prompt': nrm(ks[0], (BATCH, SEQ, D_MODEL), 1.0),
        'x_sample': nrm(ks[1], (DEC_BATCH, DEC_SEQ, D_MODEL), 1.0),
        'cache_swa_k': nrm(ks[2], (N_SWA, DEC_BATCH, w_buf, SWA_KV_HEADS, SWA_HEAD_DIM), 1.0),
        'cache_swa_v': nrm(ks[3], (N_SWA, DEC_BATCH, w_buf, SWA_KV_HEADS, SWA_HEAD_DIM), 1.0),
        'state_gdn_conv': nrm(ks[4], (N_GDN, DEC_BATCH, GDN_CONV - 1, GDN_CONV_CH), 1.0),
        'state_gdn_s': nrm(ks[5], (N_GDN, DEC_BATCH, GDN_V_HEADS, GDN_HEAD_DIM, GDN_HEAD_DIM), 0.1),
        'c_prompt': nrm(ks[6], (BATCH, D_MODEL), 1.0),
        'c_sample': nrm(ks[7], (DEC_BATCH, D_MODEL), 1.0),
        'norm_g': 1.0 + nrm(ks[8], (DEPTH, D_MODEL), 0.02),
        'w_mod': nrm(ks[9], (DEPTH, D_MODEL, 3 * D_MODEL), 0.5 * D_MODEL ** -0.5),
        'b_mod': nrm(ks[10], (DEPTH, 3 * D_MODEL), 0.01),
        'swa_w_in': nrm(ks[11], (N_SWA, D_MODEL, SWA_IN_W), D_MODEL ** -0.5),
        'swa_sinks': nrm(ks[12], (N_SWA, SWA_HEADS), 0.5),
        'swa_w_out': nrm(ks[13], (N_SWA, SWA_Q_W, D_MODEL), SWA_Q_W ** -0.5),
        'gdn_w_in': nrm(ks[14], (N_GDN, D_MODEL, GDN_IN_W), D_MODEL ** -0.5),
        'gdn_conv_w': nrm(ks[15], (N_GDN, GDN_CONV, GDN_CONV_CH), GDN_CONV ** -0.5),
        'gdn_a_log': jnp.log(jax.random.uniform(ks[16], (N_GDN, GDN_V_HEADS), f32, 1.0, 16.0)),
        'gdn_dt_bias': dt + jnp.log(-jnp.expm1(-dt)),
        'gdn_o_norm_g': 1.0 + nrm(ks[18], (N_GDN, GDN_HEAD_DIM), 0.02),
        'gdn_w_out': nrm(ks[19], (N_GDN, GDN_V_W, D_MODEL), GDN_V_W ** -0.5),
        'final_norm_g': 1.0 + nrm(ks[20], (D_MODEL,), 0.02),
    }


def reference(x_prompt, x_sample, cache_swa_k, cache_swa_v, state_gdn_conv, state_gdn_s,
              c_prompt, c_sample, norm_g, w_mod, b_mod, swa_w_in, swa_sinks, swa_w_out,
              gdn_w_in, gdn_conv_w, gdn_a_log, gdn_dt_bias, gdn_o_norm_g, gdn_w_out, final_norm_g):
    xp, xs = x_prompt, x_sample
    swa_kp, swa_vp, swa_ks, swa_vs = [], [], [], []
    conv_p, s_p, conv_s, s_s = [], [], [], []
    for layer in range(DEPTH):
        j = layer // N_MIXERS
        hp, gp = modulate(xp, c_prompt, norm_g[layer], w_mod[layer], b_mod[layer])
        hs, gs = modulate(xs, c_sample, norm_g[layer], w_mod[layer], b_mod[layer])
        if layer % N_MIXERS == 0:
            op, kp, vp = swa_branch(hp, swa_w_in[j], swa_sinks[j], swa_w_out[j])
            osm, ksm, vsm = swa_branch(hs, swa_w_in[j], swa_sinks[j], swa_w_out[j],
                                       cache_swa_k[j], cache_swa_v[j])
            swa_kp.append(kp)
            swa_vp.append(vp)
            swa_ks.append(ksm)
            swa_vs.append(vsm)
        else:
            bp = hp.shape[0]
            conv0 = jnp.zeros((bp, GDN_CONV - 1, GDN_CONV_CH), hp.dtype)
            s0 = jnp.zeros((bp, GDN_V_HEADS, GDN_HEAD_DIM, GDN_HEAD_DIM), state_gdn_s.dtype)
            op, cp, sp = gdn_branch(hp, conv0, s0, gdn_w_in[j], gdn_conv_w[j], gdn_a_log[j],
                                    gdn_dt_bias[j], gdn_o_norm_g[j], gdn_w_out[j])
            osm, csm, ssm = gdn_branch(hs, state_gdn_conv[j], state_gdn_s[j], gdn_w_in[j], gdn_conv_w[j],
                                       gdn_a_log[j], gdn_dt_bias[j], gdn_o_norm_g[j], gdn_w_out[j])
            conv_p.append(cp)
            s_p.append(sp)
            conv_s.append(csm)
            s_s.append(ssm)
        xp = xp + gp * op
        xs = xs + gs * osm
    y_prompt = rms_norm(xp, final_norm_g)
    y_sample = rms_norm(xs, final_norm_g)
    return (y_prompt, y_sample,
            jnp.stack(swa_kp), jnp.stack(swa_vp), jnp.stack(swa_ks), jnp.stack(swa_vs),
            jnp.stack(conv_p), jnp.stack(s_p), jnp.stack(conv_s), jnp.stack(s_s))
```

```python
import functools
import math

import jax
import jax.numpy as jnp
from jax import lax
from jax.experimental import pallas as pl
from jax.experimental.pallas import tpu as pltpu

F32 = jnp.float32
BF16 = jnp.bfloat16

D_MODEL = 1024
DEPTH = 4
EPS = 1e-6
SWA_HEADS = 16
SWA_KV_HEADS = 4
SWA_HEAD_DIM = 64
SWA_GROUP = SWA_HEADS // SWA_KV_HEADS
WINDOW = 128
SWA_Q_W = SWA_HEADS * SWA_HEAD_DIM
SWA_KV_W = SWA_KV_HEADS * SWA_HEAD_DIM
SWA_SCALE = SWA_HEAD_DIM ** -0.5
GDN_QK_HEADS = 8
GDN_V_HEADS = 16
GDN_HEAD_DIM = 128
GDN_K_W = GDN_QK_HEADS * GDN_HEAD_DIM
GDN_V_W = GDN_V_HEADS * GDN_HEAD_DIM
GDN_CONV = 4
GDN_CONV_CH = 2 * GDN_K_W + GDN_V_W
GDN_CHUNK = 64
LANES = 128
SUBLANES = 8
HALF = LANES // 2
GDN_TILE = 2 * GDN_CHUNK
SAMPLE_ROWS = SUBLANES
VMEM_LIMIT = 56 * 1024 * 1024

_NT = (((1,), (1,)), ((), ()))


def _dot(a, b):
    return jnp.dot(a.astype(BF16), b.astype(BF16), preferred_element_type=F32)


def _dot_nt(a, b):
    return lax.dot_general(a.astype(BF16), b.astype(BF16), _NT, preferred_element_type=F32)


def _dot_f32(a, b):
    return jnp.dot(a, b, preferred_element_type=F32, precision=lax.Precision.HIGHEST)


def _silu(x):
    return x * jax.nn.sigmoid(x)


def _iota(shape, dim):
    return lax.broadcasted_iota(jnp.int32, shape, dim)


def _rms(x, g):
    return x * lax.rsqrt(jnp.mean(x * x, axis=-1, keepdims=True) + EPS) * g


def _params(*sem):
    return pltpu.CompilerParams(dimension_semantics=sem, vmem_limit_bytes=VMEM_LIMIT)


def _mod_kernel(c_ref, w_ref, b_ref, o_ref):
    o_ref[...] = _dot(_silu(c_ref[...]), w_ref[...]) + b_ref[...]


def _modulation(c_all, w_mod, b_mod):
    r, d = c_all.shape
    n = w_mod.shape[-1]
    tn = 512
    return pl.pallas_call(
        _mod_kernel,
        grid=(DEPTH, n // tn),
        in_specs=[pl.BlockSpec((r, d), lambda l, j: (0, 0)),
                  pl.BlockSpec((None, d, tn), lambda l, j: (l, 0, j)),
                  pl.BlockSpec((None, 1, tn), lambda l, j: (l, 0, j))],
        out_specs=pl.BlockSpec((None, r, tn), lambda l, j: (l, 0, j)),
        out_shape=jax.ShapeDtypeStruct((DEPTH, r, n), F32),
        compiler_params=_params("arbitrary", "arbitrary"),
        name="adaln_modulation",
    )(c_all, w_mod, b_mod.reshape(DEPTH, 1, n))


def _inproj_kernel(x_ref, g_ref, sh_ref, sc_ref, w_ref, *o_refs, splits, scales):
    h = _rms(x_ref[...], g_ref[...]) * (1.0 + sc_ref[...]) + sh_ref[...]
    hb = h.astype(BF16)
    for o_ref, (c0, n), s in zip(o_refs, splits, scales):
        for cc in range(0, n, 512):
            w = min(512, n - cc)
            r = jnp.dot(hb, w_ref[:, c0 + cc:c0 + cc + w], preferred_element_type=F32)
            o_ref[:, cc:cc + w] = (r * s if s != 1.0 else r).astype(o_ref.dtype)


def _inproj(x, g, shift, scale, w, splits, scales, rows_per_mod, name):
    m, d = x.shape
    tm = min(256, m, max(rows_per_mod, SUBLANES))
    if rows_per_mod >= tm:
        per = rows_per_mod // tm
        mod_spec = pl.BlockSpec((None, 1, d), lambda i: (i // per, 0, 0))
    else:
        shift, scale = shift.reshape(m, d), scale.reshape(m, d)
        mod_spec = pl.BlockSpec((tm, d), lambda i: (i, 0))
    return pl.pallas_call(
        functools.partial(_inproj_kernel, splits=splits, scales=scales),
        grid=(m // tm,),
        in_specs=[pl.BlockSpec((tm, d), lambda i: (i, 0)),
                  pl.BlockSpec((1, d), lambda i: (0, 0)),
                  mod_spec, mod_spec,
                  pl.BlockSpec(w.shape, lambda i: (0, 0))],
        out_specs=[pl.BlockSpec((tm, n), lambda i: (i, 0)) for _, n in splits],
        out_shape=[jax.ShapeDtypeStruct((m, n), F32) for _, n in splits],
        compiler_params=_params("arbitrary"),
        name=name,
    )(x, g.reshape(1, d), shift, scale, w)


def _finish(a, w_ref, x, gate, fg_ref, final_norm):
    y = x + gate * _dot(a, w_ref[...])
    return _rms(y, fg_ref[...]) if final_norm else y


def _outproj_kernel(a_ref, w_ref, x_ref, gate_ref, fg_ref, o_ref, *, final_norm):
    o_ref[...] = _finish(a_ref[...], w_ref, x_ref[...], gate_ref[...], fg_ref, final_norm)


def _outproj(a, w, x, gate_rows, fg, final_norm, name):
    m, k = a.shape
    d = x.shape[-1]
    tm = min(512, m)
    return pl.pallas_call(
        functools.partial(_outproj_kernel, final_norm=final_norm),
        grid=(m // tm,),
        in_specs=[pl.BlockSpec((tm, k), lambda i: (i, 0)),
                  pl.BlockSpec((k, d), lambda i: (0, 0)),
                  pl.BlockSpec((tm, d), lambda i: (i, 0)),
                  pl.BlockSpec((tm, d), lambda i: (i, 0)),
                  pl.BlockSpec((1, d), lambda i: (0, 0))],
        out_specs=pl.BlockSpec((tm, d), lambda i: (i, 0)),
        out_shape=jax.ShapeDtypeStruct((m, d), F32),
        compiler_params=_params("arbitrary"),
        name=name,
    )(a, w, x, gate_rows, fg.reshape(1, d))


def _swa_block(q, kcat, vcat, mask4, sinks_ref):
    nq = q.shape[0]
    lo = _iota((1, LANES), 1) < HALF
    row = _iota((SWA_GROUP * nq, 1), 0)
    outs = [None] * (SWA_HEADS // 2)
    for pb in range(SWA_KV_HEADS // 2):
        k128 = kcat[:, pb * LANES:(pb + 1) * LANES]
        v128 = vcat[:, pb * LANES:(pb + 1) * LANES]
        for half in range(2):
            kh = 2 * pb + half
            keep = lo if half == 0 else jnp.logical_not(lo)
            parts = []
            for g in range(SWA_GROUP):
                qb = 2 * kh + g // 2
                qg = q[:, qb * LANES:(qb + 1) * LANES]
                if g % 2 != half:
                    qg = pltpu.roll(qg, HALF, axis=1)
                parts.append(jnp.where(keep, qg, 0.0))
            logits = _dot_nt(jnp.concatenate(parts, axis=0), k128)
            sink = jnp.full((SWA_GROUP * nq, 1), sinks_ref[SWA_GROUP * kh], F32)
            for g in range(1, SWA_GROUP):
                sink = jnp.where(row >= g * nq, sinks_ref[SWA_GROUP * kh + g], sink)
            logits = jnp.where(mask4, logits, -jnp.inf)
            m = jnp.maximum(jnp.max(logits, axis=-1, keepdims=True), sink)
            e = jnp.exp(logits - m)
            den = jnp.sum(e, axis=-1, keepdims=True) + jnp.exp(sink - m)
            o4 = _dot(e, v128) * (1.0 / den)
            for g in range(SWA_GROUP):
                piece = o4[g * nq:(g + 1) * nq]
                if g % 2 != half:
                    piece = pltpu.roll(piece, HALF, axis=1)
                piece = jnp.where(lo if g % 2 == 0 else jnp.logical_not(lo), piece, 0.0)
                qb = 2 * kh + g // 2
                outs[qb] = piece if outs[qb] is None else outs[qb] + piece
    return jnp.concatenate(outs, axis=1)


def _band_mask(nq, nk, jmin):
    i = _iota((SWA_GROUP * nq, nk), 0) % nq
    j = _iota((SWA_GROUP * nq, nk), 1)
    return (j >= i) & (j <= i + WINDOW) & (j >= jmin)


def _swa_prompt_kernel(sinks_ref, q_ref, kp_ref, vp_ref, k_ref, v_ref, z_ref, x_ref, gate_ref, w_ref, fg_ref,
                       o_ref, kbuf, vbuf, abuf, *, tq):
    t = pl.program_id(1)
    kbuf[0:WINDOW] = kp_ref[...]
    kbuf[WINDOW:WINDOW + tq] = k_ref[...]
    vbuf[0:WINDOW] = vp_ref[...]
    vbuf[WINDOW:WINDOW + tq] = v_ref[...]

    def body(sb, carry):
        r = pl.multiple_of(sb * WINDOW, WINDOW)
        jmin = jnp.where((t == 0) & (sb == 0), WINDOW, 0)
        mask4 = _band_mask(WINDOW, 2 * WINDOW, jmin)
        abuf[pl.ds(r, WINDOW), :] = _swa_block(q_ref[pl.ds(r, WINDOW), :], kbuf[pl.ds(r, 2 * WINDOW), :],
                                               vbuf[pl.ds(r, 2 * WINDOW), :], mask4, sinks_ref)
        return carry

    lax.fori_loop(0, tq // WINDOW, body, 0)
    a = abuf[...] * _silu(z_ref[...])
    o_ref[...] = _finish(a, w_ref, x_ref[...], gate_ref[...], fg_ref, False)


def _swa_prompt(sinks, q, k, v, z, x, gate, w_out, fg, b, t):
    d = x.shape[-1]
    tq = min(512, t)
    nt = t // tq
    sub = tq // WINDOW
    row = lambda bi, ti: (bi * nt + ti, 0)
    prev = lambda bi, ti: (bi * nt * sub + jnp.maximum(ti * sub - 1, 0), 0)
    const = lambda bi, ti: (0, 0)
    return pl.pallas_call(
        functools.partial(_swa_prompt_kernel, tq=tq),
        grid=(b, nt),
        in_specs=[pl.BlockSpec(memory_space=pltpu.SMEM),
                  pl.BlockSpec((tq, SWA_Q_W), row),
                  pl.BlockSpec((WINDOW, SWA_KV_W), prev),
                  pl.BlockSpec((WINDOW, SWA_KV_W), prev),
                  pl.BlockSpec((tq, SWA_KV_W), row),
                  pl.BlockSpec((tq, SWA_KV_W), row),
                  pl.BlockSpec((tq, SWA_Q_W), row),
                  pl.BlockSpec((tq, d), row),
                  pl.BlockSpec((None, 1, d), lambda bi, ti: (bi, 0, 0)),
                  pl.BlockSpec(w_out.shape, const),
                  pl.BlockSpec((1, d), const)],
        out_specs=pl.BlockSpec((tq, d), row),
        out_shape=jax.ShapeDtypeStruct(x.shape, F32),
        scratch_shapes=[pltpu.VMEM((WINDOW + tq, SWA_KV_W), F32),
                        pltpu.VMEM((WINDOW + tq, SWA_KV_W), F32),
                        pltpu.VMEM((tq, SWA_Q_W), F32)],
        compiler_params=_params("arbitrary", "arbitrary"),
        name="swa_prompt",
    )(sinks, q, k, v, k, v, z, x, gate, w_out, fg.reshape(1, d))


def _swa_sample_kernel(sinks_ref, q_ref, k_ref, v_ref, z_ref, ck_ref, cv_ref, a_ref, nk_ref, nv_ref, kall, vall,
                       *, bb, new_rows):
    nq = SAMPLE_ROWS
    mask4 = _band_mask(nq, 2 * WINDOW, 0)
    pad = jnp.zeros((WINDOW - nq, SWA_KV_W), F32)
    for bi in range(bb):
        rows = slice(bi * nq, (bi + 1) * nq)
        kc, vc, kn, vn = ck_ref[bi], cv_ref[bi], k_ref[rows, :], v_ref[rows, :]
        o = _swa_block(q_ref[rows, :], jnp.concatenate([kc, kn, pad], axis=0),
                       jnp.concatenate([vc, vn, pad], axis=0), mask4, sinks_ref)
        a_ref[rows, :] = o * _silu(z_ref[rows, :])
        kall[0:WINDOW] = kc
        kall[WINDOW:WINDOW + nq] = kn
        vall[0:WINDOW] = vc
        vall[WINDOW:WINDOW + nq] = vn
        nk_ref[bi] = kall[new_rows:new_rows + WINDOW]
        nv_ref[bi] = vall[new_rows:new_rows + WINDOW]


def _swa_sample(sinks, q, k, v, z, ck, cv, new_rows):
    db = ck.shape[0]
    bb = min(8, db)
    nq = SAMPLE_ROWS
    row = lambda i: (i, 0)
    cache = lambda i: (i, 0, 0)
    return pl.pallas_call(
        functools.partial(_swa_sample_kernel, bb=bb, new_rows=new_rows),
        grid=(db // bb,),
        in_specs=[pl.BlockSpec(memory_space=pltpu.SMEM),
                  pl.BlockSpec((bb * nq, SWA_Q_W), row),
                  pl.BlockSpec((bb * nq, SWA_KV_W), row),
                  pl.BlockSpec((bb * nq, SWA_KV_W), row),
                  pl.BlockSpec((bb * nq, SWA_Q_W), row),
                  pl.BlockSpec((bb, WINDOW, SWA_KV_W), cache),
                  pl.BlockSpec((bb, WINDOW, SWA_KV_W), cache)],
        out_specs=[pl.BlockSpec((bb * nq, SWA_Q_W), row),
                   pl.BlockSpec((bb, WINDOW, SWA_KV_W), cache),
                   pl.BlockSpec((bb, WINDOW, SWA_KV_W), cache)],
        out_shape=[jax.ShapeDtypeStruct(q.shape, F32),
                   jax.ShapeDtypeStruct(ck.shape, F32),
                   jax.ShapeDtypeStruct(cv.shape, F32)],
        scratch_shapes=[pltpu.VMEM((WINDOW + nq, SWA_KV_W), F32),
                        pltpu.VMEM((WINDOW + nq, SWA_KV_W), F32)],
        compiler_params=_params("arbitrary"),
        name="swa_sample",
    )(sinks, q, k, v, z, ck, cv)


def _pair_mul(a2, b2, lo):
    bd = jnp.concatenate([jnp.where(lo, b2, 0.0), jnp.where(lo, 0.0, b2)], axis=0)
    return _dot(a2, bd)


def _gdn_kernel(xqkv_ref, z_ref, ab_ref, x_ref, gate_ref, conv0_ref, s0_ref, cw_ref, alog_ref, dtb_ref, ong_ref,
                w_ref, fg_ref, o_ref, s_ref, xs, qkv, obuf, *, rows_in, nchunks, tvalid, final_norm):
    R, C, HD = GDN_TILE, GDN_CHUNK, GDN_HEAD_DIM
    t = pl.program_id(1)
    head = GDN_CONV - 1
    base = SUBLANES - head

    @pl.when(t == 0)
    def _():
        s_ref[...] = s0_ref[...]
        xs[0:base, :] = jnp.zeros((base, GDN_CONV_CH), F32)
        xs[base:SUBLANES, :] = conv0_ref[...]

    xs[SUBLANES:SUBLANES + rows_in, :] = xqkv_ref[...]
    if rows_in < R:
        xs[SUBLANES + rows_in:SUBLANES + R, :] = jnp.zeros((R - rows_in, GDN_CONV_CH), F32)

    for cb in range(GDN_CONV_CH // GDN_K_W):
        cols = slice(cb * GDN_K_W, (cb + 1) * GDN_K_W)
        y = xs[base:base + R, cols] * cw_ref[0:1, cols]
        for w in range(1, GDN_CONV):
            y = y + xs[base + w:base + w + R, cols] * cw_ref[w:w + 1, cols]
        y = _silu(y)
        if cb < 2:
            for p in range(GDN_QK_HEADS):
                blk = y[:, p * HD:(p + 1) * HD]
                inv = lax.rsqrt(jnp.sum(blk * blk, axis=-1, keepdims=True) + EPS)
                if cb == 0:
                    inv = inv * HD ** -0.5
                qkv[:, cb * GDN_K_W + p * HD:cb * GDN_K_W + (p + 1) * HD] = blk * inv
        else:
            qkv[:, cols] = y
    if rows_in == R:
        xs[base:SUBLANES, :] = xs[base + R:SUBLANES + R, :]

    ab = ab_ref[...]
    if rows_in < R:
        ab = jnp.concatenate([ab, jnp.zeros((R - rows_in, LANES), F32)], axis=0)
    g = -jnp.exp(alog_ref[...]) * jax.nn.softplus(ab + dtb_ref[...])
    beta = pltpu.roll(jax.nn.sigmoid(ab), LANES - GDN_V_HEADS, axis=1)
    if tvalid < R:
        live = _iota((R, 1), 0) < tvalid
        g = jnp.where(live, g, 0.0)
        beta = jnp.where(live, beta, 0.0)
    ri, ci = _iota((R, R), 0), _iota((R, R), 1)
    same = (ri // C) == (ci // C)
    gc = _dot_f32(jnp.where(same & (ci <= ri), 1.0, 0.0), g)
    gl = _dot_f32(jnp.where(same, 1.0, 0.0), g)
    egc = jnp.exp(gc)
    bgc = beta * egc
    gcT, glT = gc.T, gl.T
    dlT = glT - gcT

    lane = _iota((1, LANES), 1)
    lo = lane < HALF
    jj = lane % C
    ii = _iota((C, 1), 0)
    incl, strict = ii >= jj, ii > jj
    eye2 = jnp.where(ii == jj, 1.0, 0.0)
    zc = jnp.zeros((C, HD), F32)

    def col2(arr, r0, h0):
        return jnp.where(lo, arr[r0:r0 + C, h0:h0 + 1], arr[r0:r0 + C, h0 + 1:h0 + 2])

    for p in range(GDN_QK_HEADS):
        h0 = 2 * p
        kT = qkv[:, GDN_K_W + p * HD:GDN_K_W + (p + 1) * HD].T
        for c in range(nchunks):
            r0 = c * C
            in_chunk = (lane // C) == c
            qc = qkv[r0:r0 + C, p * HD:(p + 1) * HD]
            kc = qkv[r0:r0 + C, GDN_K_W + p * HD:GDN_K_W + (p + 1) * HD]
            m = _dot_nt(jnp.concatenate([kc, qc], axis=0), jnp.concatenate([kc, kc], axis=0))
            r_a, r_b = gcT[h0:h0 + 1, :], gcT[h0 + 1:h0 + 2, :]
            if c == 0:
                row_gc = jnp.where(lo, r_a, pltpu.roll(r_b, HALF, axis=1))
            else:
                row_gc = jnp.where(lo, pltpu.roll(r_a, HALF, axis=1), r_b)
            decay = jnp.exp(jnp.where(incl, col2(gc, r0, h0) - row_gc, -jnp.inf))
            low = jnp.where(strict, m[0:C] * decay, 0.0) * col2(beta, r0, h0)
            qk = m[C:2 * C] * decay
            inv = eye2 - jnp.where((ii // 2) == (jj // 2), low, 0.0)
            s = 2
            while s < C:
                below = ((ii // s) % 2 == 1) & ((jj // s) == (ii // s) - 1)
                inv = inv - _pair_mul(inv, _pair_mul(jnp.where(below, low, 0.0), inv, lo), lo)
                s *= 2
            rhs = []
            for idx in range(2):
                h = h0 + idx
                vh = qkv[r0:r0 + C, 2 * GDN_K_W + h * HD:2 * GDN_K_W + (h + 1) * HD]
                piece = [vh * beta[r0:r0 + C, h:h + 1], kc * bgc[r0:r0 + C, h:h + 1]]
                rhs.append(jnp.concatenate(piece + [zc, zc] if idx == 0 else [zc, zc] + piece, axis=1))
            sol = _dot(inv, jnp.concatenate(rhs, axis=0))
            us, qs = [], []
            for idx in range(2):
                h = h0 + idx
                w_v = sol[:, 2 * idx * HD:(2 * idx + 1) * HD]
                w_k = sol[:, (2 * idx + 1) * HD:(2 * idx + 2) * HD]
                q_dec = qc * egc[r0:r0 + C, h:h + 1]
                a = _dot(jnp.concatenate([w_k, q_dec], axis=0), s_ref[h])
                us.append(w_v - a[0:C])
                qs.append(a[C:2 * C])
            u2 = jnp.concatenate([jnp.concatenate([us[0], zc], axis=1),
                                  jnp.concatenate([zc, us[1]], axis=1)], axis=0)
            obuf[r0:r0 + C, h0 * HD:(h0 + 2) * HD] = jnp.concatenate(qs, axis=1) + _dot(qk, u2)
            for idx in range(2):
                h = h0 + idx
                dl = dlT[h:h + 1, :]
                k_dec_t = kT * jnp.exp(jnp.where(in_chunk, dl, -jnp.inf))
                u_ext = jnp.concatenate([us[idx], zc] if c == 0 else [zc, us[idx]], axis=0)
                glr = glT[h:h + 1, :]
                g_tot = jnp.exp(jnp.where(lo, glr, pltpu.roll(glr, HALF, axis=1)) if c == 0
                                else jnp.where(lo, pltpu.roll(glr, HALF, axis=1), glr))
                s_ref[h] = s_ref[h] * g_tot + _dot(k_dec_t, u_ext)

    o = obuf[0:rows_in, :]
    parts = [_rms(o[:, h * HD:(h + 1) * HD], ong_ref[...]) for h in range(GDN_V_HEADS)]
    a = jnp.concatenate(parts, axis=1) * _silu(z_ref[...])
    o_ref[...] = _finish(a, w_ref, x_ref[...], gate_ref[...], fg_ref, final_norm)


def _gdn(xqkv, z, ab, x, gate, conv0, s0, cw, alog, dtb, ong, w_out, fg, b, rows_in, per_row_gate, nchunks, tvalid,
         final_norm, name):
    d = x.shape[-1]
    nt = x.shape[0] // (b * rows_in)
    row = lambda bi, ti: (bi * nt + ti, 0)
    const = lambda bi, ti: (0, 0)
    per_b3 = lambda bi, ti: (bi, 0, 0)
    per_b4 = lambda bi, ti: (bi, 0, 0, 0)
    gate_spec = pl.BlockSpec((rows_in, d), row) if per_row_gate else pl.BlockSpec((None, 1, d), per_b3)
    vec = lambda a: jnp.pad(a.astype(F32), (0, LANES - a.shape[0])).reshape(1, LANES)
    state = (GDN_V_HEADS, GDN_HEAD_DIM, GDN_HEAD_DIM)
    return pl.pallas_call(
        functools.partial(_gdn_kernel, rows_in=rows_in, nchunks=nchunks, tvalid=tvalid, final_norm=final_norm),
        grid=(b, nt),
        in_specs=[pl.BlockSpec((rows_in, GDN_CONV_CH), row),
                  pl.BlockSpec((rows_in, GDN_V_W), row),
                  pl.BlockSpec((rows_in, LANES), row),
                  pl.BlockSpec((rows_in, d), row),
                  gate_spec,
                  pl.BlockSpec((None, GDN_CONV - 1, GDN_CONV_CH), per_b3),
                  pl.BlockSpec((None,) + state, per_b4),
                  pl.BlockSpec(cw.shape, const),
                  pl.BlockSpec((1, LANES), const),
                  pl.BlockSpec((1, LANES), const),
                  pl.BlockSpec((1, GDN_HEAD_DIM), const),
                  pl.BlockSpec(w_out.shape, const),
                  pl.BlockSpec((1, d), const)],
        out_specs=[pl.BlockSpec((rows_in, d), row),
                   pl.BlockSpec((None,) + state, per_b4)],
        out_shape=[jax.ShapeDtypeStruct(x.shape, F32),
                   jax.ShapeDtypeStruct((b,) + state, F32)],
        scratch_shapes=[pltpu.VMEM((SUBLANES + GDN_TILE, GDN_CONV_CH), F32),
                        pltpu.VMEM((GDN_TILE, GDN_CONV_CH), F32),
                        pltpu.VMEM((GDN_TILE, GDN_V_W), F32)],
        compiler_params=_params("arbitrary", "arbitrary"),
        name=name,
    )(xqkv, z, ab, x, gate, conv0, s0, cw, vec(alog), vec(dtb), ong.reshape(1, GDN_HEAD_DIM), w_out,
      fg.reshape(1, d))


def kernel(x_prompt, x_sample, cache_swa_k, cache_swa_v, state_gdn_conv, state_gdn_s, c_prompt, c_sample, norm_g,
           w_mod, b_mod, swa_w_in, swa_sinks, swa_w_out, gdn_w_in, gdn_conv_w, gdn_a_log, gdn_dt_bias, gdn_o_norm_g,
           gdn_w_out, final_norm_g):
    b, t, d = x_prompt.shape
    db, dt, _ = x_sample.shape
    assert d == D_MODEL and t % GDN_TILE == 0 and GDN_CONV - 1 <= dt <= SAMPLE_ROWS
    sr = SAMPLE_ROWS

    mod = _modulation(jnp.concatenate([c_prompt, c_sample], axis=0), w_mod, b_mod)
    xp = x_prompt.reshape(b * t, d)
    xs = jnp.pad(x_sample, ((0, 0), (0, sr - dt), (0, 0))).reshape(db * sr, d)

    swa_in = swa_w_in.astype(BF16)
    swa_out = swa_w_out.astype(BF16)
    gdn_pad = LANES - 2 * GDN_V_HEADS
    gdn_in = jnp.pad(gdn_w_in, ((0, 0), (0, 0), (0, gdn_pad))).astype(BF16)
    gdn_out = gdn_w_out.astype(BF16)
    swa_splits = ((0, SWA_Q_W), (SWA_Q_W, SWA_KV_W), (SWA_Q_W + SWA_KV_W, SWA_KV_W), (SWA_Q_W + 2 * SWA_KV_W, SWA_Q_W))
    swa_scales = (SWA_SCALE, 1.0, 1.0, 1.0)
    gdn_splits = ((0, GDN_CONV_CH), (GDN_CONV_CH, GDN_V_W), (GDN_CONV_CH + GDN_V_W, LANES))
    gdn_scales = (1.0, 1.0, 1.0)

    kp_out, vp_out, ks_out, vs_out, cp_out, sp_out, cs_out, ss_out = [], [], [], [], [], [], [], []
    for layer in range(DEPTH):
        j = layer // 2
        last = layer == DEPTH - 1
        shift, scale, gate = (mod[layer, :, i * d:(i + 1) * d] for i in range(3))
        p3 = lambda a: a[:b].reshape(b, 1, d)
        s_rows = lambda a: jnp.repeat(a[b:], sr, axis=0)
        g_l = norm_g[layer]
        if layer % 2 == 0:
            q, k, v, z = _inproj(xp, g_l, p3(shift), p3(scale), swa_in[j], swa_splits, swa_scales, t, "swa_inproj_p")
            xp = _swa_prompt(swa_sinks[j], q, k, v, z, xp, p3(gate), swa_out[j], final_norm_g, b, t)
            keep = min(WINDOW, t)
            kp_out.append(k.reshape(b, t, SWA_KV_HEADS, SWA_HEAD_DIM)[:, -keep:])
            vp_out.append(v.reshape(b, t, SWA_KV_HEADS, SWA_HEAD_DIM)[:, -keep:])

            q, k, v, z = _inproj(xs, g_l, s_rows(shift), s_rows(scale), swa_in[j], swa_splits, swa_scales, 1,
                                 "swa_inproj_s")
            a, nk, nv = _swa_sample(swa_sinks[j], q, k, v, z, cache_swa_k[j].reshape(db, WINDOW, SWA_KV_W),
                                    cache_swa_v[j].reshape(db, WINDOW, SWA_KV_W), dt)
            xs = _outproj(a, swa_out[j], xs, s_rows(gate), final_norm_g, False, "swa_outproj_s")
            ks_out.append(nk.reshape(db, WINDOW, SWA_KV_HEADS, SWA_HEAD_DIM))
            vs_out.append(nv.reshape(db, WINDOW, SWA_KV_HEADS, SWA_HEAD_DIM))
        else:
            cw, alog, dtb, ong = gdn_conv_w[j], gdn_a_log[j], gdn_dt_bias[j], gdn_o_norm_g[j]
            xqkv, z, ab = _inproj(xp, g_l, p3(shift), p3(scale), gdn_in[j], gdn_splits, gdn_scales, t, "gdn_inproj_p")
            conv0 = jnp.zeros((b, GDN_CONV - 1, GDN_CONV_CH), F32)
            s0 = jnp.zeros((b, GDN_V_HEADS, GDN_HEAD_DIM, GDN_HEAD_DIM), F32)
            xp, s_new = _gdn(xqkv, z, ab, xp, p3(gate), conv0, s0, cw, alog, dtb, ong, gdn_out[j], final_norm_g, b,
                             GDN_TILE, False, 2, GDN_TILE, last, "gdn_prompt")
            cp_out.append(xqkv.reshape(b, t, GDN_CONV_CH)[:, -(GDN_CONV - 1):])
            sp_out.append(s_new)

            xqkv, z, ab = _inproj(xs, g_l, s_rows(shift), s_rows(scale), gdn_in[j], gdn_splits, gdn_scales, 1,
                                  "gdn_inproj_s")
            xs, s_new = _gdn(xqkv, z, ab, xs, s_rows(gate), state_gdn_conv[j], state_gdn_s[j], cw, alog, dtb, ong,
                             gdn_out[j], final_norm_g, db, sr, True, 1, dt, last, "gdn_sample")
            cs_out.append(xqkv.reshape(db, sr, GDN_CONV_CH)[:, dt - (GDN_CONV - 1):dt])
            ss_out.append(s_new)

    y_prompt = xp.reshape(b, t, d)
    y_sample = xs.reshape(db, sr, d)[:, :dt]
    return (y_prompt, y_sample, jnp.stack(kp_out), jnp.stack(vp_out), jnp.stack(ks_out), jnp.stack(vs_out),
            jnp.stack(cp_out), jnp.stack(sp_out), jnp.stack(cs_out), jnp.stack(ss_out))
```

```python
import functools
import math

import jax
import jax.numpy as jnp
from jax import lax
from jax.experimental import pallas as pl
from jax.experimental.pallas import tpu as pltpu

F32 = jnp.float32
BF16 = jnp.bfloat16

D_MODEL = 1024
DEPTH = 4
EPS = 1e-6
SWA_HEADS = 16
SWA_KV_HEADS = 4
SWA_HEAD_DIM = 64
SWA_GROUP = SWA_HEADS // SWA_KV_HEADS
WINDOW = 128
SWA_Q_W = SWA_HEADS * SWA_HEAD_DIM
SWA_KV_W = SWA_KV_HEADS * SWA_HEAD_DIM
SWA_SCALE = SWA_HEAD_DIM ** -0.5
GDN_QK_HEADS = 8
GDN_V_HEADS = 16
GDN_HEAD_DIM = 128
GDN_K_W = GDN_QK_HEADS * GDN_HEAD_DIM
GDN_V_W = GDN_V_HEADS * GDN_HEAD_DIM
GDN_CONV = 4
GDN_CONV_CH = 2 * GDN_K_W + GDN_V_W
GDN_CHUNK = 64
LANES = 128
SUBLANES = 8
HALF = LANES // 2
GDN_TILE = 2 * GDN_CHUNK
SAMPLE_ROWS = SUBLANES
VMEM_LIMIT = 56 * 1024 * 1024

_NT = (((1,), (1,)), ((), ()))


def _dot(a, b):
    return jnp.dot(a.astype(BF16), b.astype(BF16), preferred_element_type=F32)


def _dot_nt(a, b):
    return lax.dot_general(a.astype(BF16), b.astype(BF16), _NT, preferred_element_type=F32)


def _dot_f32(a, b):
    return jnp.dot(a, b, preferred_element_type=F32, precision=lax.Precision.HIGHEST)


def _silu(x):
    return x * jax.nn.sigmoid(x)


def _iota(shape, dim):
    return lax.broadcasted_iota(jnp.int32, shape, dim)


def _rms(x, g):
    return x * lax.rsqrt(jnp.mean(x * x, axis=-1, keepdims=True) + EPS) * g


def _params(*sem):
    return pltpu.CompilerParams(dimension_semantics=sem, vmem_limit_bytes=VMEM_LIMIT)


def _mod_kernel(c_ref, w_ref, b_ref, o_ref):
    o_ref[...] = _dot(_silu(c_ref[...]), w_ref[...]) + b_ref[...]


def _modulation(c_all, w_mod, b_mod):
    r, d = c_all.shape
    n = w_mod.shape[-1]
    tn = 512
    return pl.pallas_call(
        _mod_kernel,
        grid=(DEPTH, n // tn),
        in_specs=[pl.BlockSpec((r, d), lambda l, j: (0, 0)),
                  pl.BlockSpec((None, d, tn), lambda l, j: (l, 0, j)),
                  pl.BlockSpec((None, 1, tn), lambda l, j: (l, 0, j))],
        out_specs=pl.BlockSpec((None, r, tn), lambda l, j: (l, 0, j)),
        out_shape=jax.ShapeDtypeStruct((DEPTH, r, n), F32),
        compiler_params=_params("arbitrary", "arbitrary"),
        name="adaln_modulation",
    )(c_all, w_mod, b_mod.reshape(DEPTH, 1, n))


def _inproj_kernel(x_ref, g_ref, sh_ref, sc_ref, w_ref, *o_refs, splits, scales):
    h = _rms(x_ref[...], g_ref[...]) * (1.0 + sc_ref[...]) + sh_ref[...]
    hb = h.astype(BF16)
    for o_ref, (c0, n), s in zip(o_refs, splits, scales):
        for cc in range(0, n, 512):
            w = min(512, n - cc)
            r = jnp.dot(hb, w_ref[:, c0 + cc:c0 + cc + w], preferred_element_type=F32)
            o_ref[:, cc:cc + w] = (r * s if s != 1.0 else r).astype(o_ref.dtype)


def _inproj(x, g, shift, scale, w, splits, scales, rows_per_mod, name):
    m, d = x.shape
    tm = min(256, m) if rows_per_mod == 1 else min(256, rows_per_mod)
    if rows_per_mod >= tm:
        per = rows_per_mod // tm
        mod_spec = pl.BlockSpec((None, 1, d), lambda i: (i // per, 0, 0))
    else:
        shift, scale = shift.reshape(m, d), scale.reshape(m, d)
        mod_spec = pl.BlockSpec((tm, d), lambda i: (i, 0))
    return pl.pallas_call(
        functools.partial(_inproj_kernel, splits=splits, scales=scales),
        grid=(m // tm,),
        in_specs=[pl.BlockSpec((tm, d), lambda i: (i, 0)),
                  pl.BlockSpec((1, d), lambda i: (0, 0)),
                  mod_spec, mod_spec,
                  pl.BlockSpec(w.shape, lambda i: (0, 0))],
        out_specs=[pl.BlockSpec((tm, n), lambda i: (i, 0)) for _, n in splits],
        out_shape=[jax.ShapeDtypeStruct((m, n), F32) for _, n in splits],
        compiler_params=_params("arbitrary"),
        name=name,
    )(x, g.reshape(1, d), shift, scale, w)


def _finish(a, w_ref, x, gate, fg_ref, final_norm):
    y = x + gate * _dot(a, w_ref[...])
    return _rms(y, fg_ref[...]) if final_norm else y


def _outproj_kernel(a_ref, w_ref, x_ref, gate_ref, fg_ref, o_ref, *, final_norm):
    o_ref[...] = _finish(a_ref[...], w_ref, x_ref[...], gate_ref[...], fg_ref, final_norm)


def _outproj(a, w, x, gate_rows, fg, final_norm, name):
    m, k = a.shape
    d = x.shape[-1]
    tm = min(512, m)
    return pl.pallas_call(
        functools.partial(_outproj_kernel, final_norm=final_norm),
        grid=(m // tm,),
        in_specs=[pl.BlockSpec((tm, k), lambda i: (i, 0)),
                  pl.BlockSpec((k, d), lambda i: (0, 0)),
                  pl.BlockSpec((tm, d), lambda i: (i, 0)),
                  pl.BlockSpec((tm, d), lambda i: (i, 0)),
                  pl.BlockSpec((1, d), lambda i: (0, 0))],
        out_specs=pl.BlockSpec((tm, d), lambda i: (i, 0)),
        out_shape=jax.ShapeDtypeStruct((m, d), F32),
        compiler_params=_params("arbitrary"),
        name=name,
    )(a, w, x, gate_rows, fg.reshape(1, d))


def _swa_block(q, kcat, vcat, mask4, sinks_ref):
    nq = q.shape[0]
    lo = _iota((1, LANES), 1) < HALF
    row = _iota((SWA_GROUP * nq, 1), 0)
    outs = [None] * (SWA_HEADS // 2)
    for pb in range(SWA_KV_HEADS // 2):
        k128 = kcat[:, pb * LANES:(pb + 1) * LANES]
        v128 = vcat[:, pb * LANES:(pb + 1) * LANES]
        for half in range(2):
            kh = 2 * pb + half
            keep = lo if half == 0 else jnp.logical_not(lo)
            parts = []
            for g in range(SWA_GROUP):
                qb = 2 * kh + g // 2
                qg = q[:, qb * LANES:(qb + 1) * LANES]
                if g % 2 != half:
                    qg = pltpu.roll(qg, HALF, axis=1)
                parts.append(jnp.where(keep, qg, 0.0))
            logits = _dot_nt(jnp.concatenate(parts, axis=0), k128)
            sink = jnp.full((SWA_GROUP * nq, 1), sinks_ref[SWA_GROUP * kh], F32)
            for g in range(1, SWA_GROUP):
                sink = jnp.where(row >= g * nq, sinks_ref[SWA_GROUP * kh + g], sink)
            logits = jnp.where(mask4, logits, -jnp.inf)
            m = jnp.maximum(jnp.max(logits, axis=-1, keepdims=True), sink)
            e = jnp.exp(logits - m)
            den = jnp.sum(e, axis=-1, keepdims=True) + jnp.exp(sink - m)
            o4 = _dot(e, v128) * (1.0 / den)
            for g in range(SWA_GROUP):
                piece = o4[g * nq:(g + 1) * nq]
                if g % 2 != half:
                    piece = pltpu.roll(piece, HALF, axis=1)
                piece = jnp.where(lo if g % 2 == 0 else jnp.logical_not(lo), piece, 0.0)
                qb = 2 * kh + g // 2
                outs[qb] = piece if outs[qb] is None else outs[qb] + piece
    return jnp.concatenate(outs, axis=1)


def _band_mask(nq, nk, jmin):
    i = _iota((SWA_GROUP * nq, nk), 0) % nq
    j = _iota((SWA_GROUP * nq, nk), 1)
    return (j >= i) & (j <= i + WINDOW) & (j >= jmin)


def _swa_prompt_kernel(sinks_ref, q_ref, kp_ref, vp_ref, k_ref, v_ref, z_ref, x_ref, gate_ref, w_ref, fg_ref,
                       o_ref, kbuf, vbuf, abuf, *, tq):
    t = pl.program_id(1)
    kbuf[0:WINDOW] = kp_ref[...]
    kbuf[WINDOW:WINDOW + tq] = k_ref[...]
    vbuf[0:WINDOW] = vp_ref[...]
    vbuf[WINDOW:WINDOW + tq] = v_ref[...]

    def body(sb, carry):
        r = pl.multiple_of(sb * WINDOW, WINDOW)
        jmin = jnp.where((t == 0) & (sb == 0), WINDOW, 0)
        mask4 = _band_mask(WINDOW, 2 * WINDOW, jmin)
        abuf[pl.ds(r, WINDOW), :] = _swa_block(q_ref[pl.ds(r, WINDOW), :], kbuf[pl.ds(r, 2 * WINDOW), :],
                                               vbuf[pl.ds(r, 2 * WINDOW), :], mask4, sinks_ref)
        return carry

    lax.fori_loop(0, tq // WINDOW, body, 0)
    a = abuf[...] * _silu(z_ref[...])
    o_ref[...] = _finish(a, w_ref, x_ref[...], gate_ref[...], fg_ref, False)


def _swa_prompt(sinks, q, k, v, z, x, gate, w_out, fg, b, t):
    d = x.shape[-1]
    tq = min(512, t)
    nt = t // tq
    sub = tq // WINDOW
    row = lambda bi, ti: (bi * nt + ti, 0)
    prev = lambda bi, ti: (bi * nt * sub + jnp.maximum(ti * sub - 1, 0), 0)
    const = lambda bi, ti: (0, 0)
    return pl.pallas_call(
        functools.partial(_swa_prompt_kernel, tq=tq),
        grid=(b, nt),
        in_specs=[pl.BlockSpec(memory_space=pltpu.SMEM),
                  pl.BlockSpec((tq, SWA_Q_W), row),
                  pl.BlockSpec((WINDOW, SWA_KV_W), prev),
                  pl.BlockSpec((WINDOW, SWA_KV_W), prev),
                  pl.BlockSpec((tq, SWA_KV_W), row),
                  pl.BlockSpec((tq, SWA_KV_W), row),
                  pl.BlockSpec((tq, SWA_Q_W), row),
                  pl.BlockSpec((tq, d), row),
                  pl.BlockSpec((None, 1, d), lambda bi, ti: (bi, 0, 0)),
                  pl.BlockSpec(w_out.shape, const),
                  pl.BlockSpec((1, d), const)],
        out_specs=pl.BlockSpec((tq, d), row),
        out_shape=jax.ShapeDtypeStruct(x.shape, F32),
        scratch_shapes=[pltpu.VMEM((WINDOW + tq, SWA_KV_W), F32),
                        pltpu.VMEM((WINDOW + tq, SWA_KV_W), F32),
                        pltpu.VMEM((tq, SWA_Q_W), F32)],
        compiler_params=_params("arbitrary", "arbitrary"),
        name="swa_prompt",
    )(sinks, q, k, v, k, v, z, x, gate, w_out, fg.reshape(1, d))


def _swa_sample_kernel(sinks_ref, q_ref, k_ref, v_ref, z_ref, ck_ref, cv_ref, a_ref, nk_ref, nv_ref, kall, vall,
                       *, bb, new_rows):
    nq = SAMPLE_ROWS
    mask4 = _band_mask(nq, 2 * WINDOW, 0)
    pad = jnp.zeros((WINDOW - nq, SWA_KV_W), F32)
    for bi in range(bb):
        rows = slice(bi * nq, (bi + 1) * nq)
        kc, vc, kn, vn = ck_ref[bi], cv_ref[bi], k_ref[rows, :], v_ref[rows, :]
        o = _swa_block(q_ref[rows, :], jnp.concatenate([kc, kn, pad], axis=0),
                       jnp.concatenate([vc, vn, pad], axis=0), mask4, sinks_ref)
        a_ref[rows, :] = o * _silu(z_ref[rows, :])
        kall[0:WINDOW] = kc
        kall[WINDOW:WINDOW + nq] = kn
        vall[0:WINDOW] = vc
        vall[WINDOW:WINDOW + nq] = vn
        nk_ref[bi] = kall[new_rows:new_rows + WINDOW]
        nv_ref[bi] = vall[new_rows:new_rows + WINDOW]


def _swa_sample(sinks, q, k, v, z, ck, cv, new_rows):
    db = ck.shape[0]
    bb = min(8, db)
    nq = SAMPLE_ROWS
    row = lambda i: (i, 0)
    cache = lambda i: (i, 0, 0)
    return pl.pallas_call(
        functools.partial(_swa_sample_kernel, bb=bb, new_rows=new_rows),
        grid=(db // bb,),
        in_specs=[pl.BlockSpec(memory_space=pltpu.SMEM),
                  pl.BlockSpec((bb * nq, SWA_Q_W), row),
                  pl.BlockSpec((bb * nq, SWA_KV_W), row),
                  pl.BlockSpec((bb * nq, SWA_KV_W), row),
                  pl.BlockSpec((bb * nq, SWA_Q_W), row),
                  pl.BlockSpec((bb, WINDOW, SWA_KV_W), cache),
                  pl.BlockSpec((bb, WINDOW, SWA_KV_W), cache)],
        out_specs=[pl.BlockSpec((bb * nq, SWA_Q_W), row),
                   pl.BlockSpec((bb, WINDOW, SWA_KV_W), cache),
                   pl.BlockSpec((bb, WINDOW, SWA_KV_W), cache)],
        out_shape=[jax.ShapeDtypeStruct(q.shape, F32),
                   jax.ShapeDtypeStruct(ck.shape, F32),
                   jax.ShapeDtypeStruct(cv.shape, F32)],
        scratch_shapes=[pltpu.VMEM((WINDOW + nq, SWA_KV_W), F32),
                        pltpu.VMEM((WINDOW + nq, SWA_KV_W), F32)],
        compiler_params=_params("arbitrary"),
        name="swa_sample",
    )(sinks, q, k, v, z, ck, cv)


def _pair_mul(a2, b2, lo):
    b16 = b2.astype(BF16)
    zero = jnp.zeros_like(b16)
    bd = jnp.concatenate([jnp.where(lo, b16, zero), jnp.where(lo, zero, b16)], axis=0)
    return _dot(a2, bd)


def _gdn_kernel(xqkv_ref, z_ref, ab_ref, x_ref, gate_ref, conv0_ref, s0_ref, cw_ref, alog_ref, dtb_ref, ong_ref,
                w_ref, fg_ref, o_ref, s_ref, xs, qkv, obuf, *, rows_in, nchunks, tvalid, final_norm):
    R, C, HD = GDN_TILE, GDN_CHUNK, GDN_HEAD_DIM
    t = pl.program_id(1)
    head = GDN_CONV - 1
    base = SUBLANES - head

    @pl.when(t == 0)
    def _():
        s_ref[...] = s0_ref[...]
        xs[0:base, :] = jnp.zeros((base, GDN_CONV_CH), F32)
        xs[base:SUBLANES, :] = conv0_ref[...]

    xs[SUBLANES:SUBLANES + rows_in, :] = xqkv_ref[...]
    if rows_in < R:
        xs[SUBLANES + rows_in:SUBLANES + R, :] = jnp.zeros((R - rows_in, GDN_CONV_CH), F32)

    for cb in range(GDN_CONV_CH // GDN_K_W):
        cols = slice(cb * GDN_K_W, (cb + 1) * GDN_K_W)
        y = xs[base:base + R, cols] * cw_ref[0:1, cols]
        for w in range(1, GDN_CONV):
            y = y + xs[base + w:base + w + R, cols] * cw_ref[w:w + 1, cols]
        y = _silu(y)
        if cb < 2:
            for p in range(GDN_QK_HEADS):
                blk = y[:, p * HD:(p + 1) * HD]
                inv = lax.rsqrt(jnp.sum(blk * blk, axis=-1, keepdims=True) + EPS)
                if cb == 0:
                    inv = inv * HD ** -0.5
                qkv[:, cb * GDN_K_W + p * HD:cb * GDN_K_W + (p + 1) * HD] = blk * inv
        else:
            qkv[:, cols] = y
    if rows_in == R:
        xs[base:SUBLANES, :] = xs[base + R:SUBLANES + R, :]

    ab = ab_ref[...]
    if rows_in < R:
        ab = jnp.concatenate([ab, jnp.zeros((R - rows_in, LANES), F32)], axis=0)
    g = -jnp.exp(alog_ref[...]) * jax.nn.softplus(ab + dtb_ref[...])
    beta = pltpu.roll(jax.nn.sigmoid(ab), LANES - GDN_V_HEADS, axis=1)
    if tvalid < R:
        live = _iota((R, 1), 0) < tvalid
        g = jnp.where(live, g, 0.0)
        beta = jnp.where(live, beta, 0.0)
    ri, ci = _iota((R, R), 0), _iota((R, R), 1)
    same = (ri // C) == (ci // C)
    gc = _dot_f32(jnp.where(same & (ci <= ri), 1.0, 0.0), g)
    gl = _dot_f32(jnp.where(same, 1.0, 0.0), g)
    egc = jnp.exp(gc)
    bgc = beta * egc
    gcT, glT = gc.T, gl.T
    dlT = glT - gcT

    lane = _iota((1, LANES), 1)
    lo = lane < HALF
    jj = lane % C
    ii = _iota((C, 1), 0)
    incl, strict = ii >= jj, ii > jj
    eye2 = jnp.where(ii == jj, 1.0, 0.0)
    zc = jnp.zeros((C, HD), F32)

    def col2(arr, r0, h0):
        return jnp.where(lo, arr[r0:r0 + C, h0:h0 + 1], arr[r0:r0 + C, h0 + 1:h0 + 2])

    def pick(c, row_a, row_b):
        if c == 0:
            return jnp.where(lo, row_a, pltpu.roll(row_b, HALF, axis=1))
        return jnp.where(lo, pltpu.roll(row_a, HALF, axis=1), row_b)

    pairs = range(GDN_QK_HEADS)
    units = [(c, p) for c in range(nchunks) for p in pairs]
    qsl = lambda c, p: qkv[c * C:(c + 1) * C, p * HD:(p + 1) * HD]
    ksl = lambda c, p: qkv[c * C:(c + 1) * C, GDN_K_W + p * HD:GDN_K_W + (p + 1) * HD]
    vsl = lambda c, h: qkv[c * C:(c + 1) * C, 2 * GDN_K_W + h * HD:2 * GDN_K_W + (h + 1) * HD]
    col = lambda arr, c, h: arr[c * C:(c + 1) * C, h:h + 1]

    kts = [qkv[:, GDN_K_W + p * HD:GDN_K_W + (p + 1) * HD].T for p in pairs]
    ms = [_dot_nt(jnp.concatenate([ksl(c, p), qsl(c, p)], axis=0), jnp.concatenate([ksl(c, p)] * 2, axis=0))
          for c, p in units]
    lows, qks = [], []
    for (c, p), m in zip(units, ms):
        h0 = 2 * p
        row_gc = pick(c, gcT[h0:h0 + 1, :], gcT[h0 + 1:h0 + 2, :])
        decay = jnp.exp(jnp.where(incl, col2(gc, c * C, h0) - row_gc, -jnp.inf))
        lows.append(jnp.where(strict, m[0:C] * decay, 0.0) * col2(beta, c * C, h0))
        qks.append(m[C:2 * C] * decay)
    invs = [eye2 - jnp.where((ii // 2) == (jj // 2), low, 0.0) for low in lows]
    s = 2
    while s < C:
        below = ((ii // s) % 2 == 1) & ((jj // s) == (ii // s) - 1)
        eps = [_pair_mul(jnp.where(below, low, 0.0), inv, lo) for low, inv in zip(lows, invs)]
        invs = [inv - _pair_mul(inv, ep, lo) for inv, ep in zip(invs, eps)]
        s *= 2
    sols = []
    for (c, p), inv in zip(units, invs):
        rhs = []
        for idx in range(2):
            h = 2 * p + idx
            piece = [vsl(c, h) * col(beta, c, h), ksl(c, p) * col(bgc, c, h)]
            rhs.append(jnp.concatenate(piece + [zc, zc] if idx == 0 else [zc, zc] + piece, axis=1))
        sols.append(_dot(inv, jnp.concatenate(rhs, axis=0)))

    for c in range(nchunks):
        in_chunk = (lane // C) == c
        heads = range(GDN_V_HEADS)
        w_v = lambda h: sols[c * GDN_QK_HEADS + h // 2][:, 2 * (h % 2) * HD:(2 * (h % 2) + 1) * HD]
        w_k = lambda h: sols[c * GDN_QK_HEADS + h // 2][:, (2 * (h % 2) + 1) * HD:(2 * (h % 2) + 2) * HD]
        a_s = [_dot(jnp.concatenate([w_k(h), qsl(c, h // 2) * col(egc, c, h)], axis=0), s_ref[h]) for h in heads]
        us = [w_v(h) - a_s[h][0:C] for h in heads]
        for p in pairs:
            u2 = jnp.concatenate([jnp.concatenate([us[2 * p], zc], axis=1),
                                  jnp.concatenate([zc, us[2 * p + 1]], axis=1)], axis=0)
            inter = jnp.concatenate([a_s[2 * p][C:2 * C], a_s[2 * p + 1][C:2 * C]], axis=1)
            obuf[c * C:(c + 1) * C, 2 * p * HD:(2 * p + 2) * HD] = inter + _dot(qks[c * GDN_QK_HEADS + p], u2)
        for h in heads:
            k_dec_t = kts[h // 2] * jnp.exp(jnp.where(in_chunk, dlT[h:h + 1, :], -jnp.inf))
            u_ext = jnp.concatenate([us[h], zc] if c == 0 else [zc, us[h]], axis=0)
            g_tot = jnp.exp(pick(c, glT[h:h + 1, :], glT[h:h + 1, :]))
            s_ref[h] = s_ref[h] * g_tot + _dot(k_dec_t, u_ext)

    o = obuf[0:rows_in, :]
    parts = [_rms(o[:, h * HD:(h + 1) * HD], ong_ref[...]) for h in range(GDN_V_HEADS)]
    a = jnp.concatenate(parts, axis=1) * _silu(z_ref[...])
    o_ref[...] = _finish(a, w_ref, x_ref[...], gate_ref[...], fg_ref, final_norm)


def _gdn(xqkv, z, ab, x, gate, conv0, s0, cw, alog, dtb, ong, w_out, fg, b, rows_in, per_row_gate, nchunks, tvalid,
         final_norm, name):
    d = x.shape[-1]
    nt = x.shape[0] // (b * rows_in)
    row = lambda bi, ti: (bi * nt + ti, 0)
    const = lambda bi, ti: (0, 0)
    per_b3 = lambda bi, ti: (bi, 0, 0)
    per_b4 = lambda bi, ti: (bi, 0, 0, 0)
    gate_spec = pl.BlockSpec((rows_in, d), row) if per_row_gate else pl.BlockSpec((None, 1, d), per_b3)
    vec = lambda a: jnp.pad(a.astype(F32), (0, LANES - a.shape[0])).reshape(1, LANES)
    state = (GDN_V_HEADS, GDN_HEAD_DIM, GDN_HEAD_DIM)
    return pl.pallas_call(
        functools.partial(_gdn_kernel, rows_in=rows_in, nchunks=nchunks, tvalid=tvalid, final_norm=final_norm),
        grid=(b, nt),
        in_specs=[pl.BlockSpec((rows_in, GDN_CONV_CH), row),
                  pl.BlockSpec((rows_in, GDN_V_W), row),
                  pl.BlockSpec((rows_in, LANES), row),
                  pl.BlockSpec((rows_in, d), row),
                  gate_spec,
                  pl.BlockSpec((None, GDN_CONV - 1, GDN_CONV_CH), per_b3),
                  pl.BlockSpec((None,) + state, per_b4),
                  pl.BlockSpec(cw.shape, const),
                  pl.BlockSpec((1, LANES), const),
                  pl.BlockSpec((1, LANES), const),
                  pl.BlockSpec((1, GDN_HEAD_DIM), const),
                  pl.BlockSpec(w_out.shape, const),
                  pl.BlockSpec((1, d), const)],
        out_specs=[pl.BlockSpec((rows_in, d), row),
                   pl.BlockSpec((None,) + state, per_b4)],
        out_shape=[jax.ShapeDtypeStruct(x.shape, F32),
                   jax.ShapeDtypeStruct((b,) + state, F32)],
        scratch_shapes=[pltpu.VMEM((SUBLANES + GDN_TILE, GDN_CONV_CH), F32),
                        pltpu.VMEM((GDN_TILE, GDN_CONV_CH), F32),
                        pltpu.VMEM((GDN_TILE, GDN_V_W), F32)],
        compiler_params=_params("arbitrary", "arbitrary"),
        name=name,
    )(xqkv, z, ab, x, gate, conv0, s0, cw, vec(alog), vec(dtb), ong.reshape(1, GDN_HEAD_DIM), w_out,
      fg.reshape(1, d))


def kernel(x_prompt, x_sample, cache_swa_k, cache_swa_v, state_gdn_conv, state_gdn_s, c_prompt, c_sample, norm_g,
           w_mod, b_mod, swa_w_in, swa_sinks, swa_w_out, gdn_w_in, gdn_conv_w, gdn_a_log, gdn_dt_bias, gdn_o_norm_g,
           gdn_w_out, final_norm_g):
    b, t, d = x_prompt.shape
    db, dt, _ = x_sample.shape
    assert d == D_MODEL and t % GDN_TILE == 0 and GDN_CONV - 1 <= dt <= SAMPLE_ROWS
    sr = SAMPLE_ROWS

    mod = _modulation(jnp.concatenate([c_prompt, c_sample], axis=0), w_mod, b_mod)
    xp = x_prompt.reshape(b * t, d)
    xs = jnp.pad(x_sample, ((0, 0), (0, sr - dt), (0, 0))).reshape(db * sr, d)

    swa_in = swa_w_in.astype(BF16)
    swa_out = swa_w_out.astype(BF16)
    gdn_pad = LANES - 2 * GDN_V_HEADS
    gdn_in = jnp.pad(gdn_w_in, ((0, 0), (0, 0), (0, gdn_pad))).astype(BF16)
    gdn_out = gdn_w_out.astype(BF16)
    swa_splits = ((0, SWA_Q_W), (SWA_Q_W, SWA_KV_W), (SWA_Q_W + SWA_KV_W, SWA_KV_W), (SWA_Q_W + 2 * SWA_KV_W, SWA_Q_W))
    swa_scales = (SWA_SCALE, 1.0, 1.0, 1.0)
    gdn_splits = ((0, GDN_CONV_CH), (GDN_CONV_CH, GDN_V_W), (GDN_CONV_CH + GDN_V_W, LANES))
    gdn_scales = (1.0, 1.0, 1.0)

    kp_out, vp_out, ks_out, vs_out, cp_out, sp_out, cs_out, ss_out = [], [], [], [], [], [], [], []
    for layer in range(DEPTH):
        j = layer // 2
        last = layer == DEPTH - 1
        shift, scale, gate = (mod[layer, :, i * d:(i + 1) * d] for i in range(3))
        p3 = lambda a: a[:b].reshape(b, 1, d)
        s_rows = lambda a: jnp.repeat(a[b:], sr, axis=0)
        g_l = norm_g[layer]
        if layer % 2 == 0:
            q, k, v, z = _inproj(xp, g_l, p3(shift), p3(scale), swa_in[j], swa_splits, swa_scales, t, "swa_inproj_p")
            xp = _swa_prompt(swa_sinks[j], q, k, v, z, xp, p3(gate), swa_out[j], final_norm_g, b, t)
            keep = min(WINDOW, t)
            tail = lambda a: a.reshape(b, t, SWA_KV_W)[:, -keep:].reshape(b, keep, SWA_KV_HEADS, SWA_HEAD_DIM)
            kp_out.append(tail(k))
            vp_out.append(tail(v))

            q, k, v, z = _inproj(xs, g_l, s_rows(shift), s_rows(scale), swa_in[j], swa_splits, swa_scales, 1,
                                 "swa_inproj_s")
            a, nk, nv = _swa_sample(swa_sinks[j], q, k, v, z, cache_swa_k[j].reshape(db, WINDOW, SWA_KV_W),
                                    cache_swa_v[j].reshape(db, WINDOW, SWA_KV_W), dt)
            xs = _outproj(a, swa_out[j], xs, s_rows(gate), final_norm_g, False, "swa_outproj_s")
            ks_out.append(nk.reshape(db, WINDOW, SWA_KV_HEADS, SWA_HEAD_DIM))
            vs_out.append(nv.reshape(db, WINDOW, SWA_KV_HEADS, SWA_HEAD_DIM))
        else:
            cw, alog, dtb, ong = gdn_conv_w[j], gdn_a_log[j], gdn_dt_bias[j], gdn_o_norm_g[j]
            xqkv, z, ab = _inproj(xp, g_l, p3(shift), p3(scale), gdn_in[j], gdn_splits, gdn_scales, t, "gdn_inproj_p")
            conv0 = jnp.zeros((b, GDN_CONV - 1, GDN_CONV_CH), F32)
            s0 = jnp.zeros((b, GDN_V_HEADS, GDN_HEAD_DIM, GDN_HEAD_DIM), F32)
            xp, s_new = _gdn(xqkv, z, ab, xp, p3(gate), conv0, s0, cw, alog, dtb, ong, gdn_out[j], final_norm_g, b,
                             GDN_TILE, False, 2, GDN_TILE, last, "gdn_prompt")
            cp_out.append(xqkv.reshape(b, t, GDN_CONV_CH)[:, -(GDN_CONV - 1):])
            sp_out.append(s_new)

            xqkv, z, ab = _inproj(xs, g_l, s_rows(shift), s_rows(scale), gdn_in[j], gdn_splits, gdn_scales, 1,
                                  "gdn_inproj_s")
            xs, s_new = _gdn(xqkv, z, ab, xs, s_rows(gate), state_gdn_conv[j], state_gdn_s[j], cw, alog, dtb, ong,
                             gdn_out[j], final_norm_g, db, sr, True, 1, dt, last, "gdn_sample")
            cs_out.append(xqkv.reshape(db, sr, GDN_CONV_CH)[:, dt - (GDN_CONV - 1):dt])
            ss_out.append(s_new)

    y_prompt = xp.reshape(b, t, d)
    y_sample = xs.reshape(db, sr, d)[:, :dt]
    return (y_prompt, y_sample, jnp.stack(kp_out), jnp.stack(vp_out), jnp.stack(ks_out), jnp.stack(vs_out),
            jnp.stack(cp_out), jnp.stack(sp_out), jnp.stack(cs_out), jnp.stack(ss_out))
```

```python
import functools
import math

import jax
import jax.numpy as jnp
from jax import lax
from jax.experimental import pallas as pl
from jax.experimental.pallas import tpu as pltpu

F32 = jnp.float32
BF16 = jnp.bfloat16

D_MODEL = 1024
DEPTH = 4
EPS = 1e-6
SWA_HEADS = 16
SWA_KV_HEADS = 4
SWA_HEAD_DIM = 64
SWA_GROUP = SWA_HEADS // SWA_KV_HEADS
WINDOW = 128
SWA_Q_W = SWA_HEADS * SWA_HEAD_DIM
SWA_KV_W = SWA_KV_HEADS * SWA_HEAD_DIM
SWA_SCALE = SWA_HEAD_DIM ** -0.5
GDN_QK_HEADS = 8
GDN_V_HEADS = 16
GDN_HEAD_DIM = 128
GDN_K_W = GDN_QK_HEADS * GDN_HEAD_DIM
GDN_V_W = GDN_V_HEADS * GDN_HEAD_DIM
GDN_CONV = 4
GDN_CONV_CH = 2 * GDN_K_W + GDN_V_W
GDN_CHUNK = 64
LANES = 128
SUBLANES = 8
HALF = LANES // 2
GDN_TILE = 2 * GDN_CHUNK
SAMPLE_ROWS = SUBLANES
VMEM_LIMIT = 56 * 1024 * 1024

_NT = (((1,), (1,)), ((), ()))


def _dot(a, b):
    return jnp.dot(a.astype(BF16), b.astype(BF16), preferred_element_type=F32)


def _dot_nt(a, b):
    return lax.dot_general(a.astype(BF16), b.astype(BF16), _NT, preferred_element_type=F32)


def _dot_f32(a, b):
    return jnp.dot(a, b, preferred_element_type=F32, precision=lax.Precision.HIGHEST)


def _silu(x):
    return x * jax.nn.sigmoid(x)


def _iota(shape, dim):
    return lax.broadcasted_iota(jnp.int32, shape, dim)


def _rms(x, g):
    return x * lax.rsqrt(jnp.mean(x * x, axis=-1, keepdims=True) + EPS) * g


def _params(*sem):
    return pltpu.CompilerParams(dimension_semantics=sem, vmem_limit_bytes=VMEM_LIMIT)


def _mod_kernel(c_ref, w_ref, b_ref, o_ref):
    o_ref[...] = _dot(_silu(c_ref[...]), w_ref[...]) + b_ref[...]


def _modulation(c_all, w_mod, b_mod):
    r, d = c_all.shape
    n = w_mod.shape[-1]
    tn = 512
    return pl.pallas_call(
        _mod_kernel,
        grid=(DEPTH, n // tn),
        in_specs=[pl.BlockSpec((r, d), lambda l, j: (0, 0)),
                  pl.BlockSpec((None, d, tn), lambda l, j: (l, 0, j)),
                  pl.BlockSpec((None, 1, tn), lambda l, j: (l, 0, j))],
        out_specs=pl.BlockSpec((None, r, tn), lambda l, j: (l, 0, j)),
        out_shape=jax.ShapeDtypeStruct((DEPTH, r, n), F32),
        compiler_params=_params("arbitrary", "arbitrary"),
        name="adaln_modulation",
    )(c_all, w_mod, b_mod.reshape(DEPTH, 1, n))


def _inproj_kernel(x_ref, g_ref, sh_ref, sc_ref, w_ref, *o_refs, splits, scales):
    h = _rms(x_ref[...], g_ref[...]) * (1.0 + sc_ref[...]) + sh_ref[...]
    hb = h.astype(BF16)
    for o_ref, (c0, n), s in zip(o_refs, splits, scales):
        for cc in range(0, n, 512):
            w = min(512, n - cc)
            r = jnp.dot(hb, w_ref[:, c0 + cc:c0 + cc + w], preferred_element_type=F32)
            o_ref[:, cc:cc + w] = (r * s if s != 1.0 else r).astype(o_ref.dtype)


def _inproj(x, g, shift, scale, w, splits, scales, dtypes, rows_per_mod, name):
    m, d = x.shape
    tm = min(256, m) if rows_per_mod == 1 else min(256, rows_per_mod)
    if rows_per_mod >= tm:
        per = rows_per_mod // tm
        mod_spec = pl.BlockSpec((None, 1, d), lambda i: (i // per, 0, 0))
    else:
        shift, scale = shift.reshape(m, d), scale.reshape(m, d)
        mod_spec = pl.BlockSpec((tm, d), lambda i: (i, 0))
    return pl.pallas_call(
        functools.partial(_inproj_kernel, splits=splits, scales=scales),
        grid=(m // tm,),
        in_specs=[pl.BlockSpec((tm, d), lambda i: (i, 0)),
                  pl.BlockSpec((1, d), lambda i: (0, 0)),
                  mod_spec, mod_spec,
                  pl.BlockSpec(w.shape, lambda i: (0, 0))],
        out_specs=[pl.BlockSpec((tm, n), lambda i: (i, 0)) for _, n in splits],
        out_shape=[jax.ShapeDtypeStruct((m, n), dtype) for (_, n), dtype in zip(splits, dtypes)],
        compiler_params=_params("arbitrary"),
        name=name,
    )(x, g.reshape(1, d), shift, scale, w)


def _finish(a, w_ref, x, gate, fg_ref, final_norm):
    y = x + gate * _dot(a, w_ref[...])
    return _rms(y, fg_ref[...]) if final_norm else y


def _outproj_kernel(a_ref, w_ref, x_ref, gate_ref, fg_ref, o_ref, *, final_norm):
    o_ref[...] = _finish(a_ref[...], w_ref, x_ref[...], gate_ref[...], fg_ref, final_norm)


def _outproj(a, w, x, gate_rows, fg, final_norm, name):
    m, k = a.shape
    d = x.shape[-1]
    tm = min(512, m)
    return pl.pallas_call(
        functools.partial(_outproj_kernel, final_norm=final_norm),
        grid=(m // tm,),
        in_specs=[pl.BlockSpec((tm, k), lambda i: (i, 0)),
                  pl.BlockSpec((k, d), lambda i: (0, 0)),
                  pl.BlockSpec((tm, d), lambda i: (i, 0)),
                  pl.BlockSpec((tm, d), lambda i: (i, 0)),
                  pl.BlockSpec((1, d), lambda i: (0, 0))],
        out_specs=pl.BlockSpec((tm, d), lambda i: (i, 0)),
        out_shape=jax.ShapeDtypeStruct((m, d), F32),
        compiler_params=_params("arbitrary"),
        name=name,
    )(a, w, x, gate_rows, fg.reshape(1, d))


def _swa_block(q, kcat, vcat, mask4, sinks_ref):
    nq = q.shape[0]
    lo = _iota((1, LANES), 1) < HALF
    row = _iota((SWA_GROUP * nq, 1), 0)
    outs = [None] * (SWA_HEADS // 2)
    for pb in range(SWA_KV_HEADS // 2):
        k128 = kcat[:, pb * LANES:(pb + 1) * LANES]
        v128 = vcat[:, pb * LANES:(pb + 1) * LANES]
        for half in range(2):
            kh = 2 * pb + half
            keep = lo if half == 0 else jnp.logical_not(lo)
            parts = []
            for g in range(SWA_GROUP):
                qb = 2 * kh + g // 2
                qg = q[:, qb * LANES:(qb + 1) * LANES]
                if g % 2 != half:
                    qg = pltpu.roll(qg, HALF, axis=1)
                parts.append(jnp.where(keep, qg, 0.0))
            logits = _dot_nt(jnp.concatenate(parts, axis=0), k128)
            sink = jnp.full((SWA_GROUP * nq, 1), sinks_ref[SWA_GROUP * kh], F32)
            for g in range(1, SWA_GROUP):
                sink = jnp.where(row >= g * nq, sinks_ref[SWA_GROUP * kh + g], sink)
            logits = jnp.where(mask4, logits, -jnp.inf)
            m = jnp.maximum(jnp.max(logits, axis=-1, keepdims=True), sink)
            e = jnp.exp(logits - m)
            den = jnp.sum(e, axis=-1, keepdims=True) + jnp.exp(sink - m)
            o4 = _dot(e, v128) * (1.0 / den)
            for g in range(SWA_GROUP):
                piece = o4[g * nq:(g + 1) * nq]
                if g % 2 != half:
                    piece = pltpu.roll(piece, HALF, axis=1)
                piece = jnp.where(lo if g % 2 == 0 else jnp.logical_not(lo), piece, 0.0)
                qb = 2 * kh + g // 2
                outs[qb] = piece if outs[qb] is None else outs[qb] + piece
    return jnp.concatenate(outs, axis=1)


def _band_mask(nq, nk, jmin):
    i = _iota((SWA_GROUP * nq, nk), 0) % nq
    j = _iota((SWA_GROUP * nq, nk), 1)
    return (j >= i) & (j <= i + WINDOW) & (j >= jmin)


def _swa_prompt_kernel(sinks_ref, q_ref, kp_ref, vp_ref, k_ref, v_ref, z_ref, x_ref, gate_ref, w_ref, fg_ref,
                       o_ref, kbuf, vbuf, abuf, *, tq):
    t = pl.program_id(1)
    kbuf[0:WINDOW] = kp_ref[...]
    kbuf[WINDOW:WINDOW + tq] = k_ref[...]
    vbuf[0:WINDOW] = vp_ref[...]
    vbuf[WINDOW:WINDOW + tq] = v_ref[...]

    def body(sb, carry):
        r = pl.multiple_of(sb * WINDOW, WINDOW)
        jmin = jnp.where((t == 0) & (sb == 0), WINDOW, 0)
        mask4 = _band_mask(WINDOW, 2 * WINDOW, jmin)
        abuf[pl.ds(r, WINDOW), :] = _swa_block(q_ref[pl.ds(r, WINDOW), :].astype(F32), kbuf[pl.ds(r, 2 * WINDOW), :],
                                               vbuf[pl.ds(r, 2 * WINDOW), :], mask4, sinks_ref)
        return carry

    lax.fori_loop(0, tq // WINDOW, body, 0)
    a = abuf[...] * _silu(z_ref[...].astype(F32))
    o_ref[...] = _finish(a, w_ref, x_ref[...], gate_ref[...], fg_ref, False)


def _swa_prompt(sinks, q, k, v, z, x, gate, w_out, fg, b, t):
    d = x.shape[-1]
    tq = min(512, t)
    nt = t // tq
    sub = tq // WINDOW
    row = lambda bi, ti: (bi * nt + ti, 0)
    prev = lambda bi, ti: (bi * nt * sub + jnp.maximum(ti * sub - 1, 0), 0)
    const = lambda bi, ti: (0, 0)
    return pl.pallas_call(
        functools.partial(_swa_prompt_kernel, tq=tq),
        grid=(b, nt),
        in_specs=[pl.BlockSpec(memory_space=pltpu.SMEM),
                  pl.BlockSpec((tq, SWA_Q_W), row),
                  pl.BlockSpec((WINDOW, SWA_KV_W), prev),
                  pl.BlockSpec((WINDOW, SWA_KV_W), prev),
                  pl.BlockSpec((tq, SWA_KV_W), row),
                  pl.BlockSpec((tq, SWA_KV_W), row),
                  pl.BlockSpec((tq, SWA_Q_W), row),
                  pl.BlockSpec((tq, d), row),
                  pl.BlockSpec((None, 1, d), lambda bi, ti: (bi, 0, 0)),
                  pl.BlockSpec(w_out.shape, const),
                  pl.BlockSpec((1, d), const)],
        out_specs=pl.BlockSpec((tq, d), row),
        out_shape=jax.ShapeDtypeStruct(x.shape, F32),
        scratch_shapes=[pltpu.VMEM((WINDOW + tq, SWA_KV_W), F32),
                        pltpu.VMEM((WINDOW + tq, SWA_KV_W), F32),
                        pltpu.VMEM((tq, SWA_Q_W), F32)],
        compiler_params=_params("arbitrary", "arbitrary"),
        name="swa_prompt",
    )(sinks, q, k, v, k, v, z, x, gate, w_out, fg.reshape(1, d))


def _swa_sample_kernel(sinks_ref, q_ref, k_ref, v_ref, z_ref, ck_ref, cv_ref, *rest, bb, dt, has_prev):
    a_ref, nk_ref, nv_ref = rest[2:] if has_prev else rest
    nq, grp = SAMPLE_ROWS, SWA_GROUP
    lane = _iota((1, LANES), 1)
    lo = lane < HALF
    hi = jnp.logical_not(lo)
    row = _iota((grp * nq, 1), 0)
    qi = row % nq
    see_cache, see_own = lane >= qi, lane <= qi
    zpad = jnp.zeros((WINDOW - nq, LANES), F32)
    rows = lambda bi: slice(bi * nq, (bi + 1) * nq)
    blk = lambda pb: slice(pb * LANES, (pb + 1) * LANES)
    units = [(bi, kh) for bi in range(bb) for kh in range(SWA_KV_HEADS)]

    q4s = []
    for bi, kh in units:
        keep = lo if kh % 2 == 0 else hi
        parts = []
        for g in range(grp):
            qg = q_ref[rows(bi), blk(2 * kh + g // 2)]
            if g % 2 != kh % 2:
                qg = pltpu.roll(qg, HALF, axis=1)
            parts.append(jnp.where(keep, qg, 0.0))
        q4s.append(jnp.concatenate(parts, axis=0))
    own = lambda ref, bi, pb: jnp.concatenate([ref[rows(bi), blk(pb)], zpad], axis=0)
    lcs = [_dot(q4, ck_ref[bi, blk(kh // 2), :]) for (bi, kh), q4 in zip(units, q4s)]
    los = [_dot_nt(q4, own(k_ref, bi, kh // 2)) for (bi, kh), q4 in zip(units, q4s)]
    ecs, eos, invs = [], [], []
    for (bi, kh), lc, lw in zip(units, lcs, los):
        sink = jnp.full((grp * nq, 1), sinks_ref[grp * kh], F32)
        for g in range(1, grp):
            sink = jnp.where(row >= g * nq, sinks_ref[grp * kh + g], sink)
        lc = jnp.where(see_cache, lc, -jnp.inf)
        lw = jnp.where(see_own, lw, -jnp.inf)
        m = jnp.maximum(jnp.maximum(jnp.max(lc, axis=-1, keepdims=True), jnp.max(lw, axis=-1, keepdims=True)), sink)
        ec, eo = jnp.exp(lc - m), jnp.exp(lw - m)
        den = jnp.sum(ec, axis=-1, keepdims=True) + jnp.sum(eo, axis=-1, keepdims=True) + jnp.exp(sink - m)
        ecs.append(ec)
        eos.append(eo)
        invs.append(1.0 / den)
    o4s = [(_dot_nt(ec, cv_ref[bi, blk(kh // 2), :]) + _dot(eo, own(v_ref, bi, kh // 2))) * inv
           for (bi, kh), ec, eo, inv in zip(units, ecs, eos, invs)]
    for bi in range(bb):
        outs = [None] * (SWA_HEADS // 2)
        for kh in range(SWA_KV_HEADS):
            o4 = o4s[bi * SWA_KV_HEADS + kh]
            for g in range(grp):
                piece = o4[g * nq:(g + 1) * nq]
                if g % 2 != kh % 2:
                    piece = pltpu.roll(piece, HALF, axis=1)
                piece = jnp.where(lo if g % 2 == 0 else hi, piece, 0.0)
                qb = 2 * kh + g // 2
                outs[qb] = piece if outs[qb] is None else outs[qb] + piece
        a_ref[rows(bi), :] = jnp.concatenate(outs, axis=1) * _silu(z_ref[rows(bi), :])

    fresh = lane >= WINDOW - dt
    top = jnp.zeros((WINDOW - nq, LANES), F32)
    for src, cache, dst in ((k_ref, ck_ref, nk_ref), (v_ref, cv_ref, nv_ref)):
        for bi in range(bb):
            for pb in range(SWA_KV_W // LANES):
                tail = jnp.concatenate([top, pltpu.roll(src[rows(bi), blk(pb)], dt, axis=0)], axis=0)
                dst[bi, blk(pb), :] = jnp.where(fresh, tail.T, pltpu.roll(cache[bi, blk(pb), :], WINDOW - dt, axis=1))


def _swa_sample(sinks, q, k, v, z, ck_all, cv_all, nk_prev, nv_prev, j, dt):
    db = ck_all.shape[1]
    bb = min(8, db)
    nq = SAMPLE_ROWS
    assert 2 * dt <= nq
    row = lambda i: (i, 0)
    c_spec = pl.BlockSpec((None, bb, SWA_KV_W, WINDOW), lambda i: (j, i, 0, 0))
    args = [sinks, q, k, v, z, ck_all, cv_all]
    in_specs = [pl.BlockSpec(memory_space=pltpu.SMEM),
                pl.BlockSpec((bb * nq, SWA_Q_W), row),
                pl.BlockSpec((bb * nq, SWA_KV_W), row),
                pl.BlockSpec((bb * nq, SWA_KV_W), row),
                pl.BlockSpec((bb * nq, SWA_Q_W), row),
                c_spec, c_spec]
    aliases = {}
    if nk_prev is not None:
        args += [nk_prev, nv_prev]
        in_specs += [pl.BlockSpec(memory_space=pl.ANY)] * 2
        aliases = {len(args) - 2: 1, len(args) - 1: 2}
    return pl.pallas_call(
        functools.partial(_swa_sample_kernel, bb=bb, dt=dt, has_prev=nk_prev is not None),
        grid=(db // bb,),
        in_specs=in_specs,
        out_specs=[pl.BlockSpec((bb * nq, SWA_Q_W), row), c_spec, c_spec],
        out_shape=[jax.ShapeDtypeStruct(q.shape, F32),
                   jax.ShapeDtypeStruct(ck_all.shape, F32),
                   jax.ShapeDtypeStruct(cv_all.shape, F32)],
        input_output_aliases=aliases,
        compiler_params=_params("arbitrary"),
        name="swa_sample",
    )(*args)


def _pair_mul(a2, b2, lo):
    b16 = b2.astype(BF16)
    zero = jnp.zeros_like(b16)
    bd = jnp.concatenate([jnp.where(lo, b16, zero), jnp.where(lo, zero, b16)], axis=0)
    return _dot(a2, bd)


def _gdn_kernel(xqkv_ref, z_ref, ab_ref, x_ref, gate_ref, conv0_ref, s0_ref, cw_ref, alog_ref, dtb_ref, ong_ref,
                w_ref, fg_ref, o_ref, s_ref, xs, qkv, obuf, *, final_norm):
    nchunks = GDN_TILE // GDN_CHUNK
    R, C, HD = GDN_TILE, GDN_CHUNK, GDN_HEAD_DIM
    t = pl.program_id(1)
    head = GDN_CONV - 1
    base = SUBLANES - head

    @pl.when(t == 0)
    def _():
        s_ref[...] = s0_ref[...]
        xs[0:base, :] = jnp.zeros((base, GDN_CONV_CH), F32)
        xs[base:SUBLANES, :] = conv0_ref[...]

    xs[SUBLANES:SUBLANES + R, :] = xqkv_ref[...].astype(F32)

    for cb in range(GDN_CONV_CH // GDN_K_W):
        cols = slice(cb * GDN_K_W, (cb + 1) * GDN_K_W)
        y = xs[base:base + R, cols] * cw_ref[0:1, cols]
        for w in range(1, GDN_CONV):
            y = y + xs[base + w:base + w + R, cols] * cw_ref[w:w + 1, cols]
        y = _silu(y)
        if cb < 2:
            for p in range(GDN_QK_HEADS):
                blk = y[:, p * HD:(p + 1) * HD]
                inv = lax.rsqrt(jnp.sum(blk * blk, axis=-1, keepdims=True) + EPS)
                if cb == 0:
                    inv = inv * HD ** -0.5
                qkv[:, cb * GDN_K_W + p * HD:cb * GDN_K_W + (p + 1) * HD] = blk * inv
        else:
            qkv[:, cols] = y
    xs[base:SUBLANES, :] = xs[base + R:SUBLANES + R, :]

    ab = ab_ref[...]
    g = -jnp.exp(alog_ref[...]) * jax.nn.softplus(ab + dtb_ref[...])
    beta = pltpu.roll(jax.nn.sigmoid(ab), LANES - GDN_V_HEADS, axis=1)
    ri, ci = _iota((R, R), 0), _iota((R, R), 1)
    same = (ri // C) == (ci // C)
    gc = _dot_f32(jnp.where(same & (ci <= ri), 1.0, 0.0), g)
    gl = _dot_f32(jnp.where(same, 1.0, 0.0), g)
    egc = jnp.exp(gc)
    bgc = beta * egc
    gcT, glT = gc.T, gl.T
    dlT = glT - gcT

    lane = _iota((1, LANES), 1)
    lo = lane < HALF
    jj = lane % C
    ii = _iota((C, 1), 0)
    incl, strict = ii >= jj, ii > jj
    eye2 = jnp.where(ii == jj, 1.0, 0.0)
    zc = jnp.zeros((C, HD), F32)

    def col2(arr, r0, h0):
        return jnp.where(lo, arr[r0:r0 + C, h0:h0 + 1], arr[r0:r0 + C, h0 + 1:h0 + 2])

    def pick(c, row_a, row_b):
        if c == 0:
            return jnp.where(lo, row_a, pltpu.roll(row_b, HALF, axis=1))
        return jnp.where(lo, pltpu.roll(row_a, HALF, axis=1), row_b)

    pairs = range(GDN_QK_HEADS)
    units = [(c, p) for c in range(nchunks) for p in pairs]
    qsl = lambda c, p: qkv[c * C:(c + 1) * C, p * HD:(p + 1) * HD]
    ksl = lambda c, p: qkv[c * C:(c + 1) * C, GDN_K_W + p * HD:GDN_K_W + (p + 1) * HD]
    vsl = lambda c, h: qkv[c * C:(c + 1) * C, 2 * GDN_K_W + h * HD:2 * GDN_K_W + (h + 1) * HD]
    col = lambda arr, c, h: arr[c * C:(c + 1) * C, h:h + 1]

    kts = [qkv[:, GDN_K_W + p * HD:GDN_K_W + (p + 1) * HD].T for p in pairs]
    ms = [_dot_nt(jnp.concatenate([ksl(c, p), qsl(c, p)], axis=0), jnp.concatenate([ksl(c, p)] * 2, axis=0))
          for c, p in units]
    lows, qks = [], []
    for (c, p), m in zip(units, ms):
        h0 = 2 * p
        row_gc = pick(c, gcT[h0:h0 + 1, :], gcT[h0 + 1:h0 + 2, :])
        decay = jnp.exp(jnp.where(incl, col2(gc, c * C, h0) - row_gc, -jnp.inf))
        lows.append(jnp.where(strict, m[0:C] * decay, 0.0) * col2(beta, c * C, h0))
        qks.append(m[C:2 * C] * decay)
    invs = [eye2 - jnp.where((ii // 2) == (jj // 2), low, 0.0) for low in lows]
    s = 2
    while s < C:
        below = ((ii // s) % 2 == 1) & ((jj // s) == (ii // s) - 1)
        eps = [_pair_mul(jnp.where(below, low, 0.0), inv, lo) for low, inv in zip(lows, invs)]
        invs = [inv - _pair_mul(inv, ep, lo) for inv, ep in zip(invs, eps)]
        s *= 2
    sols = []
    for (c, p), inv in zip(units, invs):
        rhs = []
        for idx in range(2):
            h = 2 * p + idx
            piece = [vsl(c, h) * col(beta, c, h), ksl(c, p) * col(bgc, c, h)]
            rhs.append(jnp.concatenate(piece + [zc, zc] if idx == 0 else [zc, zc] + piece, axis=1))
        sols.append(_dot(inv, jnp.concatenate(rhs, axis=0)))

    for c in range(nchunks):
        in_chunk = (lane // C) == c
        heads = range(GDN_V_HEADS)
        w_v = lambda h: sols[c * GDN_QK_HEADS + h // 2][:, 2 * (h % 2) * HD:(2 * (h % 2) + 1) * HD]
        w_k = lambda h: sols[c * GDN_QK_HEADS + h // 2][:, (2 * (h % 2) + 1) * HD:(2 * (h % 2) + 2) * HD]
        a_s = [_dot(jnp.concatenate([w_k(h), qsl(c, h // 2) * col(egc, c, h)], axis=0), s_ref[h]) for h in heads]
        us = [w_v(h) - a_s[h][0:C] for h in heads]
        for p in pairs:
            u2 = jnp.concatenate([jnp.concatenate([us[2 * p], zc], axis=1),
                                  jnp.concatenate([zc, us[2 * p + 1]], axis=1)], axis=0)
            inter = jnp.concatenate([a_s[2 * p][C:2 * C], a_s[2 * p + 1][C:2 * C]], axis=1)
            obuf[c * C:(c + 1) * C, 2 * p * HD:(2 * p + 2) * HD] = inter + _dot(qks[c * GDN_QK_HEADS + p], u2)
        for h in heads:
            k_dec_t = kts[h // 2] * jnp.exp(jnp.where(in_chunk, dlT[h:h + 1, :], -jnp.inf))
            u_ext = jnp.concatenate([us[h], zc] if c == 0 else [zc, us[h]], axis=0)
            g_tot = jnp.exp(pick(c, glT[h:h + 1, :], glT[h:h + 1, :]))
            s_ref[h] = s_ref[h] * g_tot + _dot(k_dec_t, u_ext)

    o = obuf[...]
    parts =[_rms(o[:, h * HD:(h + 1) * HD], ong_ref[...]) for h in range(GDN_V_HEADS)]
    a = jnp.concatenate(parts, axis=1) * _silu(z_ref[...].astype(F32))
    o_ref[...] = _finish(a, w_ref, x_ref[...], gate_ref[...], fg_ref, final_norm)


def _gdn(xqkv, z, ab, x, gate, conv0, s0, cw, alog, dtb, ong, w_out, fg, b, final_norm):
    d = x.shape[-1]
    rows_in = GDN_TILE
    nt = x.shape[0] // (b * rows_in)
    row = lambda bi, ti: (bi * nt + ti, 0)
    const = lambda bi, ti: (0, 0)
    per_b3 = lambda bi, ti: (bi, 0, 0)
    per_b4 = lambda bi, ti: (bi, 0, 0, 0)
    vec = lambda a: jnp.pad(a.astype(F32), (0, LANES - a.shape[0])).reshape(1, LANES)
    state = (GDN_V_HEADS, GDN_HEAD_DIM, GDN_HEAD_DIM)
    return pl.pallas_call(
        functools.partial(_gdn_kernel, final_norm=final_norm),
        grid=(b, nt),
        in_specs=[pl.BlockSpec((rows_in, GDN_CONV_CH), row),
                  pl.BlockSpec((rows_in, GDN_V_W), row),
                  pl.BlockSpec((rows_in, LANES), row),
                  pl.BlockSpec((rows_in, d), row),
                  pl.BlockSpec((None, 1, d), per_b3),
                  pl.BlockSpec((None, GDN_CONV - 1, GDN_CONV_CH), per_b3),
                  pl.BlockSpec((None,) + state, per_b4),
                  pl.BlockSpec(cw.shape, const),
                  pl.BlockSpec((1, LANES), const),
                  pl.BlockSpec((1, LANES), const),
                  pl.BlockSpec((1, GDN_HEAD_DIM), const),
                  pl.BlockSpec(w_out.shape, const),
                  pl.BlockSpec((1, d), const)],
        out_specs=[pl.BlockSpec((rows_in, d), row),
                   pl.BlockSpec((None,) + state, per_b4)],
        out_shape=[jax.ShapeDtypeStruct(x.shape, F32),
                   jax.ShapeDtypeStruct((b,) + state, F32)],
        scratch_shapes=[pltpu.VMEM((SUBLANES + GDN_TILE, GDN_CONV_CH), F32),
                        pltpu.VMEM((GDN_TILE, GDN_CONV_CH), F32),
                        pltpu.VMEM((GDN_TILE, GDN_V_W), F32)],
        compiler_params=_params("arbitrary", "arbitrary"),
        name="gdn_prompt",
    )(xqkv, z, ab, x, gate, conv0, s0, cw, vec(alog), vec(dtb), ong.reshape(1, GDN_HEAD_DIM), w_out,
      fg.reshape(1, d))


def _tile_roll(a, d):
    r, w = a.shape
    return pltpu.roll(a.reshape(r // SUBLANES, SUBLANES, w), d, axis=1).reshape(r, w)


def _gdn_sample_kernel(xqkv_ref, z_ref, ab_ref, conv_ref, s_ref, cw_ref, alog_ref, dtb_ref, ong_ref, *rest,
                       bb, dt, has_prev):
    a_ref, so_ref, st8, qkv = rest[1:] if has_prev else rest
    R, HD = bb * SUBLANES, GDN_HEAD_DIM
    tix = _iota((R, 1), 0) % SUBLANES
    live = tix < dt
    head = GDN_CONV - 1
    lane = _iota((1, LANES), 1)

    st8[...] = jnp.zeros(st8.shape, F32)
    for bi in range(bb):
        st8[(bi + 1) * SUBLANES - head:(bi + 1) * SUBLANES, :] = conv_ref[bi]
    for cb in range(GDN_CONV_CH // GDN_K_W):
        cols = slice(cb * GDN_K_W, (cb + 1) * GDN_K_W)
        x, st = xqkv_ref[:, cols], st8[:, cols]
        y = x * cw_ref[head:head + 1, cols]
        for d in range(1, GDN_CONV):
            y = y + jnp.where(tix >= d, _tile_roll(x, d), _tile_roll(st, d)) * cw_ref[head - d:head - d + 1, cols]
        y = _silu(y)
        if cb < 2:
            for p in range(GDN_QK_HEADS):
                blk = y[:, p * HD:(p + 1) * HD]
                inv = lax.rsqrt(jnp.sum(blk * blk, axis=-1, keepdims=True) + EPS)
                if cb == 0:
                    inv = inv * HD ** -0.5
                qkv[:, cb * GDN_K_W + p * HD:cb * GDN_K_W + (p + 1) * HD] = blk * inv
        else:
            qkv[:, cols] = y

    ab = ab_ref[...]
    g = jnp.where(live, -jnp.exp(alog_ref[...]) * jax.nn.softplus(ab + dtb_ref[...]), 0.0)
    beta = jnp.where(live, pltpu.roll(jax.nn.sigmoid(ab), LANES - GDN_V_HEADS, axis=1), 0.0)
    gc, gl = g, g
    for d in range(1, SUBLANES):
        gl = gl + _tile_roll(g, d)
        if d < dt:
            gc = gc + jnp.where(tix >= d, _tile_roll(g, d), 0.0)
    egc = jnp.exp(gc)
    bgc = beta * egc
    ekl = jnp.where(live, jnp.exp(jnp.where(live, gl - gc, 0.0)), 0.0)
    gtot = jnp.exp(gl)
    dec = [None] + [jnp.where(tix >= d, jnp.exp(jnp.where(tix >= d, gc - _tile_roll(gc, d), 0.0)), 0.0)
                    for d in range(1, dt)]

    zero = jnp.zeros((R, LANES), F32)
    kk, qk = [zero] * dt, [zero] * dt
    qsl = lambda p: qkv[:, p * HD:(p + 1) * HD]
    ksl = lambda p: qkv[:, GDN_K_W + p * HD:GDN_K_W + (p + 1) * HD]
    vsl = lambda h: qkv[:, 2 * GDN_K_W + h * HD:2 * GDN_K_W + (h + 1) * HD]
    for p in range(GDN_QK_HEADS):
        mine = (lane // 2) == p
        for d in range(dt):
            ks = ksl(p) if d == 0 else _tile_roll(ksl(p), d)
            qk[d] = jnp.where(mine, jnp.sum(qsl(p) * ks, axis=-1, keepdims=True), qk[d])
            if d:
                kk[d] = jnp.where(mine, jnp.sum(ksl(p) * ks, axis=-1, keepdims=True), kk[d])
    low = [None] + [beta * kk[d] * dec[d] for d in range(1, dt)]
    qkd = [qk[0]] + [qk[d] * dec[d] for d in range(1, dt)]
    l1s1, l1s2 = _tile_roll(low[1], 1), _tile_roll(low[1], 2)
    tinv = [None, -low[1], -low[2] + low[1] * l1s1,
            -low[3] + low[1] * _tile_roll(low[2], 1) + low[2] * l1s2 - low[1] * l1s1 * l1s2]
    col = lambda arr, h: arr[:, h:h + 1]

    heads = range(GDN_V_HEADS)
    xvs, m8s = [], []
    for h in heads:
        p = h // 2
        rv, rk = vsl(h) * col(beta, h), ksl(p) * col(bgc, h)
        xv, xk = rv, rk
        for d in range(1, dt):
            xv = xv + col(tinv[d], h) * _tile_roll(rv, d)
            xk = xk + col(tinv[d], h) * _tile_roll(rk, d)
        xvs.append(xv)
        m8s.append(jnp.where(live, xk, _tile_roll(qsl(p) * col(egc, h), dt)))
    tile = lambda a, bi: a[bi * SUBLANES:(bi + 1) * SUBLANES]
    aws = [jnp.concatenate([_dot(tile(m8s[h], bi), s_ref[bi, h]) for bi in range(bb)], axis=0) for h in heads]
    outs, uds = [], []
    for h in heads:
        u = jnp.where(live, xvs[h] - aws[h], 0.0)
        o = _tile_roll(aws[h], SUBLANES - dt) + col(qkd[0], h) * u
        for d in range(1, dt):
            o = o + col(qkd[d], h) * _tile_roll(u, d)
        outs.append(_rms(o, ong_ref[...]))
        uds.append(u * col(ekl, h))
    a_ref[...] = jnp.concatenate(outs, axis=1) * _silu(z_ref[...])
    tn = (((0,), (0,)), ((), ()))
    for h in heads:
        gt = jnp.broadcast_to(col(gtot, h), (R, LANES))
        kp = ksl(h // 2)
        for bi in range(bb):
            upd = lax.dot_general(tile(kp, bi).astype(BF16), tile(uds[h], bi).astype(BF16), tn,
                                  preferred_element_type=F32)
            so_ref[bi, h] = s_ref[bi, h] * gt[bi * SUBLANES:bi * SUBLANES + 1] + upd


def _gdn_sample(xqkv, z, ab, conv, s_all, s_prev, j, cw, alog, dtb, ong, dt):
    db = conv.shape[0]
    bb = min(8, db)
    r = bb * SUBLANES
    assert 2 * dt <= SUBLANES and dt == GDN_CONV
    row = lambda i: (i, 0)
    const = lambda i: (0, 0)
    state = (GDN_V_HEADS, GDN_HEAD_DIM, GDN_HEAD_DIM)
    s_spec = pl.BlockSpec((None, bb) + state, lambda i: (j, i, 0, 0, 0))
    vec = lambda a: jnp.pad(a.astype(F32), (0, LANES - a.shape[0])).reshape(1, LANES)
    args = [xqkv, z, ab, conv, s_all, cw, vec(alog), vec(dtb), ong.reshape(1, GDN_HEAD_DIM)]
    in_specs = [pl.BlockSpec((r, GDN_CONV_CH), row),
                pl.BlockSpec((r, GDN_V_W), row),
                pl.BlockSpec((r, LANES), row),
                pl.BlockSpec((bb, GDN_CONV - 1, GDN_CONV_CH), lambda i: (i, 0, 0)),
                s_spec,
                pl.BlockSpec(cw.shape, const),
                pl.BlockSpec((1, LANES), const),
                pl.BlockSpec((1, LANES), const),
                pl.BlockSpec((1, GDN_HEAD_DIM), const)]
    aliases = {}
    if s_prev is not None:
        args.append(s_prev)
        in_specs.append(pl.BlockSpec(memory_space=pl.ANY))
        aliases = {len(args) - 1: 1}
    return pl.pallas_call(
        functools.partial(_gdn_sample_kernel, bb=bb, dt=dt, has_prev=s_prev is not None),
        grid=(db // bb,),
        in_specs=in_specs,
        out_specs=[pl.BlockSpec((r, GDN_V_W), row), s_spec],
        out_shape=[jax.ShapeDtypeStruct(z.shape, F32), jax.ShapeDtypeStruct(s_all.shape, F32)],
        scratch_shapes=[pltpu.VMEM((r, GDN_CONV_CH), F32), pltpu.VMEM((r, GDN_CONV_CH), F32)],
        input_output_aliases=aliases,
        compiler_params=_params("arbitrary"),
        name="gdn_sample",
    )(*args)


def kernel(x_prompt, x_sample, cache_swa_k, cache_swa_v, state_gdn_conv, state_gdn_s, c_prompt, c_sample, norm_g,
           w_mod, b_mod, swa_w_in, swa_sinks, swa_w_out, gdn_w_in, gdn_conv_w, gdn_a_log, gdn_dt_bias, gdn_o_norm_g,
           gdn_w_out, final_norm_g):
    b, t, d = x_prompt.shape
    db, dt, _ = x_sample.shape
    assert d == D_MODEL and t % GDN_TILE == 0 and GDN_CONV - 1 <= dt <= SAMPLE_ROWS
    sr = SAMPLE_ROWS

    mod = _modulation(jnp.concatenate([c_prompt, c_sample], axis=0), w_mod, b_mod)
    xp = x_prompt.reshape(b * t, d)
    xs = jnp.pad(x_sample, ((0, 0), (0, sr - dt), (0, 0))).reshape(db * sr, d)

    swa_in = swa_w_in.astype(BF16)
    swa_out = swa_w_out.astype(BF16)
    gdn_pad = LANES - 2 * GDN_V_HEADS
    gdn_in = jnp.pad(gdn_w_in, ((0, 0), (0, 0), (0, gdn_pad))).astype(BF16)
    gdn_out = gdn_w_out.astype(BF16)
    swa_splits = ((0, SWA_Q_W), (SWA_Q_W, SWA_KV_W), (SWA_Q_W + SWA_KV_W, SWA_KV_W), (SWA_Q_W + 2 * SWA_KV_W, SWA_Q_W))
    swa_scales = (SWA_SCALE, 1.0, 1.0, 1.0)
    gdn_splits = ((0, GDN_CONV_CH), (GDN_CONV_CH, GDN_V_W), (GDN_CONV_CH + GDN_V_W, LANES))
    gdn_scales = (1.0, 1.0, 1.0)

    n_swa = cache_swa_k.shape[0]
    feature_major = lambda c: jnp.transpose(c, (0, 1, 3, 4, 2)).reshape(n_swa, db, SWA_KV_W, WINDOW)
    position_major = lambda c: jnp.transpose(c.reshape(n_swa, db, SWA_KV_HEADS, SWA_HEAD_DIM, WINDOW), (0, 1, 4, 2, 3))
    ck_all, cv_all = feature_major(cache_swa_k), feature_major(cache_swa_v)

    kp_out, vp_out, cp_out, sp_out, cs_out = [], [], [], [], []
    s_sample = nk_all = nv_all = None
    for layer in range(DEPTH):
        j = layer // 2
        last = layer == DEPTH - 1
        shift, scale, gate = (mod[layer, :, i * d:(i + 1) * d] for i in range(3))
        p3 = lambda a: a[:b].reshape(b, 1, d)
        s_rows = lambda a: jnp.repeat(a[b:], sr, axis=0)
        g_l = norm_g[layer]
        if layer % 2 == 0:
            q, k, v, z = _inproj(xp, g_l, p3(shift), p3(scale), swa_in[j], swa_splits, swa_scales,
                                 (BF16, F32, F32, BF16), t, "swa_inproj_p")
            xp = _swa_prompt(swa_sinks[j], q, k, v, z, xp, p3(gate), swa_out[j], final_norm_g, b, t)
            keep = min(WINDOW, t)
            tail = lambda a: a.reshape(b, t, SWA_KV_W)[:, -keep:].reshape(b, keep, SWA_KV_HEADS, SWA_HEAD_DIM)
            kp_out.append(tail(k))
            vp_out.append(tail(v))

            q, k, v, z = _inproj(xs, g_l, s_rows(shift), s_rows(scale), swa_in[j], swa_splits, swa_scales,
                                 (F32,) * 4, 1, "swa_inproj_s")
            a, nk_all, nv_all = _swa_sample(swa_sinks[j], q, k, v, z, ck_all, cv_all, nk_all, nv_all, j, dt)
            xs = _outproj(a, swa_out[j], xs, s_rows(gate), final_norm_g, False, "swa_outproj_s")
        else:
            cw, alog, dtb, ong = gdn_conv_w[j], gdn_a_log[j], gdn_dt_bias[j], gdn_o_norm_g[j]
            xqkv, z, ab = _inproj(xp, g_l, p3(shift), p3(scale), gdn_in[j], gdn_splits, gdn_scales,
                                  (BF16, BF16, F32), t, "gdn_inproj_p")
            conv0 = jnp.zeros((b, GDN_CONV - 1, GDN_CONV_CH), F32)
            s0 = jnp.zeros((b, GDN_V_HEADS, GDN_HEAD_DIM, GDN_HEAD_DIM), F32)
            xp, s_new = _gdn(xqkv, z, ab, xp, p3(gate), conv0, s0, cw, alog, dtb, ong, gdn_out[j], final_norm_g, b, last)
            cp_out.append(xqkv.reshape(b, t, GDN_CONV_CH)[:, -(GDN_CONV - 1):].astype(F32))
            sp_out.append(s_new)

            xqkv, z, ab = _inproj(xs, g_l, s_rows(shift), s_rows(scale), gdn_in[j], gdn_splits, gdn_scales,
                                  (F32,) * 3, 1, "gdn_inproj_s")
            a, s_sample = _gdn_sample(xqkv, z, ab, state_gdn_conv[j], state_gdn_s, s_sample, j, cw, alog, dtb, ong, dt)
            xs = _outproj(a, gdn_out[j], xs, s_rows(gate), final_norm_g, last, "gdn_outproj_s")
            cs_out.append(xqkv.reshape(db, sr, GDN_CONV_CH)[:, dt - (GDN_CONV - 1):dt])

    y_prompt = xp.reshape(b, t, d)
    y_sample = xs.reshape(db, sr, d)[:, :dt]
    return (y_prompt, y_sample, jnp.stack(kp_out), jnp.stack(vp_out), position_major(nk_all), position_major(nv_all),
            jnp.stack(cp_out), jnp.stack(sp_out), jnp.stack(cs_out), s_sample)
```

```python
import functools
import itertools

import jax
import jax.numpy as jnp
from jax import lax
from jax.experimental import pallas as pl
from jax.experimental.pallas import tpu as pltpu

F32 = jnp.float32
BF16 = jnp.bfloat16

D_MODEL = 1024
DEPTH = 4
EPS = 1e-6
SWA_HEADS = 16
SWA_KV_HEADS = 4
SWA_HEAD_DIM = 64
SWA_GROUP = SWA_HEADS // SWA_KV_HEADS
WINDOW = 128
SWA_Q_W = SWA_HEADS * SWA_HEAD_DIM
SWA_KV_W = SWA_KV_HEADS * SWA_HEAD_DIM
SWA_SCALE = SWA_HEAD_DIM ** -0.5
GDN_QK_HEADS = 8
GDN_V_HEADS = 16
GDN_HEAD_DIM = 128
GDN_K_W = GDN_QK_HEADS * GDN_HEAD_DIM
GDN_V_W = GDN_V_HEADS * GDN_HEAD_DIM
GDN_CONV = 4
GDN_CONV_CH = 2 * GDN_K_W + GDN_V_W
GDN_CHUNK = 64
LANES = 128
SUBLANES = 8
HALF = LANES // 2
GDN_TILE = 2 * GDN_CHUNK
SAMPLE_ROWS = SUBLANES
VMEM_LIMIT = 56 * 1024 * 1024

_NT = (((1,), (1,)), ((), ()))


def _dot(a, b):
    return jnp.dot(a.astype(BF16), b.astype(BF16), preferred_element_type=F32)


def _dot_nt(a, b):
    return lax.dot_general(a.astype(BF16), b.astype(BF16), _NT, preferred_element_type=F32)


def _dot_f32(a, b):
    return jnp.dot(a, b, preferred_element_type=F32, precision=lax.Precision.HIGHEST)


def _silu(x):
    return x * jax.nn.sigmoid(x)


def _iota(shape, dim):
    return lax.broadcasted_iota(jnp.int32, shape, dim)


def _rms(x, g):
    return x * lax.rsqrt(jnp.mean(x * x, axis=-1, keepdims=True) + EPS) * g


def _params(*sem):
    return pltpu.CompilerParams(dimension_semantics=sem, vmem_limit_bytes=VMEM_LIMIT)


def _mod_kernel(c_ref, w_ref, b_ref, o_ref):
    o_ref[...] = _dot(_silu(c_ref[...]), w_ref[...]) + b_ref[...]


def _modulation(c_all, w_mod, b_mod):
    r, d = c_all.shape
    n = w_mod.shape[-1]
    tn = 512
    return pl.pallas_call(
        _mod_kernel,
        grid=(DEPTH, n // tn),
        in_specs=[pl.BlockSpec((r, d), lambda l, j: (0, 0)),
                  pl.BlockSpec((None, d, tn), lambda l, j: (l, 0, j)),
                  pl.BlockSpec((None, 1, tn), lambda l, j: (l, 0, j))],
        out_specs=pl.BlockSpec((None, r, tn), lambda l, j: (l, 0, j)),
        out_shape=jax.ShapeDtypeStruct((DEPTH, r, n), F32),
        compiler_params=_params("arbitrary", "arbitrary"),
        name="adaln_modulation",
    )(c_all, w_mod, b_mod.reshape(DEPTH, 1, n))


def _inproj_kernel(x_ref, g_ref, sh_ref, sc_ref, w_ref, *o_refs, splits, scales):
    h = _rms(x_ref[...], g_ref[...]) * (1.0 + sc_ref[...]) + sh_ref[...]
    hb = h.astype(BF16)
    for o_ref, (c0, n), s in zip(o_refs, splits, scales):
        for cc in range(0, n, 512):
            w = min(512, n - cc)
            r = jnp.dot(hb, w_ref[:, c0 + cc:c0 + cc + w].astype(BF16), preferred_element_type=F32)
            o_ref[:, cc:cc + w] = (r * s if s != 1.0 else r).astype(o_ref.dtype)
        if n < o_ref.shape[1]:
            o_ref[:, n:] = jnp.zeros((o_ref.shape[0], o_ref.shape[1] - n), o_ref.dtype)


def _inproj(x, g, shift, scale, w_all, j, splits, scales, dtypes, rows_per_mod, name):
    m, d = x.shape
    pad = lambda n: -(-n // LANES) * LANES
    tm = min(256, m) if rows_per_mod == 1 else min(512, rows_per_mod)
    if rows_per_mod >= tm:
        per = rows_per_mod // tm
        mod_spec = pl.BlockSpec((None, 1, d), lambda i: (i // per, 0, 0))
    else:
        shift, scale = shift.reshape(m, d), scale.reshape(m, d)
        mod_spec = pl.BlockSpec((tm, d), lambda i: (i, 0))
    return pl.pallas_call(
        functools.partial(_inproj_kernel, splits=splits, scales=scales),
        grid=(m // tm,),
        in_specs=[pl.BlockSpec((tm, d), lambda i: (i, 0)),
                  pl.BlockSpec((1, d), lambda i: (0, 0)),
                  mod_spec, mod_spec,
                  pl.BlockSpec((None,) + w_all.shape[1:], lambda i: (j, 0, 0), pipeline_mode=pl.Buffered(1))],
        out_specs=[pl.BlockSpec((tm, pad(n)), lambda i: (i, 0)) for _, n in splits],
        out_shape=[jax.ShapeDtypeStruct((m, pad(n)), dtype) for (_, n), dtype in zip(splits, dtypes)],
        compiler_params=_params("arbitrary"),
        name=name,
    )(x, g.reshape(1, d), shift, scale, w_all)


def _finish(a, w_ref, x, gate, fg_ref, final_norm):
    y = x + gate * _dot(a, w_ref[...])
    return _rms(y, fg_ref[...]) if final_norm else y


def _outproj_kernel(a_ref, w_ref, x_ref, gate_ref, fg_ref, o_ref, *, final_norm):
    o_ref[...] = _finish(a_ref[...], w_ref, x_ref[...], gate_ref[...], fg_ref, final_norm)


def _outproj(a, w, x, gate_rows, fg, final_norm, name):
    m, k = a.shape
    d = x.shape[-1]
    tm = min(512, m)
    return pl.pallas_call(
        functools.partial(_outproj_kernel, final_norm=final_norm),
        grid=(m // tm,),
        in_specs=[pl.BlockSpec((tm, k), lambda i: (i, 0)),
                  pl.BlockSpec((k, d), lambda i: (0, 0)),
                  pl.BlockSpec((tm, d), lambda i: (i, 0)),
                  pl.BlockSpec((tm, d), lambda i: (i, 0)),
                  pl.BlockSpec((1, d), lambda i: (0, 0))],
        out_specs=pl.BlockSpec((tm, d), lambda i: (i, 0)),
        out_shape=jax.ShapeDtypeStruct((m, d), F32),
        compiler_params=_params("arbitrary"),
        name=name,
    )(a, w, x, gate_rows, fg.reshape(1, d))


def _swa_block(q, kcat, vcat, mask4, sinks_ref):
    nq = q.shape[0]
    lo = _iota((1, LANES), 1) < HALF
    hi = jnp.logical_not(lo)
    row = _iota((SWA_GROUP * nq, 1), 0)
    kvs = range(SWA_KV_HEADS)
    blk = lambda a, pb: a[:, pb * LANES:(pb + 1) * LANES]
    q4s = []
    for kh in kvs:
        parts = []
        for g in range(SWA_GROUP):
            qg = blk(q, 2 * kh + g // 2)
            if g % 2 != kh % 2:
                qg = pltpu.roll(qg, HALF, axis=1)
            parts.append(jnp.where(lo if kh % 2 == 0 else hi, qg, 0.0))
        q4s.append(jnp.concatenate(parts, axis=0))
    logits = [_dot_nt(q4, blk(kcat, kh // 2)) for kh, q4 in zip(kvs, q4s)]
    es, invs = [], []
    for kh, lg in zip(kvs, logits):
        sink = jnp.full((SWA_GROUP * nq, 1), sinks_ref[SWA_GROUP * kh], F32)
        for g in range(1, SWA_GROUP):
            sink = jnp.where(row >= g * nq, sinks_ref[SWA_GROUP * kh + g], sink)
        lg = jnp.where(mask4, lg, -jnp.inf)
        m = jnp.maximum(jnp.max(lg, axis=-1, keepdims=True), sink)
        e = jnp.exp(lg - m)
        es.append(e)
        invs.append(1.0 / (jnp.sum(e, axis=-1, keepdims=True) + jnp.exp(sink - m)))
    o4s = [_dot(e, blk(vcat, kh // 2)) * inv for kh, e, inv in zip(kvs, es, invs)]
    outs = [None] * (SWA_HEADS // 2)
    for kh, o4 in zip(kvs, o4s):
        for g in range(SWA_GROUP):
            piece = o4[g * nq:(g + 1) * nq]
            if g % 2 != kh % 2:
                piece = pltpu.roll(piece, HALF, axis=1)
            piece = jnp.where(lo if g % 2 == 0 else hi, piece, 0.0)
            qb = 2 * kh + g // 2
            outs[qb] = piece if outs[qb] is None else outs[qb] + piece
    return jnp.concatenate(outs, axis=1)


def _band_mask(nq, nk, jmin):
    i = _iota((SWA_GROUP * nq, nk), 0) % nq
    j = _iota((SWA_GROUP * nq, nk), 1)
    return (j >= i) & (j <= i + WINDOW) & (j >= jmin)


def _swa_prompt_kernel(sinks_ref, q_ref, kp_ref, vp_ref, k_ref, v_ref, z_ref, x_ref, gate_ref, w_ref, fg_ref,
                       o_ref, kbuf, vbuf, abuf, *, tq):
    t = pl.program_id(1)
    kbuf[0:WINDOW] = kp_ref[...]
    kbuf[WINDOW:WINDOW + tq] = k_ref[...]
    vbuf[0:WINDOW] = vp_ref[...]
    vbuf[WINDOW:WINDOW + tq] = v_ref[...]

    def body(sb, carry):
        r = pl.multiple_of(sb * WINDOW, WINDOW)
        jmin = jnp.where((t == 0) & (sb == 0), WINDOW, 0)
        mask4 = _band_mask(WINDOW, 2 * WINDOW, jmin)
        abuf[pl.ds(r, WINDOW), :] = _swa_block(q_ref[pl.ds(r, WINDOW), :].astype(F32), kbuf[pl.ds(r, 2 * WINDOW), :],
                                               vbuf[pl.ds(r, 2 * WINDOW), :], mask4, sinks_ref)
        return carry

    lax.fori_loop(0, tq // WINDOW, body, 0)
    a = abuf[...] * _silu(z_ref[...].astype(F32))
    o_ref[...] = _finish(a, w_ref, x_ref[...], gate_ref[...], fg_ref, False)


def _swa_prompt(sinks, q, k, v, z, x, gate, w_out, fg, b, t):
    d = x.shape[-1]
    tq = min(512, t)
    nt = t // tq
    sub = tq // WINDOW
    row = lambda bi, ti: (bi * nt + ti, 0)
    prev = lambda bi, ti: (bi * nt * sub + jnp.maximum(ti * sub - 1, 0), 0)
    const = lambda bi, ti: (0, 0)
    return pl.pallas_call(
        functools.partial(_swa_prompt_kernel, tq=tq),
        grid=(b, nt),
        in_specs=[pl.BlockSpec(memory_space=pltpu.SMEM),
                  pl.BlockSpec((tq, SWA_Q_W), row),
                  pl.BlockSpec((WINDOW, SWA_KV_W), prev),
                  pl.BlockSpec((WINDOW, SWA_KV_W), prev),
                  pl.BlockSpec((tq, SWA_KV_W), row),
                  pl.BlockSpec((tq, SWA_KV_W), row),
                  pl.BlockSpec((tq, SWA_Q_W), row),
                  pl.BlockSpec((tq, d), row),
                  pl.BlockSpec((None, 1, d), lambda bi, ti: (bi, 0, 0)),
                  pl.BlockSpec(w_out.shape, const),
                  pl.BlockSpec((1, d), const)],
        out_specs=pl.BlockSpec((tq, d), row),
        out_shape=jax.ShapeDtypeStruct(x.shape, F32),
        scratch_shapes=[pltpu.VMEM((WINDOW + tq, SWA_KV_W), F32),
                        pltpu.VMEM((WINDOW + tq, SWA_KV_W), F32),
                        pltpu.VMEM((tq, SWA_Q_W), F32)],
        compiler_params=_params("arbitrary", "arbitrary"),
        name="swa_prompt",
    )(sinks, q, k, v, k, v, z, x, gate, w_out, fg.reshape(1, d))


def _swa_sample_kernel(sinks_ref, q_ref, k_ref, v_ref, z_ref, ck_ref, cv_ref, *rest, bb, dt, has_prev):
    a_ref, nk_ref, nv_ref = rest[2:] if has_prev else rest
    nq, grp = SAMPLE_ROWS, SWA_GROUP
    lane = _iota((1, LANES), 1)
    lo = lane < HALF
    hi = jnp.logical_not(lo)
    row = _iota((grp * nq, 1), 0)
    qi = row % nq
    see_cache, see_own = lane >= qi, lane <= qi
    zpad = jnp.zeros((WINDOW - nq, LANES), F32)
    rows = lambda bi: slice(bi * nq, (bi + 1) * nq)
    blk = lambda pb: slice(pb * LANES, (pb + 1) * LANES)
    units = [(bi, kh) for bi in range(bb) for kh in range(SWA_KV_HEADS)]

    q4s = []
    for bi, kh in units:
        keep = lo if kh % 2 == 0 else hi
        parts = []
        for g in range(grp):
            qg = q_ref[rows(bi), blk(2 * kh + g // 2)]
            if g % 2 != kh % 2:
                qg = pltpu.roll(qg, HALF, axis=1)
            parts.append(jnp.where(keep, qg, 0.0))
        q4s.append(jnp.concatenate(parts, axis=0))
    own = lambda ref, bi, pb: jnp.concatenate([ref[rows(bi), blk(pb)], zpad], axis=0)
    lcs = [_dot(q4, ck_ref[bi, blk(kh // 2), :]) for (bi, kh), q4 in zip(units, q4s)]
    los = [_dot_nt(q4, own(k_ref, bi, kh // 2)) for (bi, kh), q4 in zip(units, q4s)]
    ecs, eos, invs = [], [], []
    for (bi, kh), lc, lw in zip(units, lcs, los):
        sink = jnp.full((grp * nq, 1), sinks_ref[grp * kh], F32)
        for g in range(1, grp):
            sink = jnp.where(row >= g * nq, sinks_ref[grp * kh + g], sink)
        lc = jnp.where(see_cache, lc, -jnp.inf)
        lw = jnp.where(see_own, lw, -jnp.inf)
        m = jnp.maximum(jnp.maximum(jnp.max(lc, axis=-1, keepdims=True), jnp.max(lw, axis=-1, keepdims=True)), sink)
        ec, eo = jnp.exp(lc - m), jnp.exp(lw - m)
        den = jnp.sum(ec, axis=-1, keepdims=True) + jnp.sum(eo, axis=-1, keepdims=True) + jnp.exp(sink - m)
        ecs.append(ec)
        eos.append(eo)
        invs.append(1.0 / den)
    o4s = [(_dot_nt(ec, cv_ref[bi, blk(kh // 2), :]) + _dot(eo, own(v_ref, bi, kh // 2))) * inv
           for (bi, kh), ec, eo, inv in zip(units, ecs, eos, invs)]
    for bi in range(bb):
        outs = [None] * (SWA_HEADS // 2)
        for kh in range(SWA_KV_HEADS):
            o4 = o4s[bi * SWA_KV_HEADS + kh]
            for g in range(grp):
                piece = o4[g * nq:(g + 1) * nq]
                if g % 2 != kh % 2:
                    piece = pltpu.roll(piece, HALF, axis=1)
                piece = jnp.where(lo if g % 2 == 0 else hi, piece, 0.0)
                qb = 2 * kh + g // 2
                outs[qb] = piece if outs[qb] is None else outs[qb] + piece
        a_ref[rows(bi), :] = jnp.concatenate(outs, axis=1) * _silu(z_ref[rows(bi), :])

    fresh = lane >= WINDOW - dt
    top = jnp.zeros((WINDOW - nq, LANES), F32)
    for src, cache, dst in ((k_ref, ck_ref, nk_ref), (v_ref, cv_ref, nv_ref)):
        for bi in range(bb):
            for pb in range(SWA_KV_W // LANES):
                tail = jnp.concatenate([top, pltpu.roll(src[rows(bi), blk(pb)], dt, axis=0)], axis=0)
                dst[bi, blk(pb), :] = jnp.where(fresh, tail.T, pltpu.roll(cache[bi, blk(pb), :], WINDOW - dt, axis=1))


def _swa_sample(sinks, q, k, v, z, ck_all, cv_all, nk_prev, nv_prev, j, dt):
    db = ck_all.shape[1]
    bb = min(8, db)
    nq = SAMPLE_ROWS
    assert 2 * dt <= nq
    row = lambda i: (i, 0)
    c_spec = pl.BlockSpec((None, bb, SWA_KV_W, WINDOW), lambda i: (j, i, 0, 0))
    args = [sinks, q, k, v, z, ck_all, cv_all]
    in_specs = [pl.BlockSpec(memory_space=pltpu.SMEM),
                pl.BlockSpec((bb * nq, SWA_Q_W), row),
                pl.BlockSpec((bb * nq, SWA_KV_W), row),
                pl.BlockSpec((bb * nq, SWA_KV_W), row),
                pl.BlockSpec((bb * nq, SWA_Q_W), row),
                c_spec, c_spec]
    aliases = {}
    if nk_prev is not None:
        args += [nk_prev, nv_prev]
        in_specs += [pl.BlockSpec(memory_space=pl.ANY)] * 2
        aliases = {len(args) - 2: 1, len(args) - 1: 2}
    return pl.pallas_call(
        functools.partial(_swa_sample_kernel, bb=bb, dt=dt, has_prev=nk_prev is not None),
        grid=(db // bb,),
        in_specs=in_specs,
        out_specs=[pl.BlockSpec((bb * nq, SWA_Q_W), row), c_spec, c_spec],
        out_shape=[jax.ShapeDtypeStruct(q.shape, F32),
                   jax.ShapeDtypeStruct(ck_all.shape, F32),
                   jax.ShapeDtypeStruct(cv_all.shape, F32)],
        input_output_aliases=aliases,
        compiler_params=_params("arbitrary"),
        name="swa_sample",
    )(*args)


def _block_mul(a, b):
    size, width = b.shape
    b16 = b.astype(BF16)
    zero = jnp.zeros_like(b16)
    blk = _iota((1, width), 1) // size
    return _dot(a, jnp.concatenate([jnp.where(blk == r, b16, zero) for r in range(width // size)], axis=0))


def _diag2(a, b):
    return jnp.concatenate([jnp.concatenate([a, jnp.zeros_like(b)], axis=1),
                            jnp.concatenate([jnp.zeros_like(a), b], axis=1)], axis=0)


def _gdn_kernel(xqkv_ref, z_ref, ab_ref, x_ref, gate_ref, conv0_ref, s0_ref, cw_ref, alog_ref, dtb_ref, ong_ref,
                w_ref, fg_ref, o_ref, s_ref, xs, qkv, obuf, *, nb, final_norm):
    nchunks = GDN_TILE // GDN_CHUNK
    R, C, HD = GDN_TILE, GDN_CHUNK, GDN_HEAD_DIM
    t = pl.program_id(1)
    head = GDN_CONV - 1
    base = SUBLANES - head
    seqs = range(nb)

    @pl.when(t == 0)
    def _():
        s_ref[...] = s0_ref[...]
        for sq in seqs:
            xs[sq, 0:base, :] = jnp.zeros((base, GDN_CONV_CH), F32)
            xs[sq, base:SUBLANES, :] = conv0_ref[sq]

    ri, ci = _iota((R, R), 0), _iota((R, R), 1)
    shifts = [None] + [jnp.where(ci == ri - d, 1.0, 0.0).astype(BF16) for d in range(1, GDN_CONV)]
    sub = _iota((SUBLANES, 1), 0)
    same = (ri // C) == (ci // C)
    tri, ones = jnp.where(same & (ci <= ri), 1.0, 0.0), jnp.where(same, 1.0, 0.0)
    gcs, betas, egcs, bgcs, gcTs, glTs, dlTs = ([None] * nb for _ in range(7))

    def prologue(sq):
        width = 2 * LANES
        for cb in range(GDN_CONV_CH // width):
            cols = slice(cb * width, (cb + 1) * width)
            xb = xqkv_ref[sq, :, cols]
            x0 = xb.astype(F32)
            prev = xs[sq, :, cols]
            y = x0 * cw_ref[head:head + 1, cols]
            fix = jnp.zeros((SUBLANES, width), F32)
            for d in range(1, GDN_CONV):
                tap = cw_ref[head - d:head - d + 1, cols]
                y = y + jnp.dot(shifts[d], xb, preferred_element_type=F32) * tap
                fix = fix + jnp.where(sub < d, pltpu.roll(prev, d, axis=0), 0.0) * tap
            y = jnp.concatenate([y[0:SUBLANES] + fix, y[SUBLANES:]], axis=0)
            xs[sq, :, cols] = x0[R - SUBLANES:R]
            y = _silu(y)
            if cb * width < 2 * GDN_K_W:
                parts = []
                for p in range(width // HD):
                    blk = y[:, p * HD:(p + 1) * HD]
                    inv = lax.rsqrt(jnp.sum(blk * blk, axis=-1, keepdims=True) + EPS)
                    parts.append(blk * (inv * HD ** -0.5 if cb * width < GDN_K_W else inv))
                y = jnp.concatenate(parts, axis=1)
            qkv[sq, :, cols] = y
            yield
        ab = ab_ref[sq]
        g = -jnp.exp(alog_ref[...]) * jax.nn.softplus(ab + dtb_ref[...])
        beta = pltpu.roll(jax.nn.sigmoid(ab), LANES - GDN_V_HEADS, axis=1)
        gc = _dot_f32(tri, g)
        gl = _dot_f32(ones, g)
        egc = jnp.exp(gc)
        gcT, glT = gc.T, gl.T
        gcs[sq], betas[sq], egcs[sq], bgcs[sq] = gc, beta, egc, beta * egc
        gcTs[sq], glTs[sq], dlTs[sq] = gcT, glT, glT - gcT
        yield

    lane = _iota((1, LANES), 1)
    lo = lane < HALF
    jj = lane % C
    ii = _iota((C, 1), 0)
    incl, strict = ii >= jj, ii > jj
    zc = jnp.zeros((C, HD), F32)

    def col2(arr, r0, h0):
        return jnp.where(lo, arr[r0:r0 + C, h0:h0 + 1], arr[r0:r0 + C, h0 + 1:h0 + 2])

    def pick(c, row_a, row_b):
        if c == 0:
            return jnp.where(lo, row_a, pltpu.roll(row_b, HALF, axis=1))
        return jnp.where(lo, pltpu.roll(row_a, HALF, axis=1), row_b)

    pairs = range(GDN_QK_HEADS)
    heads = range(GDN_V_HEADS)
    units = [(c, p) for c in range(nchunks) for p in pairs]
    quads = [(c, pp) for c in range(nchunks) for pp in range(GDN_QK_HEADS // 2)]
    qsl = lambda sq, c, p: qkv[sq, c * C:(c + 1) * C, p * HD:(p + 1) * HD]
    ksl = lambda sq, c, p: qkv[sq, c * C:(c + 1) * C, GDN_K_W + p * HD:GDN_K_W + (p + 1) * HD]
    vsl = lambda sq, c, h: qkv[sq, c * C:(c + 1) * C, 2 * GDN_K_W + h * HD:2 * GDN_K_W + (h + 1) * HD]
    col = lambda arr, c, h: arr[c * C:(c + 1) * C, h:h + 1]
    jq = _iota((1, 2 * LANES), 1) % C
    kts, lows, qks, sols = ([None] * nb for _ in range(4))

    def products(sq):
        kts[sq] = [qkv[sq, :, GDN_K_W + p * HD:GDN_K_W + (p + 1) * HD].T for p in pairs]
        yield
        ms = [_dot_nt(jnp.concatenate([ksl(sq, c, p), qsl(sq, c, p)], axis=0), jnp.concatenate([ksl(sq, c, p)] * 2, axis=0))
              for c, p in units]
        yield
        lows[sq], qks[sq] = [], []
        for (c, p), m in zip(units, ms):
            h0 = 2 * p
            row_gc = pick(c, gcTs[sq][h0:h0 + 1, :], gcTs[sq][h0 + 1:h0 + 2, :])
            decay = jnp.exp(jnp.where(incl, col2(gcs[sq], c * C, h0) - row_gc, -jnp.inf))
            lows[sq].append(jnp.where(strict, m[0:C] * decay, 0.0) * col2(betas[sq], c * C, h0))
            qks[sq].append(m[C:2 * C] * decay)
            if p % 4 == 3:
                yield

    def inverse(sq):
        low4 = [jnp.concatenate([lows[sq][c * GDN_QK_HEADS + 2 * pp], lows[sq][c * GDN_QK_HEADS + 2 * pp + 1]], axis=1)
                for c, pp in quads]
        invs = [jnp.where(ii == jq, 1.0, 0.0) - jnp.where((ii // 2) == (jq // 2), low, 0.0) for low in low4]
        yield
        s = 2
        while s < C:
            below = ((ii // s) % 2 == 1) & ((jq // s) == (ii // s) - 1)
            eps, new = [], []
            for i, (low, inv) in enumerate(zip(low4, invs)):
                eps.append(_block_mul(jnp.where(below, low, 0.0), inv))
                if i % 2:
                    yield
            for i, (inv, ep) in enumerate(zip(invs, eps)):
                new.append(inv - _block_mul(inv, ep))
                if i % 2:
                    yield
            invs = new
            s *= 2
        sols[sq] = []
        for (c, pp), inv in zip(quads, invs):
            rhs = []
            for idx in range(4):
                h = 4 * pp + idx
                piece = [vsl(sq, c, h) * col(betas[sq], c, h), ksl(sq, c, h // 2) * col(bgcs[sq], c, h)]
                rhs.append(jnp.concatenate([zc] * (2 * idx) + piece + [zc] * (6 - 2 * idx), axis=1))
            sols[sq].append(_dot(inv, jnp.concatenate(rhs, axis=0)))
            if pp % 2 == 1:
                yield

    def state(sq):
        for c in range(nchunks):
            in_chunk = (lane // C) == c
            quad = lambda h: sols[sq][c * (GDN_QK_HEADS // 2) + h // 4]
            w_v = lambda h: quad(h)[:, 2 * (h % 4) * HD:(2 * (h % 4) + 1) * HD]
            w_k = lambda h: quad(h)[:, (2 * (h % 4) + 1) * HD:(2 * (h % 4) + 2) * HD]
            a_s = []
            for p in pairs:
                top = jnp.concatenate([w_k(2 * p), w_k(2 * p + 1)], axis=1)
                bot = jnp.concatenate([qsl(sq, c, p) * col(egcs[sq], c, 2 * p), qsl(sq, c, p) * col(egcs[sq], c, 2 * p + 1)],
                                      axis=1)
                a_s.append(_dot(jnp.concatenate([top, bot], axis=0), _diag2(s_ref[sq, 2 * p], s_ref[sq, 2 * p + 1])))
            yield
            us = [[w_v(2 * p + idx) - a[0:C, idx * HD:(idx + 1) * HD] for idx in range(2)] for p, a in zip(pairs, a_s)]
            for p, a, u in zip(pairs, a_s, us):
                obuf[sq, c * C:(c + 1) * C, 2 * p * HD:(2 * p + 2) * HD] = (
                    a[C:2 * C] + _dot(qks[sq][c * GDN_QK_HEADS + p], _diag2(*u)))
            yield
            for p, u in zip(pairs, us):
                kd, ue, gt = [], [], []
                for idx in range(2):
                    h = 2 * p + idx
                    kd.append(kts[sq][p] * jnp.exp(jnp.where(in_chunk, dlTs[sq][h:h + 1, :], -jnp.inf)))
                    ue.append(jnp.concatenate([u[idx], zc] if c == 0 else [zc, u[idx]], axis=0))
                    gt.append(jnp.exp(pick(c, glTs[sq][h:h + 1, :], glTs[sq][h:h + 1, :])))
                upd = _dot(jnp.concatenate(kd, axis=1), _diag2(*ue))
                for idx in range(2):
                    h = 2 * p + idx
                    s_ref[sq, h] = s_ref[sq, h] * gt[idx] + upd[:, idx * HD:(idx + 1) * HD]
                if p % 4 == 3:
                    yield

    def run(*gens):
        gens = list(gens)
        while gens:
            for gen in list(gens):
                if next(gen, "done") == "done":
                    gens.remove(gen)

    for k in range(nb + 2):
        live = [itertools.chain(prologue(k), products(k))] if k < nb else []
        live += [inverse(k - 1)] if 0 <= k - 1 < nb else []
        live += [state(k - 2)] if 0 <= k - 2 < nb else []
        run(*live)

    rows = []
    for sq in seqs:
        o = obuf[sq]
        parts = [_rms(o[:, h * HD:(h + 1) * HD], ong_ref[...]) for h in heads]
        rows.append(jnp.concatenate(parts, axis=1) * _silu(z_ref[sq].astype(F32)))
    out = _dot(jnp.concatenate(rows, axis=0), w_ref[...])
    for sq in seqs:
        y = x_ref[sq] + gate_ref[sq] * out[sq * R:(sq + 1) * R]
        o_ref[sq] = _rms(y, fg_ref[...]) if final_norm else y


def _gdn(xqkv, z, ab, x, gate, conv0, s0, cw, alog, dtb, ong, w_out, fg, final_norm):
    b, t, d = x.shape
    nb = 2 if b % 2 == 0 else 1
    rows_in = GDN_TILE
    tile = lambda w: pl.BlockSpec((nb, rows_in, w), lambda bi, ti: (bi, ti, 0))
    const = lambda bi, ti: (0, 0)
    per_b3 = lambda bi, ti: (bi, 0, 0)
    per_b4 = lambda bi, ti: (bi, 0, 0, 0)
    vec = lambda a: jnp.pad(a.astype(F32), (0, LANES - a.shape[0])).reshape(1, LANES)
    state = (GDN_V_HEADS, GDN_HEAD_DIM, GDN_HEAD_DIM)
    return pl.pallas_call(
        functools.partial(_gdn_kernel, nb=nb, final_norm=final_norm),
        grid=(b // nb, t // rows_in),
        in_specs=[tile(GDN_CONV_CH), tile(GDN_V_W), tile(LANES), tile(d),
                  pl.BlockSpec((nb, 1, d), per_b3),
                  pl.BlockSpec((nb, GDN_CONV - 1, GDN_CONV_CH), per_b3),
                  pl.BlockSpec((nb,) + state, per_b4),
                  pl.BlockSpec(cw.shape, const),
                  pl.BlockSpec((1, LANES), const),
                  pl.BlockSpec((1, LANES), const),
                  pl.BlockSpec((1, GDN_HEAD_DIM), const),
                  pl.BlockSpec(w_out.shape, const),
                  pl.BlockSpec((1, d), const)],
        out_specs=[tile(d), pl.BlockSpec((nb,) + state, per_b4)],
        out_shape=[jax.ShapeDtypeStruct(x.shape, F32),
                   jax.ShapeDtypeStruct((b,) + state, F32)],
        scratch_shapes=[pltpu.VMEM((nb, SUBLANES, GDN_CONV_CH), F32),
                        pltpu.VMEM((nb, GDN_TILE, GDN_CONV_CH), F32),
                        pltpu.VMEM((nb, GDN_TILE, GDN_V_W), F32)],
        compiler_params=_params("arbitrary", "arbitrary"),
        name="gdn_prompt",
    )(xqkv, z, ab, x, gate, conv0, s0, cw, vec(alog), vec(dtb), ong.reshape(1, GDN_HEAD_DIM), w_out,
      fg.reshape(1, d))


def _tile_roll(a, d):
    r, w = a.shape
    return pltpu.roll(a.reshape(r // SUBLANES, SUBLANES, w), d, axis=1).reshape(r, w)


def _gdn_sample_kernel(xqkv_ref, z_ref, ab_ref, conv_ref, s_ref, cw_ref, alog_ref, dtb_ref, ong_ref, *rest,
                       bb, dt, has_prev):
    a_ref, so_ref, st8, qkv = rest[1:] if has_prev else rest
    R, HD = bb * SUBLANES, GDN_HEAD_DIM
    tix = _iota((R, 1), 0) % SUBLANES
    live = tix < dt
    head = GDN_CONV - 1
    lane = _iota((1, LANES), 1)

    st8[...] = jnp.zeros(st8.shape, F32)
    for bi in range(bb):
        st8[(bi + 1) * SUBLANES - head:(bi + 1) * SUBLANES, :] = conv_ref[bi]
    for cb in range(GDN_CONV_CH // GDN_K_W):
        cols = slice(cb * GDN_K_W, (cb + 1) * GDN_K_W)
        x, st = xqkv_ref[:, cols], st8[:, cols]
        y = x * cw_ref[head:head + 1, cols]
        for d in range(1, GDN_CONV):
            y = y + jnp.where(tix >= d, _tile_roll(x, d), _tile_roll(st, d)) * cw_ref[head - d:head - d + 1, cols]
        y = _silu(y)
        if cb < 2:
            for p in range(GDN_QK_HEADS):
                blk = y[:, p * HD:(p + 1) * HD]
                inv = lax.rsqrt(jnp.sum(blk * blk, axis=-1, keepdims=True) + EPS)
                if cb == 0:
                    inv = inv * HD ** -0.5
                qkv[:, cb * GDN_K_W + p * HD:cb * GDN_K_W + (p + 1) * HD] = blk * inv
        else:
            qkv[:, cols] = y

    ab = ab_ref[...]
    g = jnp.where(live, -jnp.exp(alog_ref[...]) * jax.nn.softplus(ab + dtb_ref[...]), 0.0)
    beta = jnp.where(live, pltpu.roll(jax.nn.sigmoid(ab), LANES - GDN_V_HEADS, axis=1), 0.0)
    gc, gl = g, g
    for d in range(1, SUBLANES):
        gl = gl + _tile_roll(g, d)
        if d < dt:
            gc = gc + jnp.where(tix >= d, _tile_roll(g, d), 0.0)
    egc = jnp.exp(gc)
    bgc = beta * egc
    ekl = jnp.where(live, jnp.exp(jnp.where(live, gl - gc, 0.0)), 0.0)
    gtot = jnp.exp(gl)
    dec = [None] + [jnp.where(tix >= d, jnp.exp(jnp.where(tix >= d, gc - _tile_roll(gc, d), 0.0)), 0.0)
                    for d in range(1, dt)]

    zero = jnp.zeros((R, LANES), F32)
    kk, qk = [zero] * dt, [zero] * dt
    qsl = lambda p: qkv[:, p * HD:(p + 1) * HD]
    ksl = lambda p: qkv[:, GDN_K_W + p * HD:GDN_K_W + (p + 1) * HD]
    vsl = lambda h: qkv[:, 2 * GDN_K_W + h * HD:2 * GDN_K_W + (h + 1) * HD]
    for p in range(GDN_QK_HEADS):
        mine = (lane // 2) == p
        for d in range(dt):
            ks = ksl(p) if d == 0 else _tile_roll(ksl(p), d)
            qk[d] = jnp.where(mine, jnp.sum(qsl(p) * ks, axis=-1, keepdims=True), qk[d])
            if d:
                kk[d] = jnp.where(mine, jnp.sum(ksl(p) * ks, axis=-1, keepdims=True), kk[d])
    low = [None] + [beta * kk[d] * dec[d] for d in range(1, dt)]
    qkd = [qk[0]] + [qk[d] * dec[d] for d in range(1, dt)]
    l1s1, l1s2 = _tile_roll(low[1], 1), _tile_roll(low[1], 2)
    tinv = [None, -low[1], -low[2] + low[1] * l1s1,
            -low[3] + low[1] * _tile_roll(low[2], 1) + low[2] * l1s2 - low[1] * l1s1 * l1s2]
    col = lambda arr, h: arr[:, h:h + 1]

    heads = range(GDN_V_HEADS)
    xvs, m8s = [], []
    for h in heads:
        p = h // 2
        rv, rk = vsl(h) * col(beta, h), ksl(p) * col(bgc, h)
        xv, xk = rv, rk
        for d in range(1, dt):
            xv = xv + col(tinv[d], h) * _tile_roll(rv, d)
            xk = xk + col(tinv[d], h) * _tile_roll(rk, d)
        xvs.append(xv)
        m8s.append(jnp.where(live, xk, _tile_roll(qsl(p) * col(egc, h), dt)))
    tile = lambda a, bi: a[bi * SUBLANES:(bi + 1) * SUBLANES]
    aws = [jnp.concatenate([_dot(tile(m8s[h], bi), s_ref[bi, h]) for bi in range(bb)], axis=0) for h in heads]
    outs, uds = [], []
    for h in heads:
        u = jnp.where(live, xvs[h] - aws[h], 0.0)
        o = _tile_roll(aws[h], SUBLANES - dt) + col(qkd[0], h) * u
        for d in range(1, dt):
            o = o + col(qkd[d], h) * _tile_roll(u, d)
        outs.append(_rms(o, ong_ref[...]))
        uds.append(u * col(ekl, h))
    a_ref[...] = jnp.concatenate(outs, axis=1) * _silu(z_ref[...])
    tn = (((0,), (0,)), ((), ()))
    for h in heads:
        gt = jnp.broadcast_to(col(gtot, h), (R, LANES))
        kp = ksl(h // 2)
        for bi in range(bb):
            upd = lax.dot_general(tile(kp, bi).astype(BF16), tile(uds[h], bi).astype(BF16), tn,
                                  preferred_element_type=F32)
            so_ref[bi, h] = s_ref[bi, h] * gt[bi * SUBLANES:bi * SUBLANES + 1] + upd


def _gdn_sample(xqkv, z, ab, conv, s_all, s_prev, j, cw, alog, dtb, ong, dt):
    db = conv.shape[0]
    bb = min(8, db)
    r = bb * SUBLANES
    assert 2 * dt <= SUBLANES and dt == GDN_CONV
    row = lambda i: (i, 0)
    const = lambda i: (0, 0)
    state = (GDN_V_HEADS, GDN_HEAD_DIM, GDN_HEAD_DIM)
    s_spec = pl.BlockSpec((None, bb) + state, lambda i: (j, i, 0, 0, 0))
    vec = lambda a: jnp.pad(a.astype(F32), (0, LANES - a.shape[0])).reshape(1, LANES)
    args = [xqkv, z, ab, conv, s_all, cw, vec(alog), vec(dtb), ong.reshape(1, GDN_HEAD_DIM)]
    in_specs = [pl.BlockSpec((r, GDN_CONV_CH), row),
                pl.BlockSpec((r, GDN_V_W), row),
                pl.BlockSpec((r, LANES), row),
                pl.BlockSpec((bb, GDN_CONV - 1, GDN_CONV_CH), lambda i: (i, 0, 0)),
                s_spec,
                pl.BlockSpec(cw.shape, const),
                pl.BlockSpec((1, LANES), const),
                pl.BlockSpec((1, LANES), const),
                pl.BlockSpec((1, GDN_HEAD_DIM), const)]
    aliases = {}
    if s_prev is not None:
        args.append(s_prev)
        in_specs.append(pl.BlockSpec(memory_space=pl.ANY))
        aliases = {len(args) - 1: 1}
    return pl.pallas_call(
        functools.partial(_gdn_sample_kernel, bb=bb, dt=dt, has_prev=s_prev is not None),
        grid=(db // bb,),
        in_specs=in_specs,
        out_specs=[pl.BlockSpec((r, GDN_V_W), row), s_spec],
        out_shape=[jax.ShapeDtypeStruct(z.shape, F32), jax.ShapeDtypeStruct(s_all.shape, F32)],
        scratch_shapes=[pltpu.VMEM((r, GDN_CONV_CH), F32), pltpu.VMEM((r, GDN_CONV_CH), F32)],
        input_output_aliases=aliases,
        compiler_params=_params("arbitrary"),
        name="gdn_sample",
    )(*args)


def kernel(x_prompt, x_sample, cache_swa_k, cache_swa_v, state_gdn_conv, state_gdn_s, c_prompt, c_sample, norm_g,
           w_mod, b_mod, swa_w_in, swa_sinks, swa_w_out, gdn_w_in, gdn_conv_w, gdn_a_log, gdn_dt_bias, gdn_o_norm_g,
           gdn_w_out, final_norm_g):
    b, t, d = x_prompt.shape
    db, dt, _ = x_sample.shape
    assert d == D_MODEL and t % GDN_TILE == 0 and GDN_CONV - 1 <= dt <= SAMPLE_ROWS
    sr = SAMPLE_ROWS

    mod = _modulation(jnp.concatenate([c_prompt, c_sample], axis=0), w_mod, b_mod)
    xp = x_prompt.reshape(b * t, d)
    xs = jnp.pad(x_sample, ((0, 0), (0, sr - dt), (0, 0))).reshape(db * sr, d)

    swa_out = swa_w_out.astype(BF16)
    gdn_out = gdn_w_out.astype(BF16)
    swa_splits = ((0, SWA_Q_W), (SWA_Q_W, SWA_KV_W), (SWA_Q_W + SWA_KV_W, SWA_KV_W), (SWA_Q_W + 2 * SWA_KV_W, SWA_Q_W))
    swa_scales = (SWA_SCALE, 1.0, 1.0, 1.0)
    gdn_splits = ((0, GDN_CONV_CH), (GDN_CONV_CH, GDN_V_W), (GDN_CONV_CH + GDN_V_W, 2 * GDN_V_HEADS))
    gdn_scales = (1.0, 1.0, 1.0)

    n_swa = cache_swa_k.shape[0]
    feature_major = lambda c: jnp.transpose(c, (0, 1, 3, 4, 2)).reshape(n_swa, db, SWA_KV_W, WINDOW)
    position_major = lambda c: jnp.transpose(c.reshape(n_swa, db, SWA_KV_HEADS, SWA_HEAD_DIM, WINDOW), (0, 1, 4, 2, 3))
    ck_all, cv_all = feature_major(cache_swa_k), feature_major(cache_swa_v)

    kp_out, vp_out, cp_out, sp_out, cs_out = [], [], [], [], []
    s_sample = nk_all = nv_all = None
    for layer in range(DEPTH):
        j = layer // 2
        last = layer == DEPTH - 1
        shift, scale, gate = (mod[layer, :, i * d:(i + 1) * d] for i in range(3))
        p3 = lambda a: a[:b].reshape(b, 1, d)
        s_rows = lambda a: jnp.repeat(a[b:], sr, axis=0)
        g_l = norm_g[layer]
        if layer % 2 == 0:
            q, k, v, z = _inproj(xp, g_l, p3(shift), p3(scale), swa_w_in, j, swa_splits, swa_scales,
                                 (BF16, F32, F32, BF16), t, "swa_inproj_p")
            xp = _swa_prompt(swa_sinks[j], q, k, v, z, xp, p3(gate), swa_out[j], final_norm_g, b, t)
            keep = min(WINDOW, t)
            tail = lambda a: a.reshape(b, t, SWA_KV_W)[:, -keep:].reshape(b, keep, SWA_KV_HEADS, SWA_HEAD_DIM)
            kp_out.append(tail(k))
            vp_out.append(tail(v))

            q, k, v, z = _inproj(xs, g_l, s_rows(shift), s_rows(scale), swa_w_in, j, swa_splits, swa_scales,
                                 (F32,) * 4, 1, "swa_inproj_s")
            a, nk_all, nv_all = _swa_sample(swa_sinks[j], q, k, v, z, ck_all, cv_all, nk_all, nv_all, j, dt)
            xs = _outproj(a, swa_out[j], xs, s_rows(gate), final_norm_g, False, "swa_outproj_s")
        else:
            cw, alog, dtb, ong = gdn_conv_w[j], gdn_a_log[j], gdn_dt_bias[j], gdn_o_norm_g[j]
            xqkv, z, ab = _inproj(xp, g_l, p3(shift), p3(scale), gdn_w_in, j, gdn_splits, gdn_scales,
                                  (BF16, BF16, F32), t, "gdn_inproj_p")
            conv0 = jnp.zeros((b, GDN_CONV - 1, GDN_CONV_CH), F32)
            s0 = jnp.zeros((b, GDN_V_HEADS, GDN_HEAD_DIM, GDN_HEAD_DIM), F32)
            seq = lambda a: a.reshape(b, t, a.shape[-1])
            xp, s_new = _gdn(seq(xqkv), seq(z), seq(ab), seq(xp), p3(gate), conv0, s0, cw, alog, dtb, ong, gdn_out[j],
                             final_norm_g, last)
            xp = xp.reshape(b * t, d)
            cp_out.append(xqkv.reshape(b, t, GDN_CONV_CH)[:, -(GDN_CONV - 1):].astype(F32))
            sp_out.append(s_new)

            xqkv, z, ab = _inproj(xs, g_l, s_rows(shift), s_rows(scale), gdn_w_in, j, gdn_splits, gdn_scales,
                                  (F32,) * 3, 1, "gdn_inproj_s")
            a, s_sample = _gdn_sample(xqkv, z, ab, state_gdn_conv[j], state_gdn_s, s_sample, j, cw, alog, dtb, ong, dt)
            xs = _outproj(a, gdn_out[j], xs, s_rows(gate), final_norm_g, last, "gdn_outproj_s")
            cs_out.append(xqkv.reshape(db, sr, GDN_CONV_CH)[:, dt - (GDN_CONV - 1):dt])

    y_prompt = xp.reshape(b, t, d)
    y_sample = xs.reshape(db, sr, d)[:, :dt]
    return (y_prompt, y_sample, jnp.stack(kp_out), jnp.stack(vp_out), position_major(nk_all), position_major(nv_all),
            jnp.stack(cp_out), jnp.stack(sp_out), jnp.stack(cs_out), s_sample)
```

```python
import functools
import itertools

import jax
import jax.numpy as jnp
from jax import lax
from jax.experimental import pallas as pl
from jax.experimental.pallas import tpu as pltpu

F32 = jnp.float32
BF16 = jnp.bfloat16

D_MODEL = 1024
DEPTH = 4
EPS = 1e-6
SWA_HEADS = 16
SWA_KV_HEADS = 4
SWA_HEAD_DIM = 64
SWA_GROUP = SWA_HEADS // SWA_KV_HEADS
WINDOW = 128
SWA_Q_W = SWA_HEADS * SWA_HEAD_DIM
SWA_KV_W = SWA_KV_HEADS * SWA_HEAD_DIM
SWA_SCALE = SWA_HEAD_DIM ** -0.5
GDN_QK_HEADS = 8
GDN_V_HEADS = 16
GDN_HEAD_DIM = 128
GDN_K_W = GDN_QK_HEADS * GDN_HEAD_DIM
GDN_V_W = GDN_V_HEADS * GDN_HEAD_DIM
GDN_CONV = 4
GDN_CONV_CH = 2 * GDN_K_W + GDN_V_W
GDN_CHUNK = 64
LANES = 128
SUBLANES = 8
HALF = LANES // 2
GDN_TILE = 2 * GDN_CHUNK
SAMPLE_ROWS = SUBLANES
VMEM_LIMIT = 56 * 1024 * 1024

_NT = (((1,), (1,)), ((), ()))


def _dot(a, b):
    return jnp.dot(a.astype(BF16), b.astype(BF16), preferred_element_type=F32)


def _dot_nt(a, b):
    return lax.dot_general(a.astype(BF16), b.astype(BF16), _NT, preferred_element_type=F32)


def _dot_f32(a, b):
    return jnp.dot(a, b, preferred_element_type=F32, precision=lax.Precision.HIGHEST)


def _silu(x):
    return x * jax.nn.sigmoid(x)


def _iota(shape, dim):
    return lax.broadcasted_iota(jnp.int32, shape, dim)


def _rms(x, g):
    return x * lax.rsqrt(jnp.mean(x * x, axis=-1, keepdims=True) + EPS) * g


def _params(*sem):
    return pltpu.CompilerParams(dimension_semantics=sem, vmem_limit_bytes=VMEM_LIMIT)


def _mod_kernel(c_ref, w_ref, b_ref, o_ref):
    o_ref[...] = _dot(_silu(c_ref[...]), w_ref[...]) + b_ref[...]


def _modulation(c_all, w_mod, b_mod):
    r, d = c_all.shape
    n = w_mod.shape[-1]
    tn = 512
    return pl.pallas_call(
        _mod_kernel,
        grid=(DEPTH, n // tn),
        in_specs=[pl.BlockSpec((r, d), lambda l, j: (0, 0)),
                  pl.BlockSpec((None, d, tn), lambda l, j: (l, 0, j)),
                  pl.BlockSpec((None, 1, tn), lambda l, j: (l, 0, j))],
        out_specs=pl.BlockSpec((None, r, tn), lambda l, j: (l, 0, j)),
        out_shape=jax.ShapeDtypeStruct((DEPTH, r, n), F32),
        compiler_params=_params("arbitrary", "arbitrary"),
        name="adaln_modulation",
    )(c_all, w_mod, b_mod.reshape(DEPTH, 1, n))


def _inproj_kernel(x_ref, g_ref, sh_ref, sc_ref, w_ref, *o_refs, splits, scales, w_rows_out):
    h = _rms(x_ref[...], g_ref[...]) * (1.0 + sc_ref[...]) + sh_ref[...]
    hb = h.astype(BF16)
    for o_ref, (c0, n), s in zip(o_refs, splits, scales):
        for cc in range(0, n, 512):
            w = min(512, n - cc)
            if w_rows_out:
                r = _dot_nt(hb, w_ref[c0 + cc:c0 + cc + w, :])
            else:
                r = jnp.dot(hb, w_ref[:, c0 + cc:c0 + cc + w].astype(BF16), preferred_element_type=F32)
            o_ref[:, cc:cc + w] = (r * s if s != 1.0 else r).astype(o_ref.dtype)
        if n < o_ref.shape[1]:
            o_ref[:, n:] = jnp.zeros((o_ref.shape[0], o_ref.shape[1] - n), o_ref.dtype)


def _inproj(x, g, shift, scale, w_all, j, splits, scales, dtypes, rows_per_mod, name):
    m, d = x.shape
    assert w_all.shape[1] != w_all.shape[2] and d in w_all.shape[1:]
    pad = lambda n: -(-n // LANES) * LANES
    tm = min(256, m) if rows_per_mod == 1 else min(512, rows_per_mod)
    if rows_per_mod >= tm:
        per = rows_per_mod // tm
        mod_spec = pl.BlockSpec((None, 1, d), lambda i: (i // per, 0, 0))
    else:
        shift, scale = shift.reshape(m, d), scale.reshape(m, d)
        mod_spec = pl.BlockSpec((tm, d), lambda i: (i, 0))
    return pl.pallas_call(
        functools.partial(_inproj_kernel, splits=splits, scales=scales, w_rows_out=w_all.shape[2] == d),
        grid=(m // tm,),
        in_specs=[pl.BlockSpec((tm, d), lambda i: (i, 0)),
                  pl.BlockSpec((1, d), lambda i: (0, 0)),
                  mod_spec, mod_spec,
                  pl.BlockSpec((None,) + w_all.shape[1:], lambda i: (j, 0, 0), pipeline_mode=pl.Buffered(1))],
        out_specs=[pl.BlockSpec((tm, pad(n)), lambda i: (i, 0)) for _, n in splits],
        out_shape=[jax.ShapeDtypeStruct((m, pad(n)), dtype) for (_, n), dtype in zip(splits, dtypes)],
        compiler_params=_params("arbitrary"),
        name=name,
    )(x, g.reshape(1, d), shift, scale, w_all)


def _gdn_front_kernel(x_ref, g_ref, sh_ref, sc_ref, w_ref, cw_ref, qkv_ref, z_ref, ab_ref, tail_ref, carry, rbuf,
                      *, tiles_per_seq):
    tm = x_ref.shape[0]
    head = GDN_CONV - 1
    base = SUBLANES - head
    width = 4 * LANES

    @pl.when(pl.program_id(0) % tiles_per_seq == 0)
    def _():
        carry[...] = jnp.zeros(carry.shape, F32)

    h = _rms(x_ref[...], g_ref[...]) * (1.0 + sc_ref[...]) + sh_ref[...]
    hb = h.astype(BF16)
    proj = lambda c0: _dot_nt(hb, w_ref[c0:c0 + width, :])
    n_qkv, n_z = GDN_CONV_CH // width, GDN_V_W // width
    r_next = proj(0)
    for cb in range(n_qkv):
        cols = slice(cb * width, (cb + 1) * width)
        r = r_next
        if cb + 1 < n_qkv:
            r_next = proj((cb + 1) * width)
        if cb % 2 == 1:
            zc = cb // 2
            z_ref[:, zc * width:(zc + 1) * width] = proj(GDN_CONV_CH + zc * width).astype(z_ref.dtype)
        rbuf[0:SUBLANES, :] = carry[:, cols]
        rbuf[SUBLANES:SUBLANES + tm, :] = r
        y = rbuf[base:base + tm, :] * cw_ref[0:1, cols]
        for w in range(1, GDN_CONV):
            y = y + rbuf[base + w:base + w + tm, :] * cw_ref[w:w + 1, cols]
        carry[:, cols] = r[tm - SUBLANES:tm]
        tail_ref[:, cols] = r[tm - SUBLANES:tm]
        y = _silu(y)
        if cb * width < 2 * GDN_K_W:
            parts = []
            for p in range(width // GDN_HEAD_DIM):
                blk = y[:, p * GDN_HEAD_DIM:(p + 1) * GDN_HEAD_DIM]
                inv = lax.rsqrt(jnp.sum(blk * blk, axis=-1, keepdims=True) + EPS)
                parts.append(blk * (inv * GDN_HEAD_DIM ** -0.5 if cb * width < GDN_K_W else inv))
            y = jnp.concatenate(parts, axis=1)
        qkv_ref[:, cols] = y.astype(qkv_ref.dtype)
    assert n_z == n_qkv // 2
    nab = 2 * GDN_V_HEADS
    ab_ref[:, 0:nab] = _dot_nt(hb, w_ref[GDN_CONV_CH + GDN_V_W:GDN_CONV_CH + GDN_V_W + nab, :])
    ab_ref[:, nab:] = jnp.zeros((tm, LANES - nab), F32)


def _gdn_front(x, g, shift, scale, w_all_t, j, cw, t):
    m, d = x.shape
    tm = min(512, t)
    per = t // tm
    mod_spec = pl.BlockSpec((None, 1, d), lambda i: (i // per, 0, 0))
    row = lambda w: pl.BlockSpec((tm, w), lambda i: (i, 0))
    const = lambda i: (0, 0)
    return pl.pallas_call(
        functools.partial(_gdn_front_kernel, tiles_per_seq=per),
        grid=(m // tm,),
        in_specs=[row(d), pl.BlockSpec((1, d), const), mod_spec, mod_spec,
                  pl.BlockSpec((None,) + w_all_t.shape[1:], lambda i: (j, 0, 0), pipeline_mode=pl.Buffered(1)),
                  pl.BlockSpec(cw.shape, const)],
        out_specs=[row(GDN_CONV_CH), row(GDN_V_W), row(LANES),
                   pl.BlockSpec((None, SUBLANES, GDN_CONV_CH), lambda i: (i // per, 0, 0))],
        out_shape=[jax.ShapeDtypeStruct((m, GDN_CONV_CH), BF16), jax.ShapeDtypeStruct((m, GDN_V_W), BF16),
                   jax.ShapeDtypeStruct((m, LANES), F32), jax.ShapeDtypeStruct((m // t, SUBLANES, GDN_CONV_CH), F32)],
        scratch_shapes=[pltpu.VMEM((SUBLANES, GDN_CONV_CH), F32), pltpu.VMEM((SUBLANES + tm, 4 * LANES), F32)],
        compiler_params=_params("arbitrary"),
        name="gdn_front_p",
    )(x, g.reshape(1, d), shift, scale, w_all_t, cw)


def _finish(a, w_ref, x, gate, fg_ref, final_norm):
    y = x + gate * _dot(a, w_ref[...])
    return _rms(y, fg_ref[...]) if final_norm else y


def _outproj_kernel(a_ref, w_ref, x_ref, gate_ref, fg_ref, o_ref, *, final_norm):
    o_ref[...] = _finish(a_ref[...], w_ref, x_ref[...], gate_ref[...], fg_ref, final_norm)


def _outproj(a, w, x, gate_rows, fg, final_norm, name):
    m, k = a.shape
    d = x.shape[-1]
    tm = min(512, m)
    return pl.pallas_call(
        functools.partial(_outproj_kernel, final_norm=final_norm),
        grid=(m // tm,),
        in_specs=[pl.BlockSpec((tm, k), lambda i: (i, 0)),
                  pl.BlockSpec((k, d), lambda i: (0, 0)),
                  pl.BlockSpec((tm, d), lambda i: (i, 0)),
                  pl.BlockSpec((tm, d), lambda i: (i, 0)),
                  pl.BlockSpec((1, d), lambda i: (0, 0))],
        out_specs=pl.BlockSpec((tm, d), lambda i: (i, 0)),
        out_shape=jax.ShapeDtypeStruct((m, d), F32),
        compiler_params=_params("arbitrary"),
        name=name,
    )(a, w, x, gate_rows, fg.reshape(1, d))


def _swa_block(q, kcat, vcat, mask4, sinks_ref):
    nq = q.shape[0]
    lo = _iota((1, LANES), 1) < HALF
    hi = jnp.logical_not(lo)
    row = _iota((SWA_GROUP * nq, 1), 0)
    kvs = range(SWA_KV_HEADS)
    blk = lambda a, pb: a[:, pb * LANES:(pb + 1) * LANES]
    q4s = []
    for kh in kvs:
        parts = []
        for g in range(SWA_GROUP):
            qg = blk(q, 2 * kh + g // 2)
            if g % 2 != kh % 2:
                qg = pltpu.roll(qg, HALF, axis=1)
            parts.append(jnp.where(lo if kh % 2 == 0 else hi, qg, 0.0))
        q4s.append(jnp.concatenate(parts, axis=0))
    logits = [_dot_nt(q4, blk(kcat, kh // 2)) for kh, q4 in zip(kvs, q4s)]
    es, invs = [], []
    for kh, lg in zip(kvs, logits):
        sink = jnp.full((SWA_GROUP * nq, 1), sinks_ref[SWA_GROUP * kh], F32)
        for g in range(1, SWA_GROUP):
            sink = jnp.where(row >= g * nq, sinks_ref[SWA_GROUP * kh + g], sink)
        lg = jnp.where(mask4, lg, -jnp.inf)
        m = jnp.maximum(jnp.max(lg, axis=-1, keepdims=True), sink)
        e = jnp.exp(lg - m)
        es.append(e)
        invs.append(1.0 / (jnp.sum(e, axis=-1, keepdims=True) + jnp.exp(sink - m)))
    o4s = [_dot(e, blk(vcat, kh // 2)) * inv for kh, e, inv in zip(kvs, es, invs)]
    outs = [None] * (SWA_HEADS // 2)
    for kh, o4 in zip(kvs, o4s):
        for g in range(SWA_GROUP):
            piece = o4[g * nq:(g + 1) * nq]
            if g % 2 != kh % 2:
                piece = pltpu.roll(piece, HALF, axis=1)
            piece = jnp.where(lo if g % 2 == 0 else hi, piece, 0.0)
            qb = 2 * kh + g // 2
            outs[qb] = piece if outs[qb] is None else outs[qb] + piece
    return jnp.concatenate(outs, axis=1)


def _band_mask(nq, nk, jmin):
    i = _iota((SWA_GROUP * nq, nk), 0) % nq
    j = _iota((SWA_GROUP * nq, nk), 1)
    return (j >= i) & (j <= i + WINDOW) & (j >= jmin)


def _swa_prompt_kernel(sinks_ref, q_ref, kp_ref, vp_ref, k_ref, v_ref, z_ref, x_ref, gate_ref, w_ref, fg_ref,
                       o_ref, kbuf, vbuf, abuf, *, tq):
    t = pl.program_id(1)
    kbuf[0:WINDOW] = kp_ref[...]
    kbuf[WINDOW:WINDOW + tq] = k_ref[...]
    vbuf[0:WINDOW] = vp_ref[...]
    vbuf[WINDOW:WINDOW + tq] = v_ref[...]

    def body(sb, carry):
        r = pl.multiple_of(sb * WINDOW, WINDOW)
        jmin = jnp.where((t == 0) & (sb == 0), WINDOW, 0)
        mask4 = _band_mask(WINDOW, 2 * WINDOW, jmin)
        abuf[pl.ds(r, WINDOW), :] = _swa_block(q_ref[pl.ds(r, WINDOW), :].astype(F32), kbuf[pl.ds(r, 2 * WINDOW), :],
                                               vbuf[pl.ds(r, 2 * WINDOW), :], mask4, sinks_ref)
        return carry

    lax.fori_loop(0, tq // WINDOW, body, 0)
    a = abuf[...] * _silu(z_ref[...].astype(F32))
    o_ref[...] = _finish(a, w_ref, x_ref[...], gate_ref[...], fg_ref, False)


def _swa_prompt(sinks, q, k, v, z, x, gate, w_out, fg, b, t):
    d = x.shape[-1]
    tq = min(512, t)
    nt = t // tq
    sub = tq // WINDOW
    row = lambda bi, ti: (bi * nt + ti, 0)
    prev = lambda bi, ti: (bi * nt * sub + jnp.maximum(ti * sub - 1, 0), 0)
    const = lambda bi, ti: (0, 0)
    return pl.pallas_call(
        functools.partial(_swa_prompt_kernel, tq=tq),
        grid=(b, nt),
        in_specs=[pl.BlockSpec(memory_space=pltpu.SMEM),
                  pl.BlockSpec((tq, SWA_Q_W), row),
                  pl.BlockSpec((WINDOW, SWA_KV_W), prev),
                  pl.BlockSpec((WINDOW, SWA_KV_W), prev),
                  pl.BlockSpec((tq, SWA_KV_W), row),
                  pl.BlockSpec((tq, SWA_KV_W), row),
                  pl.BlockSpec((tq, SWA_Q_W), row),
                  pl.BlockSpec((tq, d), row),
                  pl.BlockSpec((None, 1, d), lambda bi, ti: (bi, 0, 0)),
                  pl.BlockSpec(w_out.shape, const),
                  pl.BlockSpec((1, d), const)],
        out_specs=pl.BlockSpec((tq, d), row),
        out_shape=jax.ShapeDtypeStruct(x.shape, F32),
        scratch_shapes=[pltpu.VMEM((WINDOW + tq, SWA_KV_W), F32),
                        pltpu.VMEM((WINDOW + tq, SWA_KV_W), F32),
                        pltpu.VMEM((tq, SWA_Q_W), F32)],
        compiler_params=_params("arbitrary", "arbitrary"),
        name="swa_prompt",
    )(sinks, q, k, v, k, v, z, x, gate, w_out, fg.reshape(1, d))


def _swa_sample_kernel(sinks_ref, q_ref, k_ref, v_ref, z_ref, ck_ref, cv_ref, *rest, bb, dt, has_prev):
    a_ref, nk_ref, nv_ref = rest[2:] if has_prev else rest
    nq, grp = SAMPLE_ROWS, SWA_GROUP
    lane = _iota((1, LANES), 1)
    lo = lane < HALF
    hi = jnp.logical_not(lo)
    row = _iota((grp * nq, 1), 0)
    qi = row % nq
    see_cache, see_own = lane >= qi, lane <= qi
    zpad = jnp.zeros((WINDOW - nq, LANES), F32)
    rows = lambda bi: slice(bi * nq, (bi + 1) * nq)
    blk = lambda pb: slice(pb * LANES, (pb + 1) * LANES)
    units = [(bi, kh) for bi in range(bb) for kh in range(SWA_KV_HEADS)]

    q4s = []
    for bi, kh in units:
        keep = lo if kh % 2 == 0 else hi
        parts = []
        for g in range(grp):
            qg = q_ref[rows(bi), blk(2 * kh + g // 2)]
            if g % 2 != kh % 2:
                qg = pltpu.roll(qg, HALF, axis=1)
            parts.append(jnp.where(keep, qg, 0.0))
        q4s.append(jnp.concatenate(parts, axis=0))
    own = lambda ref, bi, pb: jnp.concatenate([ref[rows(bi), blk(pb)], zpad], axis=0)
    lcs = [_dot(q4, ck_ref[bi, blk(kh // 2), :]) for (bi, kh), q4 in zip(units, q4s)]
    los = [_dot_nt(q4, own(k_ref, bi, kh // 2)) for (bi, kh), q4 in zip(units, q4s)]
    ecs, eos, invs = [], [], []
    for (bi, kh), lc, lw in zip(units, lcs, los):
        sink = jnp.full((grp * nq, 1), sinks_ref[grp * kh], F32)
        for g in range(1, grp):
            sink = jnp.where(row >= g * nq, sinks_ref[grp * kh + g], sink)
        lc = jnp.where(see_cache, lc, -jnp.inf)
        lw = jnp.where(see_own, lw, -jnp.inf)
        m = jnp.maximum(jnp.maximum(jnp.max(lc, axis=-1, keepdims=True), jnp.max(lw, axis=-1, keepdims=True)), sink)
        ec, eo = jnp.exp(lc - m), jnp.exp(lw - m)
        den = jnp.sum(ec, axis=-1, keepdims=True) + jnp.sum(eo, axis=-1, keepdims=True) + jnp.exp(sink - m)
        ecs.append(ec)
        eos.append(eo)
        invs.append(1.0 / den)
    o4s = [(_dot_nt(ec, cv_ref[bi, blk(kh // 2), :]) + _dot(eo, own(v_ref, bi, kh // 2))) * inv
           for (bi, kh), ec, eo, inv in zip(units, ecs, eos, invs)]
    for bi in range(bb):
        outs = [None] * (SWA_HEADS // 2)
        for kh in range(SWA_KV_HEADS):
            o4 = o4s[bi * SWA_KV_HEADS + kh]
            for g in range(grp):
                piece = o4[g * nq:(g + 1) * nq]
                if g % 2 != kh % 2:
                    piece = pltpu.roll(piece, HALF, axis=1)
                piece = jnp.where(lo if g % 2 == 0 else hi, piece, 0.0)
                qb = 2 * kh + g // 2
                outs[qb] = piece if outs[qb] is None else outs[qb] + piece
        a_ref[rows(bi), :] = jnp.concatenate(outs, axis=1) * _silu(z_ref[rows(bi), :])

    fresh = lane >= WINDOW - dt
    for src, cache, dst in ((k_ref, ck_ref, nk_ref), (v_ref, cv_ref, nv_ref)):
        for pb in range(SWA_KV_W // LANES):
            new = src[:, blk(pb)]
            if bb * nq < LANES:
                new = jnp.concatenate([new, jnp.zeros((LANES - bb * nq, LANES), F32)], axis=0)
            new_t = new.T
            for bi in range(bb):
                cols = pltpu.roll(new_t, (WINDOW - dt - bi * nq) % LANES, axis=1)
                dst[bi, blk(pb), :] = jnp.where(fresh, cols, pltpu.roll(cache[bi, blk(pb), :], WINDOW - dt, axis=1))


def _swa_sample(sinks, q, k, v, z, ck_all, cv_all, nk_prev, nv_prev, j, dt):
    db = ck_all.shape[1]
    bb = min(16, db)
    nq = SAMPLE_ROWS
    assert 2 * dt <= nq
    row = lambda i: (i, 0)
    c_spec = pl.BlockSpec((None, bb, SWA_KV_W, WINDOW), lambda i: (j, i, 0, 0))
    args = [sinks, q, k, v, z, ck_all, cv_all]
    in_specs = [pl.BlockSpec(memory_space=pltpu.SMEM),
                pl.BlockSpec((bb * nq, SWA_Q_W), row),
                pl.BlockSpec((bb * nq, SWA_KV_W), row),
                pl.BlockSpec((bb * nq, SWA_KV_W), row),
                pl.BlockSpec((bb * nq, SWA_Q_W), row),
                c_spec, c_spec]
    aliases = {}
    if nk_prev is not None:
        args += [nk_prev, nv_prev]
        in_specs += [pl.BlockSpec(memory_space=pl.ANY)] * 2
        aliases = {len(args) - 2: 1, len(args) - 1: 2}
    return pl.pallas_call(
        functools.partial(_swa_sample_kernel, bb=bb, dt=dt, has_prev=nk_prev is not None),
        grid=(db // bb,),
        in_specs=in_specs,
        out_specs=[pl.BlockSpec((bb * nq, SWA_Q_W), row), c_spec, c_spec],
        out_shape=[jax.ShapeDtypeStruct(q.shape, F32),
                   jax.ShapeDtypeStruct(ck_all.shape, F32),
                   jax.ShapeDtypeStruct(cv_all.shape, F32)],
        input_output_aliases=aliases,
        compiler_params=_params("arbitrary"),
        name="swa_sample",
    )(*args)


def _block_mul(a, b):
    size, width = b.shape
    b16 = b.astype(BF16)
    zero = jnp.zeros_like(b16)
    blk = _iota((1, width), 1) // size
    return _dot(a, jnp.concatenate([jnp.where(blk == r, b16, zero) for r in range(width // size)], axis=0))


def _diag2(a, b):
    return jnp.concatenate([jnp.concatenate([a, jnp.zeros_like(b)], axis=1),
                            jnp.concatenate([jnp.zeros_like(a), b], axis=1)], axis=0)


def _gdn_kernel(qkv, z_ref, ab_ref, x_ref, gate_ref, s0_ref, alog_ref, dtb_ref, ong_ref, w_ref, fg_ref, o_ref, s_ref,
                obuf, *, nb, final_norm):
    nchunks = GDN_TILE // GDN_CHUNK
    R, C, HD = GDN_TILE, GDN_CHUNK, GDN_HEAD_DIM
    t = pl.program_id(1)
    seqs = range(nb)

    @pl.when(t == 0)
    def _():
        s_ref[...] = s0_ref[...]

    ri, ci = _iota((R, R), 0), _iota((R, R), 1)
    same = (ri // C) == (ci // C)
    tri, ones = jnp.where(same & (ci <= ri), 1.0, 0.0), jnp.where(same, 1.0, 0.0)
    gcs, betas, egcs, bgcs, gcTs, glTs, dlTs = ([None] * nb for _ in range(7))

    def prologue(sq):
        ab = ab_ref[sq]
        g = -jnp.exp(alog_ref[...]) * jax.nn.softplus(ab + dtb_ref[...])
        beta = pltpu.roll(jax.nn.sigmoid(ab), LANES - GDN_V_HEADS, axis=1)
        gc = _dot_f32(tri, g)
        gl = _dot_f32(ones, g)
        egc = jnp.exp(gc)
        gcT, glT = gc.T, gl.T
        gcs[sq], betas[sq], egcs[sq], bgcs[sq] = gc, beta, egc, beta * egc
        gcTs[sq], glTs[sq], dlTs[sq] = gcT, glT, glT - gcT
        yield

    lane = _iota((1, LANES), 1)
    lo = lane < HALF
    jj = lane % C
    ii = _iota((C, 1), 0)
    incl, strict = ii >= jj, ii > jj
    zc = jnp.zeros((C, HD), F32)

    def col2(arr, r0, h0):
        return jnp.where(lo, arr[r0:r0 + C, h0:h0 + 1], arr[r0:r0 + C, h0 + 1:h0 + 2])

    def pick(c, row_a, row_b):
        if c == 0:
            return jnp.where(lo, row_a, pltpu.roll(row_b, HALF, axis=1))
        return jnp.where(lo, pltpu.roll(row_a, HALF, axis=1), row_b)

    pairs = range(GDN_QK_HEADS)
    heads = range(GDN_V_HEADS)
    units = [(c, p) for c in range(nchunks) for p in pairs]
    quads = [(c, pp) for c in range(nchunks) for pp in range(GDN_QK_HEADS // 2)]
    qsl = lambda sq, c, p: qkv[sq, c * C:(c + 1) * C, p * HD:(p + 1) * HD]
    ksl = lambda sq, c, p: qkv[sq, c * C:(c + 1) * C, GDN_K_W + p * HD:GDN_K_W + (p + 1) * HD]
    vsl = lambda sq, c, h: qkv[sq, c * C:(c + 1) * C, 2 * GDN_K_W + h * HD:2 * GDN_K_W + (h + 1) * HD]
    col = lambda arr, c, h: arr[c * C:(c + 1) * C, h:h + 1]
    jq = _iota((1, 2 * LANES), 1) % C
    kts, lows, qks, sols = ([None] * nb for _ in range(4))

    def products(sq):
        kts[sq] = [qkv[sq, :, GDN_K_W + p * HD:GDN_K_W + (p + 1) * HD].astype(F32).T for p in pairs]
        yield
        ms = [_dot_nt(jnp.concatenate([ksl(sq, c, p), qsl(sq, c, p)], axis=0), jnp.concatenate([ksl(sq, c, p)] * 2, axis=0))
              for c, p in units]
        yield
        lows[sq], qks[sq] = [], []
        for (c, p), m in zip(units, ms):
            h0 = 2 * p
            row_gc = pick(c, gcTs[sq][h0:h0 + 1, :], gcTs[sq][h0 + 1:h0 + 2, :])
            decay = jnp.exp(jnp.where(incl, col2(gcs[sq], c * C, h0) - row_gc, -jnp.inf))
            lows[sq].append(jnp.where(strict, m[0:C] * decay, 0.0) * col2(betas[sq], c * C, h0))
            qks[sq].append(m[C:2 * C] * decay)
            if p % 4 == 3:
                yield

    def inverse(sq):
        low4 = [jnp.concatenate([lows[sq][c * GDN_QK_HEADS + 2 * pp], lows[sq][c * GDN_QK_HEADS + 2 * pp + 1]], axis=1)
                for c, pp in quads]
        invs = [jnp.where(ii == jq, 1.0, 0.0) - jnp.where((ii // 2) == (jq // 2), low, 0.0) for low in low4]
        yield
        s = 2
        while s < C:
            below = ((ii // s) % 2 == 1) & ((jq // s) == (ii // s) - 1)
            eps, new = [], []
            for i, (low, inv) in enumerate(zip(low4, invs)):
                eps.append(_block_mul(jnp.where(below, low, 0.0), inv))
                if i % 2:
                    yield
            for i, (inv, ep) in enumerate(zip(invs, eps)):
                new.append(inv - _block_mul(inv, ep))
                if i % 2:
                    yield
            invs = new
            s *= 2
        sols[sq] = []
        for (c, pp), inv in zip(quads, invs):
            rhs = []
            for idx in range(4):
                h = 4 * pp + idx
                piece = [vsl(sq, c, h) * col(betas[sq], c, h), ksl(sq, c, h // 2) * col(bgcs[sq], c, h)]
                rhs.append(jnp.concatenate([zc] * (2 * idx) + piece + [zc] * (6 - 2 * idx), axis=1))
            sols[sq].append(_dot(inv, jnp.concatenate(rhs, axis=0)))
            if pp % 2 == 1:
                yield

    def state(sq):
        for c in range(nchunks):
            in_chunk = (lane // C) == c
            quad = lambda h: sols[sq][c * (GDN_QK_HEADS // 2) + h // 4]
            w_v = lambda h: quad(h)[:, 2 * (h % 4) * HD:(2 * (h % 4) + 1) * HD]
            w_k = lambda h: quad(h)[:, (2 * (h % 4) + 1) * HD:(2 * (h % 4) + 2) * HD]
            a_s = []
            for p in pairs:
                top = jnp.concatenate([w_k(2 * p), w_k(2 * p + 1)], axis=1)
                bot = jnp.concatenate([qsl(sq, c, p) * col(egcs[sq], c, 2 * p), qsl(sq, c, p) * col(egcs[sq], c, 2 * p + 1)],
                                      axis=1)
                a_s.append(_dot(jnp.concatenate([top, bot], axis=0), _diag2(s_ref[sq, 2 * p], s_ref[sq, 2 * p + 1])))
            yield
            us = [[w_v(2 * p + idx) - a[0:C, idx * HD:(idx + 1) * HD] for idx in range(2)] for p, a in zip(pairs, a_s)]
            for p, a, u in zip(pairs, a_s, us):
                obuf[sq, c * C:(c + 1) * C, 2 * p * HD:(2 * p + 2) * HD] = (
                    a[C:2 * C] + _dot(qks[sq][c * GDN_QK_HEADS + p], _diag2(*u)))
            yield
            for p, u in zip(pairs, us):
                kd, ue, gt = [], [], []
                for idx in range(2):
                    h = 2 * p + idx
                    kd.append(kts[sq][p] * jnp.exp(jnp.where(in_chunk, dlTs[sq][h:h + 1, :], -jnp.inf)))
                    ue.append(jnp.concatenate([u[idx], zc] if c == 0 else [zc, u[idx]], axis=0))
                    gt.append(jnp.exp(pick(c, glTs[sq][h:h + 1, :], glTs[sq][h:h + 1, :])))
                upd = _dot(jnp.concatenate(kd, axis=1), _diag2(*ue))
                for idx in range(2):
                    h = 2 * p + idx
                    s_ref[sq, h] = s_ref[sq, h] * gt[idx] + upd[:, idx * HD:(idx + 1) * HD]
                if p % 4 == 3:
                    yield

    def run(*gens):
        gens = list(gens)
        while gens:
            for gen in list(gens):
                if next(gen, "done") == "done":
                    gens.remove(gen)

    for k in range(nb + 2):
        live = [itertools.chain(prologue(k), products(k))] if k < nb else []
        live += [inverse(k - 1)] if 0 <= k - 1 < nb else []
        live += [state(k - 2)] if 0 <= k - 2 < nb else []
        run(*live)

    rows = []
    for sq in seqs:
        o = obuf[sq]
        parts = [_rms(o[:, h * HD:(h + 1) * HD], ong_ref[...]) for h in heads]
        rows.append(jnp.concatenate(parts, axis=1) * _silu(z_ref[sq].astype(F32)))
    out = _dot(jnp.concatenate(rows, axis=0), w_ref[...])
    for sq in seqs:
        y = x_ref[sq] + gate_ref[sq] * out[sq * R:(sq + 1) * R]
        o_ref[sq] = _rms(y, fg_ref[...]) if final_norm else y


def _gdn(qkv, z, ab, x, gate, s0, alog, dtb, ong, w_out, fg, final_norm):
    b, t, d = x.shape
    nb = 2 if b % 2 == 0 else 1
    rows_in = GDN_TILE
    tile = lambda w: pl.BlockSpec((nb, rows_in, w), lambda bi, ti: (bi, ti, 0))
    const = lambda bi, ti: (0, 0)
    per_b3 = lambda bi, ti: (bi, 0, 0)
    per_b4 = lambda bi, ti: (bi, 0, 0, 0)
    vec = lambda a: jnp.pad(a.astype(F32), (0, LANES - a.shape[0])).reshape(1, LANES)
    state = (GDN_V_HEADS, GDN_HEAD_DIM, GDN_HEAD_DIM)
    return pl.pallas_call(
        functools.partial(_gdn_kernel, nb=nb, final_norm=final_norm),
        grid=(b // nb, t // rows_in),
        in_specs=[tile(GDN_CONV_CH), tile(GDN_V_W), tile(LANES), tile(d),
                  pl.BlockSpec((nb, 1, d), per_b3),
                  pl.BlockSpec((nb,) + state, per_b4),
                  pl.BlockSpec((1, LANES), const),
                  pl.BlockSpec((1, LANES), const),
                  pl.BlockSpec((1, GDN_HEAD_DIM), const),
                  pl.BlockSpec(w_out.shape, const),
                  pl.BlockSpec((1, d), const)],
        out_specs=[tile(d), pl.BlockSpec((nb,) + state, per_b4)],
        out_shape=[jax.ShapeDtypeStruct(x.shape, F32),
                   jax.ShapeDtypeStruct((b,) + state, F32)],
        scratch_shapes=[pltpu.VMEM((nb, GDN_TILE, GDN_V_W), F32)],
        compiler_params=_params("arbitrary", "arbitrary"),
        name="gdn_prompt",
    )(qkv, z, ab, x, gate, s0, vec(alog), vec(dtb), ong.reshape(1, GDN_HEAD_DIM), w_out, fg.reshape(1, d))


def _tile_roll(a, d):
    r, w = a.shape
    return pltpu.roll(a.reshape(r // SUBLANES, SUBLANES, w), d, axis=1).reshape(r, w)


def _gdn_sample_kernel(xqkv_ref, z_ref, ab_ref, conv_ref, s_ref, cw_ref, alog_ref, dtb_ref, ong_ref, *rest,
                       bb, dt, has_prev):
    a_ref, so_ref, st8, qkv = rest[1:] if has_prev else rest
    R, HD = bb * SUBLANES, GDN_HEAD_DIM
    tix = _iota((R, 1), 0) % SUBLANES
    live = tix < dt
    head = GDN_CONV - 1
    lane = _iota((1, LANES), 1)

    st8[...] = jnp.zeros(st8.shape, F32)
    for bi in range(bb):
        st8[(bi + 1) * SUBLANES - head:(bi + 1) * SUBLANES, :] = conv_ref[bi]
    for cb in range(GDN_CONV_CH // GDN_K_W):
        cols = slice(cb * GDN_K_W, (cb + 1) * GDN_K_W)
        x, st = xqkv_ref[:, cols], st8[:, cols]
        y = x * cw_ref[head:head + 1, cols]
        for d in range(1, GDN_CONV):
            y = y + jnp.where(tix >= d, _tile_roll(x, d), _tile_roll(st, d)) * cw_ref[head - d:head - d + 1, cols]
        y = _silu(y)
        if cb < 2:
            for p in range(GDN_QK_HEADS):
                blk = y[:, p * HD:(p + 1) * HD]
                inv = lax.rsqrt(jnp.sum(blk * blk, axis=-1, keepdims=True) + EPS)
                if cb == 0:
                    inv = inv * HD ** -0.5
                qkv[:, cb * GDN_K_W + p * HD:cb * GDN_K_W + (p + 1) * HD] = blk * inv
        else:
            qkv[:, cols] = y

    ab = ab_ref[...]
    g = jnp.where(live, -jnp.exp(alog_ref[...]) * jax.nn.softplus(ab + dtb_ref[...]), 0.0)
    beta = jnp.where(live, pltpu.roll(jax.nn.sigmoid(ab), LANES - GDN_V_HEADS, axis=1), 0.0)
    gc, gl = g, g
    for d in range(1, SUBLANES):
        gl = gl + _tile_roll(g, d)
        if d < dt:
            gc = gc + jnp.where(tix >= d, _tile_roll(g, d), 0.0)
    egc = jnp.exp(gc)
    bgc = beta * egc
    ekl = jnp.where(live, jnp.exp(jnp.where(live, gl - gc, 0.0)), 0.0)
    gtot = jnp.exp(gl)
    dec = [None] + [jnp.where(tix >= d, jnp.exp(jnp.where(tix >= d, gc - _tile_roll(gc, d), 0.0)), 0.0)
                    for d in range(1, dt)]

    zero = jnp.zeros((R, LANES), F32)
    kk, qk = [zero] * dt, [zero] * dt
    qsl = lambda p: qkv[:, p * HD:(p + 1) * HD]
    ksl = lambda p: qkv[:, GDN_K_W + p * HD:GDN_K_W + (p + 1) * HD]
    vsl = lambda h: qkv[:, 2 * GDN_K_W + h * HD:2 * GDN_K_W + (h + 1) * HD]
    for p in range(GDN_QK_HEADS):
        mine = (lane // 2) == p
        for d in range(dt):
            ks = ksl(p) if d == 0 else _tile_roll(ksl(p), d)
            qk[d] = jnp.where(mine, jnp.sum(qsl(p) * ks, axis=-1, keepdims=True), qk[d])
            if d:
                kk[d] = jnp.where(mine, jnp.sum(ksl(p) * ks, axis=-1, keepdims=True), kk[d])
    low = [None] + [beta * kk[d] * dec[d] for d in range(1, dt)]
    qkd = [qk[0]] + [qk[d] * dec[d] for d in range(1, dt)]
    l1s1, l1s2 = _tile_roll(low[1], 1), _tile_roll(low[1], 2)
    tinv = [None, -low[1], -low[2] + low[1] * l1s1,
            -low[3] + low[1] * _tile_roll(low[2], 1) + low[2] * l1s2 - low[1] * l1s1 * l1s2]
    heads = range(GDN_V_HEADS)
    spread = lambda arr: [jnp.broadcast_to(arr[:, h:h + 1], (R, LANES)) for h in heads]
    s_beta, s_bgc, s_egc, s_ekl, s_gtot = spread(beta), spread(bgc), spread(egc), spread(ekl), spread(gtot)
    s_tinv = [None] + [spread(tinv[d]) for d in range(1, dt)]
    s_qkd = [spread(qkd[d]) for d in range(dt)]

    xvs, m8s = [], []
    for h in heads:
        p = h // 2
        rv, rk = vsl(h) * s_beta[h], ksl(p) * s_bgc[h]
        xv, xk = rv, rk
        for d in range(1, dt):
            xv = xv + s_tinv[d][h] * _tile_roll(rv, d)
            xk = xk + s_tinv[d][h] * _tile_roll(rk, d)
        xvs.append(xv)
        m8s.append(jnp.where(live, xk, _tile_roll(qsl(p) * s_egc[h], dt)))
    tile = lambda a, bi: a[bi * SUBLANES:(bi + 1) * SUBLANES]
    aws = [jnp.concatenate([_dot(tile(m8s[h], bi), s_ref[bi, h]) for bi in range(bb)], axis=0) for h in heads]
    outs, uds = [], []
    for h in heads:
        u = jnp.where(live, xvs[h] - aws[h], 0.0)
        o = _tile_roll(aws[h], SUBLANES - dt) + s_qkd[0][h] * u
        for d in range(1, dt):
            o = o + s_qkd[d][h] * _tile_roll(u, d)
        outs.append(_rms(o, ong_ref[...]))
        uds.append(u * s_ekl[h])
    a_ref[...] = jnp.concatenate(outs, axis=1) * _silu(z_ref[...])
    tn = (((0,), (0,)), ((), ()))
    for h in heads:
        kp = ksl(h // 2)
        for bi in range(bb):
            upd = lax.dot_general(tile(kp, bi).astype(BF16), tile(uds[h], bi).astype(BF16), tn,
                                  preferred_element_type=F32)
            so_ref[bi, h] = s_ref[bi, h] * s_gtot[h][bi * SUBLANES:bi * SUBLANES + 1] + upd


def _gdn_sample(xqkv, z, ab, conv, s_all, s_prev, j, cw, alog, dtb, ong, dt):
    db = conv.shape[0]
    bb = min(8, db)
    r = bb * SUBLANES
    assert 2 * dt <= SUBLANES and dt == GDN_CONV
    row = lambda i: (i, 0)
    const = lambda i: (0, 0)
    state = (GDN_V_HEADS, GDN_HEAD_DIM, GDN_HEAD_DIM)
    s_spec = pl.BlockSpec((None, bb) + state, lambda i: (j, i, 0, 0, 0))
    vec = lambda a: jnp.pad(a.astype(F32), (0, LANES - a.shape[0])).reshape(1, LANES)
    args = [xqkv, z, ab, conv, s_all, cw, vec(alog), vec(dtb), ong.reshape(1, GDN_HEAD_DIM)]
    in_specs = [pl.BlockSpec((r, GDN_CONV_CH), row),
                pl.BlockSpec((r, GDN_V_W), row),
                pl.BlockSpec((r, LANES), row),
                pl.BlockSpec((bb, GDN_CONV - 1, GDN_CONV_CH), lambda i: (i, 0, 0)),
                s_spec,
                pl.BlockSpec(cw.shape, const),
                pl.BlockSpec((1, LANES), const),
                pl.BlockSpec((1, LANES), const),
                pl.BlockSpec((1, GDN_HEAD_DIM), const)]
    aliases = {}
    if s_prev is not None:
        args.append(s_prev)
        in_specs.append(pl.BlockSpec(memory_space=pl.ANY))
        aliases = {len(args) - 1: 1}
    return pl.pallas_call(
        functools.partial(_gdn_sample_kernel, bb=bb, dt=dt, has_prev=s_prev is not None),
        grid=(db // bb,),
        in_specs=in_specs,
        out_specs=[pl.BlockSpec((r, GDN_V_W), row), s_spec],
        out_shape=[jax.ShapeDtypeStruct(z.shape, F32), jax.ShapeDtypeStruct(s_all.shape, F32)],
        scratch_shapes=[pltpu.VMEM((r, GDN_CONV_CH), F32), pltpu.VMEM((r, GDN_CONV_CH), F32)],
        input_output_aliases=aliases,
        compiler_params=_params("arbitrary"),
        name="gdn_sample",
    )(*args)


def kernel(x_prompt, x_sample, cache_swa_k, cache_swa_v, state_gdn_conv, state_gdn_s, c_prompt, c_sample, norm_g,
           w_mod, b_mod, swa_w_in, swa_sinks, swa_w_out, gdn_w_in, gdn_conv_w, gdn_a_log, gdn_dt_bias, gdn_o_norm_g,
           gdn_w_out, final_norm_g):
    b, t, d = x_prompt.shape
    db, dt, _ = x_sample.shape
    assert d == D_MODEL and t % GDN_TILE == 0 and GDN_CONV - 1 <= dt <= SAMPLE_ROWS
    sr = SAMPLE_ROWS

    mod = _modulation(jnp.concatenate([c_prompt, c_sample], axis=0), w_mod, b_mod)
    xp = x_prompt.reshape(b * t, d)
    xs = jnp.pad(x_sample, ((0, 0), (0, sr - dt), (0, 0))).reshape(db * sr, d)

    gdn_in_t = jnp.transpose(gdn_w_in, (0, 2, 1))
    swa_out = swa_w_out.astype(BF16)
    gdn_out = gdn_w_out.astype(BF16)
    swa_splits = ((0, SWA_Q_W), (SWA_Q_W, SWA_KV_W), (SWA_Q_W + SWA_KV_W, SWA_KV_W), (SWA_Q_W + 2 * SWA_KV_W, SWA_Q_W))
    swa_scales = (SWA_SCALE, 1.0, 1.0, 1.0)
    gdn_splits = ((0, GDN_CONV_CH), (GDN_CONV_CH, GDN_V_W), (GDN_CONV_CH + GDN_V_W, 2 * GDN_V_HEADS))
    gdn_scales = (1.0, 1.0, 1.0)

    n_swa = cache_swa_k.shape[0]
    feature_major = lambda c: jnp.transpose(c, (0, 1, 3, 4, 2)).reshape(n_swa, db, SWA_KV_W, WINDOW)
    position_major = lambda c: jnp.transpose(c.reshape(n_swa, db, SWA_KV_HEADS, SWA_HEAD_DIM, WINDOW), (0, 1, 4, 2, 3))
    ck_all, cv_all = feature_major(cache_swa_k), feature_major(cache_swa_v)

    kp_out, vp_out, cp_out, sp_out, cs_out = [], [], [], [], []
    s_sample = nk_all = nv_all = None
    for layer in range(DEPTH):
        j = layer // 2
        last = layer == DEPTH - 1
        shift, scale, gate = (mod[layer, :, i * d:(i + 1) * d] for i in range(3))
        p3 = lambda a: a[:b].reshape(b, 1, d)
        s_rows = lambda a: jnp.repeat(a[b:], sr, axis=0)
        g_l = norm_g[layer]
        if layer % 2 == 0:
            q, k, v, z = _inproj(xp, g_l, p3(shift), p3(scale), swa_w_in, j, swa_splits, swa_scales,
                                 (BF16, F32, F32, BF16), t, "swa_inproj_p")
            xp = _swa_prompt(swa_sinks[j], q, k, v, z, xp, p3(gate), swa_out[j], final_norm_g, b, t)
            keep = min(WINDOW, t)
            tail = lambda a: a.reshape(b, t, SWA_KV_W)[:, -keep:].reshape(b, keep, SWA_KV_HEADS, SWA_HEAD_DIM)
            kp_out.append(tail(k))
            vp_out.append(tail(v))

            q, k, v, z = _inproj(xs, g_l, s_rows(shift), s_rows(scale), swa_w_in, j, swa_splits, swa_scales,
                                 (F32,) * 4, 1, "swa_inproj_s")
            a, nk_all, nv_all = _swa_sample(swa_sinks[j], q, k, v, z, ck_all, cv_all, nk_all, nv_all, j, dt)
            xs = _outproj(a, swa_out[j], xs, s_rows(gate), final_norm_g, False, "swa_outproj_s")
        else:
            cw, alog, dtb, ong = gdn_conv_w[j], gdn_a_log[j], gdn_dt_bias[j], gdn_o_norm_g[j]
            qkv, z, ab, tail = _gdn_front(xp, g_l, p3(shift), p3(scale), gdn_in_t, j, cw, t)
            s0 = jnp.zeros((b, GDN_V_HEADS, GDN_HEAD_DIM, GDN_HEAD_DIM), F32)
            seq = lambda a: a.reshape(b, t, a.shape[-1])
            xp, s_new = _gdn(seq(qkv), seq(z), seq(ab), seq(xp), p3(gate), s0, alog, dtb, ong, gdn_out[j],
                             final_norm_g, last)
            xp = xp.reshape(b * t, d)
            cp_out.append(tail[:, SUBLANES - (GDN_CONV - 1):])
            sp_out.append(s_new)

            xqkv, z, ab = _inproj(xs, g_l, s_rows(shift), s_rows(scale), gdn_in_t, j, gdn_splits, gdn_scales,
                                  (F32,) * 3, 1, "gdn_inproj_s")
            a, s_sample = _gdn_sample(xqkv, z, ab, state_gdn_conv[j], state_gdn_s, s_sample, j, cw, alog, dtb, ong, dt)
            xs = _outproj(a, gdn_out[j], xs, s_rows(gate), final_norm_g, last, "gdn_outproj_s")
            cs_out.append(xqkv.reshape(db, sr, GDN_CONV_CH)[:, dt - (GDN_CONV - 1):dt])

    y_prompt = xp.reshape(b, t, d)
    y_sample = xs.reshape(db, sr, d)[:, :dt]
    return (y_prompt, y_sample, jnp.stack(kp_out), jnp.stack(vp_out), position_major(nk_all), position_major(nv_all),
            jnp.stack(cp_out), jnp.stack(sp_out), jnp.stack(cs_out), s_sample)
```

```python
import functools
import itertools

import jax
import jax.numpy as jnp
from jax import lax
from jax.experimental import pallas as pl
from jax.experimental.pallas import tpu as pltpu

F32 = jnp.float32
BF16 = jnp.bfloat16

D_MODEL = 1024
DEPTH = 4
EPS = 1e-6
SWA_HEADS = 16
SWA_KV_HEADS = 4
SWA_HEAD_DIM = 64
SWA_GROUP = SWA_HEADS // SWA_KV_HEADS
WINDOW = 128
SWA_Q_W = SWA_HEADS * SWA_HEAD_DIM
SWA_KV_W = SWA_KV_HEADS * SWA_HEAD_DIM
SWA_SCALE = SWA_HEAD_DIM ** -0.5
GDN_QK_HEADS = 8
GDN_V_HEADS = 16
GDN_HEAD_DIM = 128
GDN_K_W = GDN_QK_HEADS * GDN_HEAD_DIM
GDN_V_W = GDN_V_HEADS * GDN_HEAD_DIM
GDN_CONV = 4
GDN_CONV_CH = 2 * GDN_K_W + GDN_V_W
GDN_CHUNK = 64
LANES = 128
SUBLANES = 8
HALF = LANES // 2
GDN_TILE = 2 * GDN_CHUNK
SAMPLE_ROWS = SUBLANES
VMEM_LIMIT = 56 * 1024 * 1024
PROMPT_ROWS = 512
SAMPLE_TILE_ROWS = 256
PROJ_COLS = 512
SWA_SAMPLE_SEQS = 16
GDN_SAMPLE_SEQS = 8

_NT = (((1,), (1,)), ((), ()))


def _dot(a, b):
    return jnp.dot(a.astype(BF16), b.astype(BF16), preferred_element_type=F32)


def _dot_nt(a, b):
    return lax.dot_general(a.astype(BF16), b.astype(BF16), _NT, preferred_element_type=F32)


def _dot_f32(a, b):
    return jnp.dot(a, b, preferred_element_type=F32, precision=lax.Precision.HIGHEST)


def _silu(x):
    return x * jax.nn.sigmoid(x)


def _iota(shape, dim):
    return lax.broadcasted_iota(jnp.int32, shape, dim)


def _rms(x, g):
    return x * lax.rsqrt(jnp.mean(x * x, axis=-1, keepdims=True) + EPS) * g


def _params(*sem):
    return pltpu.CompilerParams(dimension_semantics=sem, vmem_limit_bytes=VMEM_LIMIT)


def _mod_kernel(c_ref, w_ref, b_ref, o_ref):
    o_ref[...] = _dot(_silu(c_ref[...]), w_ref[...]) + b_ref[...]


def _modulation(c_all, w_mod, b_mod):
    r, d = c_all.shape
    n = w_mod.shape[-1]
    tn = PROJ_COLS
    return pl.pallas_call(
        _mod_kernel,
        grid=(DEPTH, n // tn),
        in_specs=[pl.BlockSpec((r, d), lambda l, j: (0, 0)),
                  pl.BlockSpec((None, d, tn), lambda l, j: (l, 0, j)),
                  pl.BlockSpec((None, 1, tn), lambda l, j: (l, 0, j))],
        out_specs=pl.BlockSpec((None, r, tn), lambda l, j: (l, 0, j)),
        out_shape=jax.ShapeDtypeStruct((DEPTH, r, n), F32),
        compiler_params=_params("arbitrary", "arbitrary"),
        name="adaln_modulation",
    )(c_all, w_mod, b_mod.reshape(DEPTH, 1, n))


def _per_row(ref, rows):
    a = ref[...]
    if a.ndim == 2:
        return a
    g, _, d = a.shape
    return jnp.broadcast_to(a, (g, rows // g, d)).reshape(rows, d)


def _inproj_kernel(x_ref, g_ref, sh_ref, sc_ref, w_ref, *o_refs, splits, scales, w_rows_out):
    tm = x_ref.shape[0]
    h = _rms(x_ref[...], g_ref[...]) * (1.0 + _per_row(sc_ref, tm)) + _per_row(sh_ref, tm)
    hb = h.astype(BF16)
    for o_ref, (c0, n), s in zip(o_refs, splits, scales):
        for cc in range(0, n, PROJ_COLS):
            w = min(PROJ_COLS, n - cc)
            if w_rows_out:
                r = _dot_nt(hb, w_ref[c0 + cc:c0 + cc + w, :])
            else:
                r = jnp.dot(hb, w_ref[:, c0 + cc:c0 + cc + w].astype(BF16), preferred_element_type=F32)
            o_ref[:, cc:cc + w] = (r * s if s != 1.0 else r).astype(o_ref.dtype)
        if n < o_ref.shape[1]:
            o_ref[:, n:] = jnp.zeros((o_ref.shape[0], o_ref.shape[1] - n), o_ref.dtype)


def _inproj(x, g, shift, scale, w_all, j, splits, scales, dtypes, rows_per_mod, name):
    m, d = x.shape
    assert w_all.shape[1] != w_all.shape[2] and d in w_all.shape[1:]
    pad = lambda n: -(-n // LANES) * LANES
    tm = min(PROMPT_ROWS, rows_per_mod) if rows_per_mod >= PROMPT_ROWS else min(SAMPLE_TILE_ROWS, m)
    if rows_per_mod >= tm:
        per = rows_per_mod // tm
        mod_spec = pl.BlockSpec((None, 1, d), lambda i: (i // per, 0, 0))
    else:
        mod_spec = pl.BlockSpec((tm // rows_per_mod, 1, d), lambda i: (i, 0, 0))
    return pl.pallas_call(
        functools.partial(_inproj_kernel, splits=splits, scales=scales, w_rows_out=w_all.shape[2] == d),
        grid=(m // tm,),
        in_specs=[pl.BlockSpec((tm, d), lambda i: (i, 0)),
                  pl.BlockSpec((1, d), lambda i: (0, 0)),
                  mod_spec, mod_spec,
                  pl.BlockSpec((None,) + w_all.shape[1:], lambda i: (j, 0, 0), pipeline_mode=pl.Buffered(1))],
        out_specs=[pl.BlockSpec((tm, pad(n)), lambda i: (i, 0)) for _, n in splits],
        out_shape=[jax.ShapeDtypeStruct((m, pad(n)), dtype) for (_, n), dtype in zip(splits, dtypes)],
        compiler_params=_params("arbitrary"),
        name=name,
    )(x, g.reshape(1, d), shift, scale, w_all)


def _gdn_front_kernel(x_ref, g_ref, sh_ref, sc_ref, w_ref, cw_ref, qkv_ref, z_ref, ab_ref, tail_ref, carry, rbuf,
                      *, tiles_per_seq):
    tm = x_ref.shape[0]
    head = GDN_CONV - 1
    base = SUBLANES - head
    width = 4 * LANES

    @pl.when(pl.program_id(0) % tiles_per_seq == 0)
    def _():
        carry[...] = jnp.zeros(carry.shape, F32)

    h = _rms(x_ref[...], g_ref[...]) * (1.0 + sc_ref[...]) + sh_ref[...]
    hb = h.astype(BF16)
    proj = lambda c0: _dot_nt(hb, w_ref[c0:c0 + width, :])
    n_qkv, n_z = GDN_CONV_CH // width, GDN_V_W // width
    r_next = proj(0)
    for cb in range(n_qkv):
        cols = slice(cb * width, (cb + 1) * width)
        r = r_next
        if cb + 1 < n_qkv:
            r_next = proj((cb + 1) * width)
        if cb % 2 == 1:
            zc = cb // 2
            z_ref[:, zc * width:(zc + 1) * width] = proj(GDN_CONV_CH + zc * width).astype(z_ref.dtype)
        rbuf[0:SUBLANES, :] = carry[:, cols]
        rbuf[SUBLANES:SUBLANES + tm, :] = r
        y = rbuf[base:base + tm, :] * cw_ref[0:1, cols]
        for w in range(1, GDN_CONV):
            y = y + rbuf[base + w:base + w + tm, :] * cw_ref[w:w + 1, cols]
        carry[:, cols] = r[tm - SUBLANES:tm]
        tail_ref[:, cols] = r[tm - SUBLANES:tm]
        y = _silu(y)
        if cb * width < 2 * GDN_K_W:
            parts = []
            for p in range(width // GDN_HEAD_DIM):
                blk = y[:, p * GDN_HEAD_DIM:(p + 1) * GDN_HEAD_DIM]
                inv = lax.rsqrt(jnp.sum(blk * blk, axis=-1, keepdims=True) + EPS)
                parts.append(blk * (inv * GDN_HEAD_DIM ** -0.5 if cb * width < GDN_K_W else inv))
            y = jnp.concatenate(parts, axis=1)
        qkv_ref[:, cols] = y.astype(qkv_ref.dtype)
    assert n_z == n_qkv // 2
    nab = 2 * GDN_V_HEADS
    ab_ref[:, 0:nab] = _dot_nt(hb, w_ref[GDN_CONV_CH + GDN_V_W:GDN_CONV_CH + GDN_V_W + nab, :])
    ab_ref[:, nab:] = jnp.zeros((tm, LANES - nab), F32)


def _gdn_front(x, g, shift, scale, w_all_t, j, cw, t):
    m, d = x.shape
    tm = min(PROMPT_ROWS, t)
    per = t // tm
    mod_spec = pl.BlockSpec((None, 1, d), lambda i: (i // per, 0, 0))
    row = lambda w: pl.BlockSpec((tm, w), lambda i: (i, 0))
    const = lambda i: (0, 0)
    return pl.pallas_call(
        functools.partial(_gdn_front_kernel, tiles_per_seq=per),
        grid=(m // tm,),
        in_specs=[row(d), pl.BlockSpec((1, d), const), mod_spec, mod_spec,
                  pl.BlockSpec((None,) + w_all_t.shape[1:], lambda i: (j, 0, 0), pipeline_mode=pl.Buffered(1)),
                  pl.BlockSpec(cw.shape, const)],
        out_specs=[row(GDN_CONV_CH), row(GDN_V_W), row(LANES),
                   pl.BlockSpec((None, SUBLANES, GDN_CONV_CH), lambda i: (i // per, 0, 0))],
        out_shape=[jax.ShapeDtypeStruct((m, GDN_CONV_CH), BF16), jax.ShapeDtypeStruct((m, GDN_V_W), BF16),
                   jax.ShapeDtypeStruct((m, LANES), F32), jax.ShapeDtypeStruct((m // t, SUBLANES, GDN_CONV_CH), F32)],
        scratch_shapes=[pltpu.VMEM((SUBLANES, GDN_CONV_CH), F32), pltpu.VMEM((SUBLANES + tm, 4 * LANES), F32)],
        compiler_params=_params("arbitrary"),
        name="gdn_front_p",
    )(x, g.reshape(1, d), shift, scale, w_all_t, cw)


def _finish(a, w_ref, x, gate, fg_ref, final_norm):
    y = x + gate * _dot(a, w_ref[...])
    return _rms(y, fg_ref[...]) if final_norm else y


def _outproj_kernel(a_ref, w_ref, x_ref, gate_ref, fg_ref, o_ref, *, final_norm):
    gate = _per_row(gate_ref, x_ref.shape[0])
    o_ref[...] = _finish(a_ref[...], w_ref, x_ref[...], gate, fg_ref, final_norm)


def _outproj(a, w, x, gate, fg, final_norm, name):
    m, k = a.shape
    d = x.shape[-1]
    tm = min(PROMPT_ROWS, m)
    return pl.pallas_call(
        functools.partial(_outproj_kernel, final_norm=final_norm),
        grid=(m // tm,),
        in_specs=[pl.BlockSpec((tm, k), lambda i: (i, 0)),
                  pl.BlockSpec((k, d), lambda i: (0, 0)),
                  pl.BlockSpec((tm, d), lambda i: (i, 0)),
                  pl.BlockSpec((tm // SAMPLE_ROWS, 1, d), lambda i: (i, 0, 0)),
                  pl.BlockSpec((1, d), lambda i: (0, 0))],
        out_specs=pl.BlockSpec((tm, d), lambda i: (i, 0)),
        out_shape=jax.ShapeDtypeStruct((m, d), F32),
        compiler_params=_params("arbitrary"),
        name=name,
    )(a, w, x, gate, fg.reshape(1, d))


def _swa_block(q, kcat, vcat, mask4, sinks_ref):
    nq = q.shape[0]
    lo = _iota((1, LANES), 1) < HALF
    hi = jnp.logical_not(lo)
    row = _iota((SWA_GROUP * nq, 1), 0)
    kvs = range(SWA_KV_HEADS)
    blk = lambda a, pb: a[:, pb * LANES:(pb + 1) * LANES]
    q4s = []
    for kh in kvs:
        parts = []
        for g in range(SWA_GROUP):
            qg = blk(q, 2 * kh + g // 2)
            if g % 2 != kh % 2:
                qg = pltpu.roll(qg, HALF, axis=1)
            parts.append(jnp.where(lo if kh % 2 == 0 else hi, qg, 0.0))
        q4s.append(jnp.concatenate(parts, axis=0))
    logits = [_dot_nt(q4, blk(kcat, kh // 2)) for kh, q4 in zip(kvs, q4s)]
    es, invs = [], []
    for kh, lg in zip(kvs, logits):
        sink = jnp.full((SWA_GROUP * nq, 1), sinks_ref[SWA_GROUP * kh], F32)
        for g in range(1, SWA_GROUP):
            sink = jnp.where(row >= g * nq, sinks_ref[SWA_GROUP * kh + g], sink)
        lg = jnp.where(mask4, lg, -jnp.inf)
        m = jnp.maximum(jnp.max(lg, axis=-1, keepdims=True), sink)
        e = jnp.exp(lg - m)
        es.append(e)
        invs.append(1.0 / (jnp.sum(e, axis=-1, keepdims=True) + jnp.exp(sink - m)))
    o4s = [_dot(e, blk(vcat, kh // 2)) * inv for kh, e, inv in zip(kvs, es, invs)]
    outs = [None] * (SWA_HEADS // 2)
    for kh, o4 in zip(kvs, o4s):
        for g in range(SWA_GROUP):
            piece = o4[g * nq:(g + 1) * nq]
            if g % 2 != kh % 2:
                piece = pltpu.roll(piece, HALF, axis=1)
            piece = jnp.where(lo if g % 2 == 0 else hi, piece, 0.0)
            qb = 2 * kh + g // 2
            outs[qb] = piece if outs[qb] is None else outs[qb] + piece
    return jnp.concatenate(outs, axis=1)


def _band_mask(nq, nk, jmin):
    i = _iota((SWA_GROUP * nq, nk), 0) % nq
    j = _iota((SWA_GROUP * nq, nk), 1)
    return (j >= i) & (j <= i + WINDOW) & (j >= jmin)


def _swa_prompt_kernel(sinks_ref, q_ref, kp_ref, vp_ref, k_ref, v_ref, z_ref, x_ref, gate_ref, w_ref,
                       o_ref, kbuf, vbuf, abuf, *, tq):
    t = pl.program_id(1)
    kbuf[0:WINDOW] = kp_ref[...]
    kbuf[WINDOW:WINDOW + tq] = k_ref[...]
    vbuf[0:WINDOW] = vp_ref[...]
    vbuf[WINDOW:WINDOW + tq] = v_ref[...]

    def body(sb, carry):
        r = pl.multiple_of(sb * WINDOW, WINDOW)
        jmin = jnp.where((t == 0) & (sb == 0), WINDOW, 0)
        mask4 = _band_mask(WINDOW, 2 * WINDOW, jmin)
        abuf[pl.ds(r, WINDOW), :] = _swa_block(q_ref[pl.ds(r, WINDOW), :].astype(F32), kbuf[pl.ds(r, 2 * WINDOW), :],
                                               vbuf[pl.ds(r, 2 * WINDOW), :], mask4, sinks_ref)
        return carry

    lax.fori_loop(0, tq // WINDOW, body, 0)
    a = abuf[...] * _silu(z_ref[...].astype(F32))
    o_ref[...] = x_ref[...] + gate_ref[...] * _dot(a, w_ref[...])


def _swa_prompt(sinks, q, k, v, z, x, gate, w_out, b, t):
    d = x.shape[-1]
    tq = min(PROMPT_ROWS, t)
    nt = t // tq
    sub = tq // WINDOW
    row = lambda bi, ti: (bi * nt + ti, 0)
    prev = lambda bi, ti: (bi * nt * sub + jnp.maximum(ti * sub - 1, 0), 0)
    const = lambda bi, ti: (0, 0)
    return pl.pallas_call(
        functools.partial(_swa_prompt_kernel, tq=tq),
        grid=(b, nt),
        in_specs=[pl.BlockSpec(memory_space=pltpu.SMEM),
                  pl.BlockSpec((tq, SWA_Q_W), row),
                  pl.BlockSpec((WINDOW, SWA_KV_W), prev),
                  pl.BlockSpec((WINDOW, SWA_KV_W), prev),
                  pl.BlockSpec((tq, SWA_KV_W), row),
                  pl.BlockSpec((tq, SWA_KV_W), row),
                  pl.BlockSpec((tq, SWA_Q_W), row),
                  pl.BlockSpec((tq, d), row),
                  pl.BlockSpec((None, 1, d), lambda bi, ti: (bi, 0, 0)),
                  pl.BlockSpec(w_out.shape, const)],
        out_specs=pl.BlockSpec((tq, d), row),
        out_shape=jax.ShapeDtypeStruct(x.shape, F32),
        scratch_shapes=[pltpu.VMEM((WINDOW + tq, SWA_KV_W), F32),
                        pltpu.VMEM((WINDOW + tq, SWA_KV_W), F32),
                        pltpu.VMEM((tq, SWA_Q_W), F32)],
        compiler_params=_params("arbitrary", "arbitrary"),
        name="swa_prompt",
    )(sinks, q, k, v, k, v, z, x, gate, w_out)


def _swa_sample_kernel(sinks_ref, q_ref, k_ref, v_ref, z_ref, ck_ref, cv_ref, *rest, bb, dt, has_prev):
    a_ref, nk_ref, nv_ref = rest[2:] if has_prev else rest
    nq, grp = SAMPLE_ROWS, SWA_GROUP
    lane = _iota((1, LANES), 1)
    lo = lane < HALF
    hi = jnp.logical_not(lo)
    row = _iota((grp * nq, 1), 0)
    qi = row % nq
    see_cache, see_own = lane >= qi, lane <= qi
    zpad = jnp.zeros((WINDOW - nq, LANES), F32)
    rows = lambda bi: slice(bi * nq, (bi + 1) * nq)
    blk = lambda pb: slice(pb * LANES, (pb + 1) * LANES)
    units = [(bi, kh) for bi in range(bb) for kh in range(SWA_KV_HEADS)]

    q4s = []
    for bi, kh in units:
        keep = lo if kh % 2 == 0 else hi
        parts = []
        for g in range(grp):
            qg = q_ref[rows(bi), blk(2 * kh + g // 2)]
            if g % 2 != kh % 2:
                qg = pltpu.roll(qg, HALF, axis=1)
            parts.append(jnp.where(keep, qg, 0.0))
        q4s.append(jnp.concatenate(parts, axis=0))
    own = lambda ref, bi, pb: jnp.concatenate([ref[rows(bi), blk(pb)], zpad], axis=0)
    lcs = [_dot(q4, ck_ref[bi, blk(kh // 2), :]) for (bi, kh), q4 in zip(units, q4s)]
    los = [_dot_nt(q4, own(k_ref, bi, kh // 2)) for (bi, kh), q4 in zip(units, q4s)]
    ecs, eos, invs = [], [], []
    for (bi, kh), lc, lw in zip(units, lcs, los):
        sink = jnp.full((grp * nq, 1), sinks_ref[grp * kh], F32)
        for g in range(1, grp):
            sink = jnp.where(row >= g * nq, sinks_ref[grp * kh + g], sink)
        lc = jnp.where(see_cache, lc, -jnp.inf)
        lw = jnp.where(see_own, lw, -jnp.inf)
        m = jnp.maximum(jnp.maximum(jnp.max(lc, axis=-1, keepdims=True), jnp.max(lw, axis=-1, keepdims=True)), sink)
        ec, eo = jnp.exp(lc - m), jnp.exp(lw - m)
        den = jnp.sum(ec, axis=-1, keepdims=True) + jnp.sum(eo, axis=-1, keepdims=True) + jnp.exp(sink - m)
        ecs.append(ec)
        eos.append(eo)
        invs.append(1.0 / den)
    o4s = [(_dot_nt(ec, cv_ref[bi, blk(kh // 2), :]) + _dot(eo, own(v_ref, bi, kh // 2))) * inv
           for (bi, kh), ec, eo, inv in zip(units, ecs, eos, invs)]
    for bi in range(bb):
        outs = [None] * (SWA_HEADS // 2)
        for kh in range(SWA_KV_HEADS):
            o4 = o4s[bi * SWA_KV_HEADS + kh]
            for g in range(grp):
                piece = o4[g * nq:(g + 1) * nq]
                if g % 2 != kh % 2:
                    piece = pltpu.roll(piece, HALF, axis=1)
                piece = jnp.where(lo if g % 2 == 0 else hi, piece, 0.0)
                qb = 2 * kh + g // 2
                outs[qb] = piece if outs[qb] is None else outs[qb] + piece
        a_ref[rows(bi), :] = jnp.concatenate(outs, axis=1) * _silu(z_ref[rows(bi), :])

    fresh = lane >= WINDOW - dt
    for src, cache, dst in ((k_ref, ck_ref, nk_ref), (v_ref, cv_ref, nv_ref)):
        for pb in range(SWA_KV_W // LANES):
            new = src[:, blk(pb)]
            if bb * nq < LANES:
                new = jnp.concatenate([new, jnp.zeros((LANES - bb * nq, LANES), F32)], axis=0)
            new_t = new.T
            for bi in range(bb):
                cols = pltpu.roll(new_t, (WINDOW - dt - bi * nq) % LANES, axis=1)
                dst[bi, blk(pb), :] = jnp.where(fresh, cols, pltpu.roll(cache[bi, blk(pb), :], WINDOW - dt, axis=1))


def _swa_sample(sinks, q, k, v, z, ck_all, cv_all, nk_prev, nv_prev, j, dt):
    db = ck_all.shape[1]
    bb = min(SWA_SAMPLE_SEQS, db)
    nq = SAMPLE_ROWS
    assert 2 * dt <= nq
    row = lambda i: (i, 0)
    c_spec = pl.BlockSpec((None, bb, SWA_KV_W, WINDOW), lambda i: (j, i, 0, 0))
    args = [sinks, q, k, v, z, ck_all, cv_all]
    in_specs = [pl.BlockSpec(memory_space=pltpu.SMEM),
                pl.BlockSpec((bb * nq, SWA_Q_W), row),
                pl.BlockSpec((bb * nq, SWA_KV_W), row),
                pl.BlockSpec((bb * nq, SWA_KV_W), row),
                pl.BlockSpec((bb * nq, SWA_Q_W), row),
                c_spec, c_spec]
    aliases = {}
    if nk_prev is not None:
        args += [nk_prev, nv_prev]
        in_specs += [pl.BlockSpec(memory_space=pl.ANY)] * 2
        aliases = {len(args) - 2: 1, len(args) - 1: 2}
    return pl.pallas_call(
        functools.partial(_swa_sample_kernel, bb=bb, dt=dt, has_prev=nk_prev is not None),
        grid=(db // bb,),
        in_specs=in_specs,
        out_specs=[pl.BlockSpec((bb * nq, SWA_Q_W), row), c_spec, c_spec],
        out_shape=[jax.ShapeDtypeStruct(q.shape, F32),
                   jax.ShapeDtypeStruct(ck_all.shape, F32),
                   jax.ShapeDtypeStruct(cv_all.shape, F32)],
        input_output_aliases=aliases,
        compiler_params=_params("arbitrary"),
        name="swa_sample",
    )(*args)


def _block_mul(a, b):
    size, width = b.shape
    b16 = b.astype(BF16)
    zero = jnp.zeros_like(b16)
    blk = _iota((1, width), 1) // size
    return _dot(a, jnp.concatenate([jnp.where(blk == r, b16, zero) for r in range(width // size)], axis=0))


def _diag2(a, b):
    return jnp.concatenate([jnp.concatenate([a, jnp.zeros_like(b)], axis=1),
                            jnp.concatenate([jnp.zeros_like(a), b], axis=1)], axis=0)


def _gdn_kernel(qkv, z_ref, ab_ref, x_ref, gate_ref, s0_ref, alog_ref, dtb_ref, ong_ref, w_ref, fg_ref, o_ref, s_ref,
                obuf, *, nb, final_norm):
    nchunks = GDN_TILE // GDN_CHUNK
    R, C, HD = GDN_TILE, GDN_CHUNK, GDN_HEAD_DIM
    t = pl.program_id(1)
    seqs = range(nb)

    @pl.when(t == 0)
    def _():
        s_ref[...] = s0_ref[...]

    ri, ci = _iota((R, R), 0), _iota((R, R), 1)
    same = (ri // C) == (ci // C)
    tri, ones = jnp.where(same & (ci <= ri), 1.0, 0.0), jnp.where(same, 1.0, 0.0)
    gcs, betas, egcs, bgcs, gcTs, glTs, dlTs = ([None] * nb for _ in range(7))

    def prologue(sq):
        ab = ab_ref[sq]
        g = -jnp.exp(alog_ref[...]) * jax.nn.softplus(ab + dtb_ref[...])
        beta = pltpu.roll(jax.nn.sigmoid(ab), LANES - GDN_V_HEADS, axis=1)
        gc = _dot_f32(tri, g)
        gl = _dot_f32(ones, g)
        egc = jnp.exp(gc)
        gcT, glT = gc.T, gl.T
        gcs[sq], betas[sq], egcs[sq], bgcs[sq] = gc, beta, egc, beta * egc
        gcTs[sq], glTs[sq], dlTs[sq] = gcT, glT, glT - gcT
        yield

    lane = _iota((1, LANES), 1)
    lo = lane < HALF
    jj = lane % C
    ii = _iota((C, 1), 0)
    incl, strict = ii >= jj, ii > jj
    zc = jnp.zeros((C, HD), F32)

    def col2(arr, r0, h0):
        return jnp.where(lo, arr[r0:r0 + C, h0:h0 + 1], arr[r0:r0 + C, h0 + 1:h0 + 2])

    def pick(c, row_a, row_b):
        if c == 0:
            return jnp.where(lo, row_a, pltpu.roll(row_b, HALF, axis=1))
        return jnp.where(lo, pltpu.roll(row_a, HALF, axis=1), row_b)

    pairs = range(GDN_QK_HEADS)
    heads = range(GDN_V_HEADS)
    units = [(c, p) for c in range(nchunks) for p in pairs]
    quads = [(c, pp) for c in range(nchunks) for pp in range(GDN_QK_HEADS // 2)]
    qsl = lambda sq, c, p: qkv[sq, c * C:(c + 1) * C, p * HD:(p + 1) * HD]
    ksl = lambda sq, c, p: qkv[sq, c * C:(c + 1) * C, GDN_K_W + p * HD:GDN_K_W + (p + 1) * HD]
    vsl = lambda sq, c, h: qkv[sq, c * C:(c + 1) * C, 2 * GDN_K_W + h * HD:2 * GDN_K_W + (h + 1) * HD]
    col = lambda arr, c, h: arr[c * C:(c + 1) * C, h:h + 1]
    jq = _iota((1, 2 * LANES), 1) % C
    kts, lows, qks, sols = ([None] * nb for _ in range(4))

    def products(sq):
        kts[sq] = [qkv[sq, :, GDN_K_W + p * HD:GDN_K_W + (p + 1) * HD].astype(F32).T for p in pairs]
        yield
        ms = [_dot_nt(jnp.concatenate([ksl(sq, c, p), qsl(sq, c, p)], axis=0), jnp.concatenate([ksl(sq, c, p)] * 2, axis=0))
              for c, p in units]
        yield
        lows[sq], qks[sq] = [], []
        for (c, p), m in zip(units, ms):
            h0 = 2 * p
            row_gc = pick(c, gcTs[sq][h0:h0 + 1, :], gcTs[sq][h0 + 1:h0 + 2, :])
            decay = jnp.exp(jnp.where(incl, col2(gcs[sq], c * C, h0) - row_gc, -jnp.inf))
            lows[sq].append(jnp.where(strict, m[0:C] * decay, 0.0) * col2(betas[sq], c * C, h0))
            qks[sq].append(m[C:2 * C] * decay)
            if p % 4 == 3:
                yield

    def inverse(sq):
        low4 = [jnp.concatenate([lows[sq][c * GDN_QK_HEADS + 2 * pp], lows[sq][c * GDN_QK_HEADS + 2 * pp + 1]], axis=1)
                for c, pp in quads]
        invs = [jnp.where(ii == jq, 1.0, 0.0) - jnp.where((ii // 2) == (jq // 2), low, 0.0) for low in low4]
        yield
        s = 2
        while s < C:
            below = ((ii // s) % 2 == 1) & ((jq // s) == (ii // s) - 1)
            eps, new = [], []
            for i, (low, inv) in enumerate(zip(low4, invs)):
                eps.append(_block_mul(jnp.where(below, low, 0.0), inv))
                if i % 2:
                    yield
            for i, (inv, ep) in enumerate(zip(invs, eps)):
                new.append(inv - _block_mul(inv, ep))
                if i % 2:
                    yield
            invs = new
            s *= 2
        sols[sq] = []
        for (c, pp), inv in zip(quads, invs):
            rhs = []
            for idx in range(4):
                h = 4 * pp + idx
                piece = [vsl(sq, c, h) * col(betas[sq], c, h), ksl(sq, c, h // 2) * col(bgcs[sq], c, h)]
                rhs.append(jnp.concatenate([zc] * (2 * idx) + piece + [zc] * (6 - 2 * idx), axis=1))
            sols[sq].append(_dot(inv, jnp.concatenate(rhs, axis=0)))
            if pp % 2 == 1:
                yield

    def state(sq):
        for c in range(nchunks):
            in_chunk = (lane // C) == c
            quad = lambda h: sols[sq][c * (GDN_QK_HEADS // 2) + h // 4]
            w_v = lambda h: quad(h)[:, 2 * (h % 4) * HD:(2 * (h % 4) + 1) * HD]
            w_k = lambda h: quad(h)[:, (2 * (h % 4) + 1) * HD:(2 * (h % 4) + 2) * HD]
            a_s = []
            for p in pairs:
                top = jnp.concatenate([w_k(2 * p), w_k(2 * p + 1)], axis=1)
                bot = jnp.concatenate([qsl(sq, c, p) * col(egcs[sq], c, 2 * p), qsl(sq, c, p) * col(egcs[sq], c, 2 * p + 1)],
                                      axis=1)
                a_s.append(_dot(jnp.concatenate([top, bot], axis=0), _diag2(s_ref[sq, 2 * p], s_ref[sq, 2 * p + 1])))
            yield
            us = [[w_v(2 * p + idx) - a[0:C, idx * HD:(idx + 1) * HD] for idx in range(2)] for p, a in zip(pairs, a_s)]
            for p, a, u in zip(pairs, a_s, us):
                obuf[sq, c * C:(c + 1) * C, 2 * p * HD:(2 * p + 2) * HD] = (
                    a[C:2 * C] + _dot(qks[sq][c * GDN_QK_HEADS + p], _diag2(*u)))
            yield
            for p, u in zip(pairs, us):
                kd, ue, gt = [], [], []
                for idx in range(2):
                    h = 2 * p + idx
                    kd.append(kts[sq][p] * jnp.exp(jnp.where(in_chunk, dlTs[sq][h:h + 1, :], -jnp.inf)))
                    ue.append(jnp.concatenate([u[idx], zc] if c == 0 else [zc, u[idx]], axis=0))
                    gt.append(jnp.exp(pick(c, glTs[sq][h:h + 1, :], glTs[sq][h:h + 1, :])))
                upd = _dot(jnp.concatenate(kd, axis=1), _diag2(*ue))
                for idx in range(2):
                    h = 2 * p + idx
                    s_ref[sq, h] = s_ref[sq, h] * gt[idx] + upd[:, idx * HD:(idx + 1) * HD]
                if p % 4 == 3:
                    yield

    def run(*gens):
        gens = list(gens)
        while gens:
            for gen in list(gens):
                if next(gen, "done") == "done":
                    gens.remove(gen)

    for k in range(nb + 2):
        live = [itertools.chain(prologue(k), products(k))] if k < nb else []
        live += [inverse(k - 1)] if 0 <= k - 1 < nb else []
        live += [state(k - 2)] if 0 <= k - 2 < nb else []
        run(*live)

    rows = []
    for sq in seqs:
        o = obuf[sq]
        parts = [_rms(o[:, h * HD:(h + 1) * HD], ong_ref[...]) for h in heads]
        rows.append(jnp.concatenate(parts, axis=1) * _silu(z_ref[sq].astype(F32)))
    out = _dot(jnp.concatenate(rows, axis=0), w_ref[...])
    for sq in seqs:
        y = x_ref[sq] + gate_ref[sq] * out[sq * R:(sq + 1) * R]
        o_ref[sq] = _rms(y, fg_ref[...]) if final_norm else y


def _gdn(qkv, z, ab, x, gate, s0, alog, dtb, ong, w_out, fg, final_norm):
    b, t, d = x.shape
    nb = 2 if b % 2 == 0 else 1
    rows_in = GDN_TILE
    tile = lambda w: pl.BlockSpec((nb, rows_in, w), lambda bi, ti: (bi, ti, 0))
    const = lambda bi, ti: (0, 0)
    per_b3 = lambda bi, ti: (bi, 0, 0)
    per_b4 = lambda bi, ti: (bi, 0, 0, 0)
    vec = lambda a: jnp.pad(a.astype(F32), (0, LANES - a.shape[0])).reshape(1, LANES)
    state = (GDN_V_HEADS, GDN_HEAD_DIM, GDN_HEAD_DIM)
    return pl.pallas_call(
        functools.partial(_gdn_kernel, nb=nb, final_norm=final_norm),
        grid=(b // nb, t // rows_in),
        in_specs=[tile(GDN_CONV_CH), tile(GDN_V_W), tile(LANES), tile(d),
                  pl.BlockSpec((nb, 1, d), per_b3),
                  pl.BlockSpec((nb,) + state, per_b4),
                  pl.BlockSpec((1, LANES), const),
                  pl.BlockSpec((1, LANES), const),
                  pl.BlockSpec((1, GDN_HEAD_DIM), const),
                  pl.BlockSpec(w_out.shape, const),
                  pl.BlockSpec((1, d), const)],
        out_specs=[tile(d), pl.BlockSpec((nb,) + state, per_b4)],
        out_shape=[jax.ShapeDtypeStruct(x.shape, F32),
                   jax.ShapeDtypeStruct((b,) + state, F32)],
        scratch_shapes=[pltpu.VMEM((nb, GDN_TILE, GDN_V_W), F32)],
        compiler_params=_params("arbitrary", "arbitrary"),
        name="gdn_prompt",
    )(qkv, z, ab, x, gate, s0, vec(alog), vec(dtb), ong.reshape(1, GDN_HEAD_DIM), w_out, fg.reshape(1, d))


def _tile_roll(a, d):
    r, w = a.shape
    return pltpu.roll(a.reshape(r // SUBLANES, SUBLANES, w), d, axis=1).reshape(r, w)


def _gdn_sample_kernel(xqkv_ref, z_ref, ab_ref, conv_ref, s_ref, cw_ref, alog_ref, dtb_ref, ong_ref, *rest,
                       bb, dt, has_prev):
    a_ref, so_ref, st8, qkv = rest[1:] if has_prev else rest
    R, HD = bb * SUBLANES, GDN_HEAD_DIM
    tix = _iota((R, 1), 0) % SUBLANES
    live = tix < dt
    head = GDN_CONV - 1
    lane = _iota((1, LANES), 1)

    st8[...] = jnp.zeros(st8.shape, F32)
    for bi in range(bb):
        st8[(bi + 1) * SUBLANES - head:(bi + 1) * SUBLANES, :] = conv_ref[bi]
    for cb in range(GDN_CONV_CH // GDN_K_W):
        cols = slice(cb * GDN_K_W, (cb + 1) * GDN_K_W)
        x, st = xqkv_ref[:, cols], st8[:, cols]
        y = x * cw_ref[head:head + 1, cols]
        for d in range(1, GDN_CONV):
            y = y + jnp.where(tix >= d, _tile_roll(x, d), _tile_roll(st, d)) * cw_ref[head - d:head - d + 1, cols]
        y = _silu(y)
        if cb < 2:
            for p in range(GDN_QK_HEADS):
                blk = y[:, p * HD:(p + 1) * HD]
                inv = lax.rsqrt(jnp.sum(blk * blk, axis=-1, keepdims=True) + EPS)
                if cb == 0:
                    inv = inv * HD ** -0.5
                qkv[:, cb * GDN_K_W + p * HD:cb * GDN_K_W + (p + 1) * HD] = blk * inv
        else:
            qkv[:, cols] = y

    ab = ab_ref[...]
    g = jnp.where(live, -jnp.exp(alog_ref[...]) * jax.nn.softplus(ab + dtb_ref[...]), 0.0)
    beta = jnp.where(live, pltpu.roll(jax.nn.sigmoid(ab), LANES - GDN_V_HEADS, axis=1), 0.0)
    gc, gl = g, g
    for d in range(1, SUBLANES):
        gl = gl + _tile_roll(g, d)
        if d < dt:
            gc = gc + jnp.where(tix >= d, _tile_roll(g, d), 0.0)
    egc = jnp.exp(gc)
    bgc = beta * egc
    ekl = jnp.where(live, jnp.exp(jnp.where(live, gl - gc, 0.0)), 0.0)
    gtot = jnp.exp(gl)
    dec = [None] + [jnp.where(tix >= d, jnp.exp(jnp.where(tix >= d, gc - _tile_roll(gc, d), 0.0)), 0.0)
                    for d in range(1, dt)]

    zero = jnp.zeros((R, LANES), F32)
    kk, qk = [zero] * dt, [zero] * dt
    qsl = lambda p: qkv[:, p * HD:(p + 1) * HD]
    ksl = lambda p: qkv[:, GDN_K_W + p * HD:GDN_K_W + (p + 1) * HD]
    vsl = lambda h: qkv[:, 2 * GDN_K_W + h * HD:2 * GDN_K_W + (h + 1) * HD]
    for p in range(GDN_QK_HEADS):
        mine = (lane // 2) == p
        for d in range(dt):
            ks = ksl(p) if d == 0 else _tile_roll(ksl(p), d)
            qk[d] = jnp.where(mine, jnp.sum(qsl(p) * ks, axis=-1, keepdims=True), qk[d])
            if d:
                kk[d] = jnp.where(mine, jnp.sum(ksl(p) * ks, axis=-1, keepdims=True), kk[d])
    low = [None] + [beta * kk[d] * dec[d] for d in range(1, dt)]
    qkd = [qk[0]] + [qk[d] * dec[d] for d in range(1, dt)]
    l1s1, l1s2 = _tile_roll(low[1], 1), _tile_roll(low[1], 2)
    tinv = [None, -low[1], -low[2] + low[1] * l1s1,
            -low[3] + low[1] * _tile_roll(low[2], 1) + low[2] * l1s2 - low[1] * l1s1 * l1s2]
    heads = range(GDN_V_HEADS)
    spread = lambda arr: [jnp.broadcast_to(arr[:, h:h + 1], (R, LANES)) for h in heads]
    s_beta, s_bgc, s_egc, s_ekl, s_gtot = spread(beta), spread(bgc), spread(egc), spread(ekl), spread(gtot)
    s_tinv = [None] + [spread(tinv[d]) for d in range(1, dt)]
    s_qkd = [spread(qkd[d]) for d in range(dt)]

    xvs, m8s = [], []
    for h in heads:
        p = h // 2
        rv, rk = vsl(h) * s_beta[h], ksl(p) * s_bgc[h]
        xv, xk = rv, rk
        for d in range(1, dt):
            xv = xv + s_tinv[d][h] * _tile_roll(rv, d)
            xk = xk + s_tinv[d][h] * _tile_roll(rk, d)
        xvs.append(xv)
        m8s.append(jnp.where(live, xk, _tile_roll(qsl(p) * s_egc[h], dt)))
    tile = lambda a, bi: a[bi * SUBLANES:(bi + 1) * SUBLANES]
    aws = [jnp.concatenate([_dot(tile(m8s[h], bi), s_ref[bi, h]) for bi in range(bb)], axis=0) for h in heads]
    outs, uds = [], []
    for h in heads:
        u = jnp.where(live, xvs[h] - aws[h], 0.0)
        o = _tile_roll(aws[h], SUBLANES - dt) + s_qkd[0][h] * u
        for d in range(1, dt):
            o = o + s_qkd[d][h] * _tile_roll(u, d)
        outs.append(_rms(o, ong_ref[...]))
        uds.append(u * s_ekl[h])
    a_ref[...] = jnp.concatenate(outs, axis=1) * _silu(z_ref[...])
    tn = (((0,), (0,)), ((), ()))
    for h in heads:
        kp = ksl(h // 2)
        for bi in range(bb):
            upd = lax.dot_general(tile(kp, bi).astype(BF16), tile(uds[h], bi).astype(BF16), tn,
                                  preferred_element_type=F32)
            so_ref[bi, h] = s_ref[bi, h] * s_gtot[h][bi * SUBLANES:bi * SUBLANES + 1] + upd


def _gdn_sample(xqkv, z, ab, conv, s_all, s_prev, j, cw, alog, dtb, ong, dt):
    db = conv.shape[0]
    bb = min(GDN_SAMPLE_SEQS, db)
    r = bb * SUBLANES
    assert 2 * dt <= SUBLANES and dt == GDN_CONV
    row = lambda i: (i, 0)
    const = lambda i: (0, 0)
    state = (GDN_V_HEADS, GDN_HEAD_DIM, GDN_HEAD_DIM)
    s_spec = pl.BlockSpec((None, bb) + state, lambda i: (j, i, 0, 0, 0))
    vec = lambda a: jnp.pad(a.astype(F32), (0, LANES - a.shape[0])).reshape(1, LANES)
    args = [xqkv, z, ab, conv, s_all, cw, vec(alog), vec(dtb), ong.reshape(1, GDN_HEAD_DIM)]
    in_specs = [pl.BlockSpec((r, GDN_CONV_CH), row),
                pl.BlockSpec((r, GDN_V_W), row),
                pl.BlockSpec((r, LANES), row),
                pl.BlockSpec((bb, GDN_CONV - 1, GDN_CONV_CH), lambda i: (i, 0, 0)),
                s_spec,
                pl.BlockSpec(cw.shape, const),
                pl.BlockSpec((1, LANES), const),
                pl.BlockSpec((1, LANES), const),
                pl.BlockSpec((1, GDN_HEAD_DIM), const)]
    aliases = {}
    if s_prev is not None:
        args.append(s_prev)
        in_specs.append(pl.BlockSpec(memory_space=pl.ANY))
        aliases = {len(args) - 1: 1}
    return pl.pallas_call(
        functools.partial(_gdn_sample_kernel, bb=bb, dt=dt, has_prev=s_prev is not None),
        grid=(db // bb,),
        in_specs=in_specs,
        out_specs=[pl.BlockSpec((r, GDN_V_W), row), s_spec],
        out_shape=[jax.ShapeDtypeStruct(z.shape, F32), jax.ShapeDtypeStruct(s_all.shape, F32)],
        scratch_shapes=[pltpu.VMEM((r, GDN_CONV_CH), F32), pltpu.VMEM((r, GDN_CONV_CH), F32)],
        input_output_aliases=aliases,
        compiler_params=_params("arbitrary"),
        name="gdn_sample",
    )(*args)


def kernel(x_prompt, x_sample, cache_swa_k, cache_swa_v, state_gdn_conv, state_gdn_s, c_prompt, c_sample, norm_g,
           w_mod, b_mod, swa_w_in, swa_sinks, swa_w_out, gdn_w_in, gdn_conv_w, gdn_a_log, gdn_dt_bias, gdn_o_norm_g,
           gdn_w_out, final_norm_g):
    b, t, d = x_prompt.shape
    db, dt, _ = x_sample.shape
    assert d == D_MODEL and t % GDN_TILE == 0 and GDN_CONV - 1 <= dt <= SAMPLE_ROWS
    sr = SAMPLE_ROWS

    mod = _modulation(jnp.concatenate([c_prompt, c_sample], axis=0), w_mod, b_mod)
    xp = x_prompt.reshape(b * t, d)
    xs = jnp.pad(x_sample, ((0, 0), (0, sr - dt), (0, 0))).reshape(db * sr, d)

    gdn_in_t = jnp.transpose(gdn_w_in, (0, 2, 1))
    swa_out = swa_w_out.astype(BF16)
    gdn_out = gdn_w_out.astype(BF16)
    swa_splits = ((0, SWA_Q_W), (SWA_Q_W, SWA_KV_W), (SWA_Q_W + SWA_KV_W, SWA_KV_W), (SWA_Q_W + 2 * SWA_KV_W, SWA_Q_W))
    swa_scales = (SWA_SCALE, 1.0, 1.0, 1.0)
    gdn_splits = ((0, GDN_CONV_CH), (GDN_CONV_CH, GDN_V_W), (GDN_CONV_CH + GDN_V_W, 2 * GDN_V_HEADS))
    gdn_scales = (1.0, 1.0, 1.0)

    n_swa = cache_swa_k.shape[0]
    feature_major = lambda c: jnp.transpose(c, (0, 1, 3, 4, 2)).reshape(n_swa, db, SWA_KV_W, WINDOW)
    position_major = lambda c: jnp.transpose(c.reshape(n_swa, db, SWA_KV_HEADS, SWA_HEAD_DIM, WINDOW), (0, 1, 4, 2, 3))
    ck_all, cv_all = feature_major(cache_swa_k), feature_major(cache_swa_v)

    kp_out, vp_out, cp_out, sp_out, cs_out = [], [], [], [], []
    s_sample = nk_all = nv_all = None
    for layer in range(DEPTH):
        j = layer // 2
        last = layer == DEPTH - 1
        shift, scale, gate = (mod[layer, :, i * d:(i + 1) * d] for i in range(3))
        p3 = lambda a: a[:b].reshape(b, 1, d)
        s_rows = lambda a: a[b:].reshape(db, 1, d)
        g_l = norm_g[layer]
        if layer % 2 == 0:
            q, k, v, z = _inproj(xp, g_l, p3(shift), p3(scale), swa_w_in, j, swa_splits, swa_scales,
                                 (BF16, F32, F32, BF16), t, "swa_inproj_p")
            xp = _swa_prompt(swa_sinks[j], q, k, v, z, xp, p3(gate), swa_out[j], b, t)
            keep = min(WINDOW, t)
            tail = lambda a: a.reshape(b, t, SWA_KV_W)[:, -keep:].reshape(b, keep, SWA_KV_HEADS, SWA_HEAD_DIM)
            kp_out.append(tail(k))
            vp_out.append(tail(v))

            q, k, v, z = _inproj(xs, g_l, s_rows(shift), s_rows(scale), swa_w_in, j, swa_splits, swa_scales,
                                 (F32,) * 4, sr, "swa_inproj_s")
            a, nk_all, nv_all = _swa_sample(swa_sinks[j], q, k, v, z, ck_all, cv_all, nk_all, nv_all, j, dt)
            xs = _outproj(a, swa_out[j], xs, s_rows(gate), final_norm_g, False, "swa_outproj_s")
        else:
            cw, alog, dtb, ong = gdn_conv_w[j], gdn_a_log[j], gdn_dt_bias[j], gdn_o_norm_g[j]
            qkv, z, ab, tail = _gdn_front(xp, g_l, p3(shift), p3(scale), gdn_in_t, j, cw, t)
            s0 = jnp.zeros((b, GDN_V_HEADS, GDN_HEAD_DIM, GDN_HEAD_DIM), F32)
            seq = lambda a: a.reshape(b, t, a.shape[-1])
            xp, s_new = _gdn(seq(qkv), seq(z), seq(ab), seq(xp), p3(gate), s0, alog, dtb, ong, gdn_out[j],
                             final_norm_g, last)
            xp = xp.reshape(b * t, d)
            cp_out.append(tail[:, SUBLANES - (GDN_CONV - 1):])
            sp_out.append(s_new)

            xqkv, z, ab = _inproj(xs, g_l, s_rows(shift), s_rows(scale), gdn_in_t, j, gdn_splits, gdn_scales,
                                  (F32,) * 3, sr, "gdn_inproj_s")
            a, s_sample = _gdn_sample(xqkv, z, ab, state_gdn_conv[j], state_gdn_s, s_sample, j, cw, alog, dtb, ong, dt)
            xs = _outproj(a, gdn_out[j], xs, s_rows(gate), final_norm_g, last, "gdn_outproj_s")
            cs_out.append(xqkv.reshape(db, sr, GDN_CONV_CH)[:, dt - (GDN_CONV - 1):dt])

    y_prompt = xp.reshape(b, t, d)
    y_sample = xs.reshape(db, sr, d)[:, :dt]
    return (y_prompt, y_sample, jnp.stack(kp_out), jnp.stack(vp_out), position_major(nk_all), position_major(nv_all),
            jnp.stack(cp_out), jnp.stack(sp_out), jnp.stack(cs_out), s_sample)
```

```python
import functools
import itertools

import jax
import jax.numpy as jnp
from jax import lax
from jax.experimental import pallas as pl
from jax.experimental.pallas import tpu as pltpu

F32 = jnp.float32
BF16 = jnp.bfloat16

D_MODEL = 1024
DEPTH = 4
EPS = 1e-6
SWA_HEADS = 16
SWA_KV_HEADS = 4
SWA_HEAD_DIM = 64
SWA_GROUP = SWA_HEADS // SWA_KV_HEADS
WINDOW = 128
SWA_Q_W = SWA_HEADS * SWA_HEAD_DIM
SWA_KV_W = SWA_KV_HEADS * SWA_HEAD_DIM
SWA_SCALE = SWA_HEAD_DIM ** -0.5
GDN_QK_HEADS = 8
GDN_V_HEADS = 16
GDN_HEAD_DIM = 128
GDN_K_W = GDN_QK_HEADS * GDN_HEAD_DIM
GDN_V_W = GDN_V_HEADS * GDN_HEAD_DIM
GDN_CONV = 4
GDN_CONV_CH = 2 * GDN_K_W + GDN_V_W
GDN_CHUNK = 64
LANES = 128
SUBLANES = 8
HALF = LANES // 2
GDN_TILE = 2 * GDN_CHUNK
SAMPLE_ROWS = SUBLANES
VMEM_LIMIT = 56 * 1024 * 1024
PROMPT_ROWS = 512
SAMPLE_TILE_ROWS = 256
PROJ_COLS = 512
SWA_SAMPLE_SEQS = 16
GDN_SAMPLE_SEQS = 8

_NT = (((1,), (1,)), ((), ()))


def _dot(a, b):
    return jnp.dot(a.astype(BF16), b.astype(BF16), preferred_element_type=F32)


def _dot_nt(a, b):
    return lax.dot_general(a.astype(BF16), b.astype(BF16), _NT, preferred_element_type=F32)


def _dot_f32(a, b):
    return jnp.dot(a, b, preferred_element_type=F32, precision=lax.Precision.HIGHEST)


def _silu(x):
    return x * jax.nn.sigmoid(x)


def _iota(shape, dim):
    return lax.broadcasted_iota(jnp.int32, shape, dim)


def _rms(x, g):
    return x * lax.rsqrt(jnp.mean(x * x, axis=-1, keepdims=True) + EPS) * g


def _params(*sem):
    return pltpu.CompilerParams(dimension_semantics=sem, vmem_limit_bytes=VMEM_LIMIT)


def _mod_kernel(c_ref, w_ref, b_ref, o_ref):
    o_ref[...] = _dot(_silu(c_ref[...]), w_ref[...]) + b_ref[...]


def _modulation(c_all, w_mod, b_mod):
    r, d = c_all.shape
    n = w_mod.shape[-1]
    tn = PROJ_COLS
    return pl.pallas_call(
        _mod_kernel,
        grid=(DEPTH, n // tn),
        in_specs=[pl.BlockSpec((r, d), lambda l, j: (0, 0)),
                  pl.BlockSpec((None, d, tn), lambda l, j: (l, 0, j)),
                  pl.BlockSpec((None, 1, tn), lambda l, j: (l, 0, j))],
        out_specs=pl.BlockSpec((None, r, tn), lambda l, j: (l, 0, j)),
        out_shape=jax.ShapeDtypeStruct((DEPTH, r, n), F32),
        compiler_params=_params("arbitrary", "arbitrary"),
        name="adaln_modulation",
    )(c_all, w_mod, b_mod.reshape(DEPTH, 1, n))


def _per_row(ref, rows):
    a = ref[...]
    if a.ndim == 2:
        return a
    g, _, d = a.shape
    return jnp.broadcast_to(a, (g, rows // g, d)).reshape(rows, d)


def _inproj_kernel(x_ref, g_ref, sh_ref, sc_ref, w_ref, *o_refs, splits, scales, w_rows_out):
    tm = x_ref.shape[0]
    h = _rms(x_ref[...], g_ref[...]) * (1.0 + _per_row(sc_ref, tm)) + _per_row(sh_ref, tm)
    hb = h.astype(BF16)
    for o_ref, (c0, n), s in zip(o_refs, splits, scales):
        for cc in range(0, n, PROJ_COLS):
            w = min(PROJ_COLS, n - cc)
            if w_rows_out:
                r = _dot_nt(hb, w_ref[c0 + cc:c0 + cc + w, :])
            else:
                r = jnp.dot(hb, w_ref[:, c0 + cc:c0 + cc + w].astype(BF16), preferred_element_type=F32)
            o_ref[:, cc:cc + w] = (r * s if s != 1.0 else r).astype(o_ref.dtype)
        if n < o_ref.shape[1]:
            o_ref[:, n:] = jnp.zeros((o_ref.shape[0], o_ref.shape[1] - n), o_ref.dtype)


def _inproj(x, g, shift, scale, w_all, j, splits, scales, dtypes, rows_per_mod, name):
    m, d = x.shape
    assert w_all.shape[1] != w_all.shape[2] and d in w_all.shape[1:]
    pad = lambda n: -(-n // LANES) * LANES
    tm = min(PROMPT_ROWS, rows_per_mod) if rows_per_mod >= PROMPT_ROWS else min(SAMPLE_TILE_ROWS, m)
    if rows_per_mod >= tm:
        per = rows_per_mod // tm
        mod_spec = pl.BlockSpec((None, 1, d), lambda i: (i // per, 0, 0))
    else:
        mod_spec = pl.BlockSpec((tm // rows_per_mod, 1, d), lambda i: (i, 0, 0))
    return pl.pallas_call(
        functools.partial(_inproj_kernel, splits=splits, scales=scales, w_rows_out=w_all.shape[2] == d),
        grid=(m // tm,),
        in_specs=[pl.BlockSpec((tm, d), lambda i: (i, 0)),
                  pl.BlockSpec((1, d), lambda i: (0, 0)),
                  mod_spec, mod_spec,
                  pl.BlockSpec((None,) + w_all.shape[1:], lambda i: (j, 0, 0), pipeline_mode=pl.Buffered(1))],
        out_specs=[pl.BlockSpec((tm, pad(n)), lambda i: (i, 0)) for _, n in splits],
        out_shape=[jax.ShapeDtypeStruct((m, pad(n)), dtype) for (_, n), dtype in zip(splits, dtypes)],
        compiler_params=_params("arbitrary"),
        name=name,
    )(x, g.reshape(1, d), shift, scale, w_all)


def _gdn_front_kernel(x_ref, g_ref, sh_ref, sc_ref, w_ref, cw_ref, qkv_ref, z_ref, ab_ref, tail_ref, carry, rbuf,
                      *, tiles_per_seq):
    tm = x_ref.shape[0]
    head = GDN_CONV - 1
    base = SUBLANES - head
    width = 4 * LANES

    @pl.when(pl.program_id(0) % tiles_per_seq == 0)
    def _():
        carry[...] = jnp.zeros(carry.shape, F32)

    h = _rms(x_ref[...], g_ref[...]) * (1.0 + sc_ref[...]) + sh_ref[...]
    hb = h.astype(BF16)
    proj = lambda c0: _dot_nt(hb, w_ref[c0:c0 + width, :])
    n_qkv, n_z = GDN_CONV_CH // width, GDN_V_W // width
    r_next = proj(0)
    for cb in range(n_qkv):
        cols = slice(cb * width, (cb + 1) * width)
        r = r_next
        if cb + 1 < n_qkv:
            r_next = proj((cb + 1) * width)
        if cb % 2 == 1:
            zc = cb // 2
            z_ref[:, zc * width:(zc + 1) * width] = proj(GDN_CONV_CH + zc * width).astype(z_ref.dtype)
        rbuf[0:SUBLANES, :] = carry[:, cols]
        rbuf[SUBLANES:SUBLANES + tm, :] = r
        y = rbuf[base:base + tm, :] * cw_ref[0:1, cols]
        for w in range(1, GDN_CONV):
            y = y + rbuf[base + w:base + w + tm, :] * cw_ref[w:w + 1, cols]
        carry[:, cols] = r[tm - SUBLANES:tm]
        tail_ref[:, cols] = r[tm - SUBLANES:tm]
        y = _silu(y)
        if cb * width < 2 * GDN_K_W:
            parts = []
            for p in range(width // GDN_HEAD_DIM):
                blk = y[:, p * GDN_HEAD_DIM:(p + 1) * GDN_HEAD_DIM]
                inv = lax.rsqrt(jnp.sum(blk * blk, axis=-1, keepdims=True) + EPS)
                parts.append(blk * (inv * GDN_HEAD_DIM ** -0.5 if cb * width < GDN_K_W else inv))
            y = jnp.concatenate(parts, axis=1)
        qkv_ref[:, cols] = y.astype(qkv_ref.dtype)
    assert n_z == n_qkv // 2
    nab = 2 * GDN_V_HEADS
    ab_ref[:, 0:nab] = _dot_nt(hb, w_ref[GDN_CONV_CH + GDN_V_W:GDN_CONV_CH + GDN_V_W + nab, :])
    ab_ref[:, nab:] = jnp.zeros((tm, LANES - nab), F32)


def _gdn_front(x, g, shift, scale, w_all_t, j, cw, t):
    m, d = x.shape
    tm = min(PROMPT_ROWS, t)
    per = t // tm
    mod_spec = pl.BlockSpec((None, 1, d), lambda i: (i // per, 0, 0))
    row = lambda w: pl.BlockSpec((tm, w), lambda i: (i, 0))
    const = lambda i: (0, 0)
    return pl.pallas_call(
        functools.partial(_gdn_front_kernel, tiles_per_seq=per),
        grid=(m // tm,),
        in_specs=[row(d), pl.BlockSpec((1, d), const), mod_spec, mod_spec,
                  pl.BlockSpec((None,) + w_all_t.shape[1:], lambda i: (j, 0, 0), pipeline_mode=pl.Buffered(1)),
                  pl.BlockSpec(cw.shape, const)],
        out_specs=[row(GDN_CONV_CH), row(GDN_V_W), row(LANES),
                   pl.BlockSpec((None, SUBLANES, GDN_CONV_CH), lambda i: (i // per, 0, 0))],
        out_shape=[jax.ShapeDtypeStruct((m, GDN_CONV_CH), BF16), jax.ShapeDtypeStruct((m, GDN_V_W), BF16),
                   jax.ShapeDtypeStruct((m, LANES), F32), jax.ShapeDtypeStruct((m // t, SUBLANES, GDN_CONV_CH), F32)],
        scratch_shapes=[pltpu.VMEM((SUBLANES, GDN_CONV_CH), F32), pltpu.VMEM((SUBLANES + tm, 4 * LANES), F32)],
        compiler_params=_params("arbitrary"),
        name="gdn_front_p",
    )(x, g.reshape(1, d), shift, scale, w_all_t, cw)


def _finish(a, w_ref, x, gate, fg_ref, final_norm):
    y = x + gate * _dot(a, w_ref[...])
    return _rms(y, fg_ref[...]) if final_norm else y


def _outproj_kernel(a_ref, w_ref, x_ref, gate_ref, fg_ref, o_ref, *, final_norm):
    gate = _per_row(gate_ref, x_ref.shape[0])
    o_ref[...] = _finish(a_ref[...], w_ref, x_ref[...], gate, fg_ref, final_norm)


def _outproj(a, w, x, gate, fg, final_norm, name):
    m, k = a.shape
    d = x.shape[-1]
    tm = min(PROMPT_ROWS, m)
    return pl.pallas_call(
        functools.partial(_outproj_kernel, final_norm=final_norm),
        grid=(m // tm,),
        in_specs=[pl.BlockSpec((tm, k), lambda i: (i, 0)),
                  pl.BlockSpec((k, d), lambda i: (0, 0)),
                  pl.BlockSpec((tm, d), lambda i: (i, 0)),
                  pl.BlockSpec((tm // SAMPLE_ROWS, 1, d), lambda i: (i, 0, 0)),
                  pl.BlockSpec((1, d), lambda i: (0, 0))],
        out_specs=pl.BlockSpec((tm, d), lambda i: (i, 0)),
        out_shape=jax.ShapeDtypeStruct((m, d), F32),
        compiler_params=_params("arbitrary"),
        name=name,
    )(a, w, x, gate, fg.reshape(1, d))


def _swa_block(q, kcat, vcat, mask4, sinks_ref):
    nq = q.shape[0]
    lo = _iota((1, LANES), 1) < HALF
    hi = jnp.logical_not(lo)
    row = _iota((SWA_GROUP * nq, 1), 0)
    kvs = range(SWA_KV_HEADS)
    blk = lambda a, pb: a[:, pb * LANES:(pb + 1) * LANES]
    q4s = []
    for kh in kvs:
        parts = []
        for g in range(SWA_GROUP):
            qg = blk(q, 2 * kh + g // 2)
            if g % 2 != kh % 2:
                qg = pltpu.roll(qg, HALF, axis=1)
            parts.append(jnp.where(lo if kh % 2 == 0 else hi, qg, 0.0))
        q4s.append(jnp.concatenate(parts, axis=0))
    logits = [_dot_nt(q4, blk(kcat, kh // 2)) for kh, q4 in zip(kvs, q4s)]
    es, invs = [], []
    for kh, lg in zip(kvs, logits):
        sink = jnp.full((SWA_GROUP * nq, 1), sinks_ref[SWA_GROUP * kh], F32)
        for g in range(1, SWA_GROUP):
            sink = jnp.where(row >= g * nq, sinks_ref[SWA_GROUP * kh + g], sink)
        lg = jnp.where(mask4, lg, -jnp.inf)
        m = jnp.maximum(jnp.max(lg, axis=-1, keepdims=True), sink)
        e = jnp.exp(lg - m)
        es.append(e)
        invs.append(1.0 / (jnp.sum(e, axis=-1, keepdims=True) + jnp.exp(sink - m)))
    o4s = [_dot(e, blk(vcat, kh // 2)) * inv for kh, e, inv in zip(kvs, es, invs)]
    outs = [None] * (SWA_HEADS // 2)
    for kh, o4 in zip(kvs, o4s):
        for g in range(SWA_GROUP):
            piece = o4[g * nq:(g + 1) * nq]
            if g % 2 != kh % 2:
                piece = pltpu.roll(piece, HALF, axis=1)
            piece = jnp.where(lo if g % 2 == 0 else hi, piece, 0.0)
            qb = 2 * kh + g // 2
            outs[qb] = piece if outs[qb] is None else outs[qb] + piece
    return jnp.concatenate(outs, axis=1)


def _band_mask(nq, nk, jmin):
    i = _iota((SWA_GROUP * nq, nk), 0) % nq
    j = _iota((SWA_GROUP * nq, nk), 1)
    return (j >= i) & (j <= i + WINDOW) & (j >= jmin)


def _swa_prompt_kernel(sinks_ref, q_ref, kp_ref, vp_ref, k_ref, v_ref, z_ref, x_ref, gate_ref, w_ref,
                       o_ref, kbuf, vbuf, abuf, *, tq):
    t = pl.program_id(1)
    kbuf[0:WINDOW] = kp_ref[...]
    kbuf[WINDOW:WINDOW + tq] = k_ref[...]
    vbuf[0:WINDOW] = vp_ref[...]
    vbuf[WINDOW:WINDOW + tq] = v_ref[...]

    def body(sb, carry):
        r = pl.multiple_of(sb * WINDOW, WINDOW)
        jmin = jnp.where((t == 0) & (sb == 0), WINDOW, 0)
        mask4 = _band_mask(WINDOW, 2 * WINDOW, jmin)
        abuf[pl.ds(r, WINDOW), :] = _swa_block(q_ref[pl.ds(r, WINDOW), :].astype(F32), kbuf[pl.ds(r, 2 * WINDOW), :],
                                               vbuf[pl.ds(r, 2 * WINDOW), :], mask4, sinks_ref)
        return carry

    lax.fori_loop(0, tq // WINDOW, body, 0)
    a = abuf[...] * _silu(z_ref[...].astype(F32))
    o_ref[...] = x_ref[...] + gate_ref[...] * _dot(a, w_ref[...])


def _swa_prompt(sinks, q, k, v, z, x, gate, w_out, b, t):
    d = x.shape[-1]
    tq = min(PROMPT_ROWS, t)
    nt = t // tq
    sub = tq // WINDOW
    row = lambda bi, ti: (bi * nt + ti, 0)
    prev = lambda bi, ti: (bi * nt * sub + jnp.maximum(ti * sub - 1, 0), 0)
    const = lambda bi, ti: (0, 0)
    return pl.pallas_call(
        functools.partial(_swa_prompt_kernel, tq=tq),
        grid=(b, nt),
        in_specs=[pl.BlockSpec(memory_space=pltpu.SMEM),
                  pl.BlockSpec((tq, SWA_Q_W), row),
                  pl.BlockSpec((WINDOW, SWA_KV_W), prev),
                  pl.BlockSpec((WINDOW, SWA_KV_W), prev),
                  pl.BlockSpec((tq, SWA_KV_W), row),
                  pl.BlockSpec((tq, SWA_KV_W), row),
                  pl.BlockSpec((tq, SWA_Q_W), row),
                  pl.BlockSpec((tq, d), row),
                  pl.BlockSpec((None, 1, d), lambda bi, ti: (bi, 0, 0)),
                  pl.BlockSpec(w_out.shape, const)],
        out_specs=pl.BlockSpec((tq, d), row),
        out_shape=jax.ShapeDtypeStruct(x.shape, F32),
        scratch_shapes=[pltpu.VMEM((WINDOW + tq, SWA_KV_W), F32),
                        pltpu.VMEM((WINDOW + tq, SWA_KV_W), F32),
                        pltpu.VMEM((tq, SWA_Q_W), F32)],
        compiler_params=_params("arbitrary", "arbitrary"),
        name="swa_prompt",
    )(sinks, q, k, v, k, v, z, x, gate, w_out)


def _swa_sample_kernel(sinks_ref, q_ref, k_ref, v_ref, z_ref, ck_ref, cv_ref, *rest, bb, dt, has_prev):
    a_ref, nk_ref, nv_ref = rest[2:] if has_prev else rest
    nq, grp = SAMPLE_ROWS, SWA_GROUP
    lane = _iota((1, LANES), 1)
    lo = lane < HALF
    hi = jnp.logical_not(lo)
    row = _iota((grp * nq, 1), 0)
    qi = row % nq
    see_cache, see_own = lane >= qi, lane <= qi
    zpad = jnp.zeros((WINDOW - nq, LANES), F32)
    rows = lambda bi: slice(bi * nq, (bi + 1) * nq)
    blk = lambda pb: slice(pb * LANES, (pb + 1) * LANES)
    units = [(bi, kh) for bi in range(bb) for kh in range(SWA_KV_HEADS)]

    q4s = []
    for bi, kh in units:
        keep = lo if kh % 2 == 0 else hi
        parts = []
        for g in range(grp):
            qg = q_ref[rows(bi), blk(2 * kh + g // 2)]
            if g % 2 != kh % 2:
                qg = pltpu.roll(qg, HALF, axis=1)
            parts.append(jnp.where(keep, qg, 0.0))
        q4s.append(jnp.concatenate(parts, axis=0))
    own = lambda ref, bi, pb: jnp.concatenate([ref[rows(bi), blk(pb)], zpad], axis=0)
    lcs = [_dot(q4, ck_ref[bi, blk(kh // 2), :]) for (bi, kh), q4 in zip(units, q4s)]
    los = [_dot_nt(q4, own(k_ref, bi, kh // 2)) for (bi, kh), q4 in zip(units, q4s)]
    ecs, eos, invs = [], [], []
    for (bi, kh), lc, lw in zip(units, lcs, los):
        sink = jnp.full((grp * nq, 1), sinks_ref[grp * kh], F32)
        for g in range(1, grp):
            sink = jnp.where(row >= g * nq, sinks_ref[grp * kh + g], sink)
        lc = jnp.where(see_cache, lc, -jnp.inf)
        lw = jnp.where(see_own, lw, -jnp.inf)
        m = jnp.maximum(jnp.maximum(jnp.max(lc, axis=-1, keepdims=True), jnp.max(lw, axis=-1, keepdims=True)), sink)
        ec, eo = jnp.exp(lc - m), jnp.exp(lw - m)
        den = jnp.sum(ec, axis=-1, keepdims=True) + jnp.sum(eo, axis=-1, keepdims=True) + jnp.exp(sink - m)
        ecs.append(ec)
        eos.append(eo)
        invs.append(1.0 / den)
    o4s = [(_dot_nt(ec, cv_ref[bi, blk(kh // 2), :]) + _dot(eo, own(v_ref, bi, kh // 2))) * inv
           for (bi, kh), ec, eo, inv in zip(units, ecs, eos, invs)]
    for bi in range(bb):
        outs = [None] * (SWA_HEADS // 2)
        for kh in range(SWA_KV_HEADS):
            o4 = o4s[bi * SWA_KV_HEADS + kh]
            for g in range(grp):
                piece = o4[g * nq:(g + 1) * nq]
                if g % 2 != kh % 2:
                    piece = pltpu.roll(piece, HALF, axis=1)
                piece = jnp.where(lo if g % 2 == 0 else hi, piece, 0.0)
                qb = 2 * kh + g // 2
                outs[qb] = piece if outs[qb] is None else outs[qb] + piece
        a_ref[rows(bi), :] = jnp.concatenate(outs, axis=1) * _silu(z_ref[rows(bi), :])

    fresh = lane >= WINDOW - dt
    for src, cache, dst in ((k_ref, ck_ref, nk_ref), (v_ref, cv_ref, nv_ref)):
        for pb in range(SWA_KV_W // LANES):
            new = src[:, blk(pb)]
            if bb * nq < LANES:
                new = jnp.concatenate([new, jnp.zeros((LANES - bb * nq, LANES), F32)], axis=0)
            new_t = new.T
            for bi in range(bb):
                cols = pltpu.roll(new_t, (WINDOW - dt - bi * nq) % LANES, axis=1)
                dst[bi, blk(pb), :] = jnp.where(fresh, cols, pltpu.roll(cache[bi, blk(pb), :], WINDOW - dt, axis=1))


def _swa_sample(sinks, q, k, v, z, ck_all, cv_all, nk_prev, nv_prev, j, dt):
    db = ck_all.shape[1]
    bb = min(SWA_SAMPLE_SEQS, db)
    nq = SAMPLE_ROWS
    assert 2 * dt <= nq
    row = lambda i: (i, 0)
    c_spec = pl.BlockSpec((None, bb, SWA_KV_W, WINDOW), lambda i: (j, i, 0, 0))
    args = [sinks, q, k, v, z, ck_all, cv_all]
    in_specs = [pl.BlockSpec(memory_space=pltpu.SMEM),
                pl.BlockSpec((bb * nq, SWA_Q_W), row),
                pl.BlockSpec((bb * nq, SWA_KV_W), row),
                pl.BlockSpec((bb * nq, SWA_KV_W), row),
                pl.BlockSpec((bb * nq, SWA_Q_W), row),
                c_spec, c_spec]
    aliases = {}
    if nk_prev is not None:
        args += [nk_prev, nv_prev]
        in_specs += [pl.BlockSpec(memory_space=pl.ANY)] * 2
        aliases = {len(args) - 2: 1, len(args) - 1: 2}
    return pl.pallas_call(
        functools.partial(_swa_sample_kernel, bb=bb, dt=dt, has_prev=nk_prev is not None),
        grid=(db // bb,),
        in_specs=in_specs,
        out_specs=[pl.BlockSpec((bb * nq, SWA_Q_W), row), c_spec, c_spec],
        out_shape=[jax.ShapeDtypeStruct(q.shape, F32),
                   jax.ShapeDtypeStruct(ck_all.shape, F32),
                   jax.ShapeDtypeStruct(cv_all.shape, F32)],
        input_output_aliases=aliases,
        compiler_params=_params("arbitrary"),
        name="swa_sample",
    )(*args)


def _block_mul(a, b):
    size, width = b.shape
    b16 = b.astype(BF16)
    zero = jnp.zeros_like(b16)
    blk = _iota((1, width), 1) // size
    return _dot(a, jnp.concatenate([jnp.where(blk == r, b16, zero) for r in range(width // size)], axis=0))


def _diag2(a, b):
    return jnp.concatenate([jnp.concatenate([a, jnp.zeros_like(b)], axis=1),
                            jnp.concatenate([jnp.zeros_like(a), b], axis=1)], axis=0)


def _gdn_kernel(qkv, z_ref, ab_ref, x_ref, gate_ref, s0_ref, alog_ref, dtb_ref, ong_ref, w_ref, fg_ref, o_ref, s_ref,
                obuf, *, nb, final_norm):
    nchunks = GDN_TILE // GDN_CHUNK
    R, C, HD = GDN_TILE, GDN_CHUNK, GDN_HEAD_DIM
    t = pl.program_id(1)
    seqs = range(nb)

    @pl.when(t == 0)
    def _():
        s_ref[...] = s0_ref[...]

    ri, ci = _iota((R, R), 0), _iota((R, R), 1)
    same = (ri // C) == (ci // C)
    tri, ones = jnp.where(same & (ci <= ri), 1.0, 0.0), jnp.where(same, 1.0, 0.0)
    gcs, betas, egcs, bgcs, gcTs, glTs, dlTs = ([None] * nb for _ in range(7))

    def prologue(sq):
        ab = ab_ref[sq]
        g = -jnp.exp(alog_ref[...]) * jax.nn.softplus(ab + dtb_ref[...])
        beta = pltpu.roll(jax.nn.sigmoid(ab), LANES - GDN_V_HEADS, axis=1)
        gc = _dot_f32(tri, g)
        gl = _dot_f32(ones, g)
        egc = jnp.exp(gc)
        gcT, glT = gc.T, gl.T
        gcs[sq], betas[sq], egcs[sq], bgcs[sq] = gc, beta, egc, beta * egc
        gcTs[sq], glTs[sq], dlTs[sq] = gcT, glT, glT - gcT
        yield

    lane = _iota((1, LANES), 1)
    lo = lane < HALF
    jj = lane % C
    ii = _iota((C, 1), 0)
    incl, strict = ii >= jj, ii > jj
    zc = jnp.zeros((C, HD), F32)

    def col2(arr, r0, h0):
        return jnp.where(lo, arr[r0:r0 + C, h0:h0 + 1], arr[r0:r0 + C, h0 + 1:h0 + 2])

    def pick(c, row_a, row_b):
        if c == 0:
            return jnp.where(lo, row_a, pltpu.roll(row_b, HALF, axis=1))
        return jnp.where(lo, pltpu.roll(row_a, HALF, axis=1), row_b)

    pairs = range(GDN_QK_HEADS)
    heads = range(GDN_V_HEADS)
    units = [(c, p) for c in range(nchunks) for p in pairs]
    quads = [(c, pp) for c in range(nchunks) for pp in range(GDN_QK_HEADS // 2)]
    qsl = lambda sq, c, p: qkv[sq, c * C:(c + 1) * C, p * HD:(p + 1) * HD]
    ksl = lambda sq, c, p: qkv[sq, c * C:(c + 1) * C, GDN_K_W + p * HD:GDN_K_W + (p + 1) * HD]
    vsl = lambda sq, c, h: qkv[sq, c * C:(c + 1) * C, 2 * GDN_K_W + h * HD:2 * GDN_K_W + (h + 1) * HD]
    col = lambda arr, c, h: arr[c * C:(c + 1) * C, h:h + 1]
    jq = _iota((1, 2 * LANES), 1) % C
    kts, lows, qks, sols = ([None] * nb for _ in range(4))

    def products(sq):
        kts[sq] = [qkv[sq, :, GDN_K_W + p * HD:GDN_K_W + (p + 1) * HD].astype(F32).T for p in pairs]
        yield
        ms = [_dot_nt(jnp.concatenate([ksl(sq, c, p), qsl(sq, c, p)], axis=0), jnp.concatenate([ksl(sq, c, p)] * 2, axis=0))
              for c, p in units]
        yield
        lows[sq], qks[sq] = [], []
        for (c, p), m in zip(units, ms):
            h0 = 2 * p
            row_gc = pick(c, gcTs[sq][h0:h0 + 1, :], gcTs[sq][h0 + 1:h0 + 2, :])
            decay = jnp.exp(jnp.where(incl, col2(gcs[sq], c * C, h0) - row_gc, -jnp.inf))
            lows[sq].append(jnp.where(strict, m[0:C] * decay, 0.0) * col2(betas[sq], c * C, h0))
            qks[sq].append(m[C:2 * C] * decay)
            if p % 4 == 3:
                yield

    def inverse(sq):
        low4 = [jnp.concatenate([lows[sq][c * GDN_QK_HEADS + 2 * pp], lows[sq][c * GDN_QK_HEADS + 2 * pp + 1]], axis=1)
                for c, pp in quads]
        invs = [jnp.where(ii == jq, 1.0, 0.0) - jnp.where((ii // 2) == (jq // 2), low, 0.0) for low in low4]
        yield
        s = 2
        while s < C:
            below = ((ii // s) % 2 == 1) & ((jq // s) == (ii // s) - 1)
            eps, new = [], []
            for i, (low, inv) in enumerate(zip(low4, invs)):
                eps.append(_block_mul(jnp.where(below, low, 0.0), inv))
                if i % 2:
                    yield
            for i, (inv, ep) in enumerate(zip(invs, eps)):
                new.append(inv - _block_mul(inv, ep))
                if i % 2:
                    yield
            invs = new
            s *= 2
        sols[sq] = []
        for (c, pp), inv in zip(quads, invs):
            rhs = []
            for idx in range(4):
                h = 4 * pp + idx
                piece = [vsl(sq, c, h) * col(betas[sq], c, h), ksl(sq, c, h // 2) * col(bgcs[sq], c, h)]
                rhs.append(jnp.concatenate([zc] * (2 * idx) + piece + [zc] * (6 - 2 * idx), axis=1))
            sols[sq].append(_dot(inv, jnp.concatenate(rhs, axis=0)))
            if pp % 2 == 1:
                yield

    def state(sq):
        for c in range(nchunks):
            in_chunk = (lane // C) == c
            quad = lambda h: sols[sq][c * (GDN_QK_HEADS // 2) + h // 4]
            w_v = lambda h: quad(h)[:, 2 * (h % 4) * HD:(2 * (h % 4) + 1) * HD]
            w_k = lambda h: quad(h)[:, (2 * (h % 4) + 1) * HD:(2 * (h % 4) + 2) * HD]
            a_s = []
            for p in pairs:
                top = jnp.concatenate([w_k(2 * p), w_k(2 * p + 1)], axis=1)
                bot = jnp.concatenate([qsl(sq, c, p) * col(egcs[sq], c, 2 * p), qsl(sq, c, p) * col(egcs[sq], c, 2 * p + 1)],
                                      axis=1)
                a_s.append(_dot(jnp.concatenate([top, bot], axis=0), _diag2(s_ref[sq, 2 * p], s_ref[sq, 2 * p + 1])))
            yield
            us = [[w_v(2 * p + idx) - a[0:C, idx * HD:(idx + 1) * HD] for idx in range(2)] for p, a in zip(pairs, a_s)]
            for p, a, u in zip(pairs, a_s, us):
                obuf[sq, c * C:(c + 1) * C, 2 * p * HD:(2 * p + 2) * HD] = (
                    a[C:2 * C] + _dot(qks[sq][c * GDN_QK_HEADS + p], _diag2(*u)))
            yield
            for p, u in zip(pairs, us):
                kd, ue, gt = [], [], []
                for idx in range(2):
                    h = 2 * p + idx
                    kd.append(kts[sq][p] * jnp.exp(jnp.where(in_chunk, dlTs[sq][h:h + 1, :], -jnp.inf)))
                    ue.append(jnp.concatenate([u[idx], zc] if c == 0 else [zc, u[idx]], axis=0))
                    gt.append(jnp.exp(pick(c, glTs[sq][h:h + 1, :], glTs[sq][h:h + 1, :])))
                upd = _dot(jnp.concatenate(kd, axis=1), _diag2(*ue))
                for idx in range(2):
                    h = 2 * p + idx
                    s_ref[sq, h] = s_ref[sq, h] * gt[idx] + upd[:, idx * HD:(idx + 1) * HD]
                if p % 4 == 3:
                    yield

    def run(*gens):
        gens = list(gens)
        while gens:
            for gen in list(gens):
                if next(gen, "done") == "done":
                    gens.remove(gen)

    for k in range(nb + 2):
        live = [itertools.chain(prologue(k), products(k))] if k < nb else []
        live += [inverse(k - 1)] if 0 <= k - 1 < nb else []
        live += [state(k - 2)] if 0 <= k - 2 < nb else []
        run(*live)

    rows = []
    for sq in seqs:
        o = obuf[sq]
        parts = [_rms(o[:, h * HD:(h + 1) * HD], ong_ref[...]) for h in heads]
        rows.append(jnp.concatenate(parts, axis=1) * _silu(z_ref[sq].astype(F32)))
    out = _dot(jnp.concatenate(rows, axis=0), w_ref[...])
    for sq in seqs:
        y = x_ref[sq] + gate_ref[sq] * out[sq * R:(sq + 1) * R]
        o_ref[sq] = _rms(y, fg_ref[...]) if final_norm else y


def _gdn(qkv, z, ab, x, gate, s0, alog, dtb, ong, w_out, fg, final_norm):
    b, t, d = x.shape
    nb = 1
    rows_in = GDN_TILE
    tile = lambda w: pl.BlockSpec((nb, rows_in, w), lambda bi, ti: (bi, ti, 0))
    const = lambda bi, ti: (0, 0)
    per_b3 = lambda bi, ti: (bi, 0, 0)
    per_b4 = lambda bi, ti: (bi, 0, 0, 0)
    vec = lambda a: jnp.pad(a.astype(F32), (0, LANES - a.shape[0])).reshape(1, LANES)
    state = (GDN_V_HEADS, GDN_HEAD_DIM, GDN_HEAD_DIM)
    return pl.pallas_call(
        functools.partial(_gdn_kernel, nb=nb, final_norm=final_norm),
        grid=(b // nb, t // rows_in),
        in_specs=[tile(GDN_CONV_CH), tile(GDN_V_W), tile(LANES), tile(d),
                  pl.BlockSpec((nb, 1, d), per_b3),
                  pl.BlockSpec((nb,) + state, per_b4),
                  pl.BlockSpec((1, LANES), const),
                  pl.BlockSpec((1, LANES), const),
                  pl.BlockSpec((1, GDN_HEAD_DIM), const),
                  pl.BlockSpec(w_out.shape, const),
                  pl.BlockSpec((1, d), const)],
        out_specs=[tile(d), pl.BlockSpec((nb,) + state, per_b4)],
        out_shape=[jax.ShapeDtypeStruct(x.shape, F32),
                   jax.ShapeDtypeStruct((b,) + state, F32)],
        scratch_shapes=[pltpu.VMEM((nb, GDN_TILE, GDN_V_W), F32)],
        compiler_params=_params("arbitrary", "arbitrary"),
        name="gdn_prompt",
    )(qkv, z, ab, x, gate, s0, vec(alog), vec(dtb), ong.reshape(1, GDN_HEAD_DIM), w_out, fg.reshape(1, d))


def _tile_roll(a, d):
    r, w = a.shape
    return pltpu.roll(a.reshape(r // SUBLANES, SUBLANES, w), d, axis=1).reshape(r, w)


def _gdn_sample_kernel(xqkv_ref, z_ref, ab_ref, conv_ref, s_ref, cw_ref, alog_ref, dtb_ref, ong_ref, *rest,
                       bb, dt, has_prev):
    a_ref, so_ref, st8, qkv = rest[1:] if has_prev else rest
    R, HD = bb * SUBLANES, GDN_HEAD_DIM
    tix = _iota((R, 1), 0) % SUBLANES
    live = tix < dt
    head = GDN_CONV - 1
    lane = _iota((1, LANES), 1)

    st8[...] = jnp.zeros(st8.shape, F32)
    for bi in range(bb):
        st8[(bi + 1) * SUBLANES - head:(bi + 1) * SUBLANES, :] = conv_ref[bi]
    for cb in range(GDN_CONV_CH // GDN_K_W):
        cols = slice(cb * GDN_K_W, (cb + 1) * GDN_K_W)
        x, st = xqkv_ref[:, cols], st8[:, cols]
        y = x * cw_ref[head:head + 1, cols]
        for d in range(1, GDN_CONV):
            y = y + jnp.where(tix >= d, _tile_roll(x, d), _tile_roll(st, d)) * cw_ref[head - d:head - d + 1, cols]
        y = _silu(y)
        if cb < 2:
            for p in range(GDN_QK_HEADS):
                blk = y[:, p * HD:(p + 1) * HD]
                inv = lax.rsqrt(jnp.sum(blk * blk, axis=-1, keepdims=True) + EPS)
                if cb == 0:
                    inv = inv * HD ** -0.5
                qkv[:, cb * GDN_K_W + p * HD:cb * GDN_K_W + (p + 1) * HD] = blk * inv
        else:
            qkv[:, cols] = y

    ab = ab_ref[...]
    g = jnp.where(live, -jnp.exp(alog_ref[...]) * jax.nn.softplus(ab + dtb_ref[...]), 0.0)
    beta = jnp.where(live, pltpu.roll(jax.nn.sigmoid(ab), LANES - GDN_V_HEADS, axis=1), 0.0)
    gc, gl = g, g
    for d in range(1, SUBLANES):
        gl = gl + _tile_roll(g, d)
        if d < dt:
            gc = gc + jnp.where(tix >= d, _tile_roll(g, d), 0.0)
    egc = jnp.exp(gc)
    bgc = beta * egc
    ekl = jnp.where(live, jnp.exp(jnp.where(live, gl - gc, 0.0)), 0.0)
    gtot = jnp.exp(gl)
    dec = [None] + [jnp.where(tix >= d, jnp.exp(jnp.where(tix >= d, gc - _tile_roll(gc, d), 0.0)), 0.0)
                    for d in range(1, dt)]

    zero = jnp.zeros((R, LANES), F32)
    kk, qk = [zero] * dt, [zero] * dt
    qsl = lambda p: qkv[:, p * HD:(p + 1) * HD]
    ksl = lambda p: qkv[:, GDN_K_W + p * HD:GDN_K_W + (p + 1) * HD]
    vsl = lambda h: qkv[:, 2 * GDN_K_W + h * HD:2 * GDN_K_W + (h + 1) * HD]
    for p in range(GDN_QK_HEADS):
        mine = (lane // 2) == p
        for d in range(dt):
            ks = ksl(p) if d == 0 else _tile_roll(ksl(p), d)
            qk[d] = jnp.where(mine, jnp.sum(qsl(p) * ks, axis=-1, keepdims=True), qk[d])
            if d:
                kk[d] = jnp.where(mine, jnp.sum(ksl(p) * ks, axis=-1, keepdims=True), kk[d])
    low = [None] + [beta * kk[d] * dec[d] for d in range(1, dt)]
    qkd = [qk[0]] + [qk[d] * dec[d] for d in range(1, dt)]
    l1s1, l1s2 = _tile_roll(low[1], 1), _tile_roll(low[1], 2)
    tinv = [None, -low[1], -low[2] + low[1] * l1s1,
            -low[3] + low[1] * _tile_roll(low[2], 1) + low[2] * l1s2 - low[1] * l1s1 * l1s2]
    heads = range(GDN_V_HEADS)
    spread = lambda arr: [jnp.broadcast_to(arr[:, h:h + 1], (R, LANES)) for h in heads]
    s_beta, s_bgc, s_egc, s_ekl, s_gtot = spread(beta), spread(bgc), spread(egc), spread(ekl), spread(gtot)
    s_tinv = [None] + [spread(tinv[d]) for d in range(1, dt)]
    s_qkd = [spread(qkd[d]) for d in range(dt)]

    xvs, m8s = [], []
    for h in heads:
        p = h // 2
        rv, rk = vsl(h) * s_beta[h], ksl(p) * s_bgc[h]
        xv, xk = rv, rk
        for d in range(1, dt):
            xv = xv + s_tinv[d][h] * _tile_roll(rv, d)
            xk = xk + s_tinv[d][h] * _tile_roll(rk, d)
        xvs.append(xv)
        m8s.append(jnp.where(live, xk, _tile_roll(qsl(p) * s_egc[h], dt)))
    tile = lambda a, bi: a[bi * SUBLANES:(bi + 1) * SUBLANES]
    aws = [jnp.concatenate([_dot(tile(m8s[h], bi), s_ref[bi, h]) for bi in range(bb)], axis=0) for h in heads]
    outs, uds = [], []
    for h in heads:
        u = jnp.where(live, xvs[h] - aws[h], 0.0)
        o = _tile_roll(aws[h], SUBLANES - dt) + s_qkd[0][h] * u
        for d in range(1, dt):
            o = o + s_qkd[d][h] * _tile_roll(u, d)
        outs.append(_rms(o, ong_ref[...]))
        uds.append(u * s_ekl[h])
    a_ref[...] = jnp.concatenate(outs, axis=1) * _silu(z_ref[...])
    tn = (((0,), (0,)), ((), ()))
    for h in heads:
        kp = ksl(h // 2)
        for bi in range(bb):
            upd = lax.dot_general(tile(kp, bi).astype(BF16), tile(uds[h], bi).astype(BF16), tn,
                                  preferred_element_type=F32)
            so_ref[bi, h] = s_ref[bi, h] * s_gtot[h][bi * SUBLANES:bi * SUBLANES + 1] + upd


def _gdn_sample(xqkv, z, ab, conv, s_all, s_prev, j, cw, alog, dtb, ong, dt):
    db = conv.shape[0]
    bb = min(GDN_SAMPLE_SEQS, db)
    r = bb * SUBLANES
    assert 2 * dt <= SUBLANES and dt == GDN_CONV
    row = lambda i: (i, 0)
    const = lambda i: (0, 0)
    state = (GDN_V_HEADS, GDN_HEAD_DIM, GDN_HEAD_DIM)
    s_spec = pl.BlockSpec((None, bb) + state, lambda i: (j, i, 0, 0, 0))
    vec = lambda a: jnp.pad(a.astype(F32), (0, LANES - a.shape[0])).reshape(1, LANES)
    args = [xqkv, z, ab, conv, s_all, cw, vec(alog), vec(dtb), ong.reshape(1, GDN_HEAD_DIM)]
    in_specs = [pl.BlockSpec((r, GDN_CONV_CH), row),
                pl.BlockSpec((r, GDN_V_W), row),
                pl.BlockSpec((r, LANES), row),
                pl.BlockSpec((bb, GDN_CONV - 1, GDN_CONV_CH), lambda i: (i, 0, 0)),
                s_spec,
                pl.BlockSpec(cw.shape, const),
                pl.BlockSpec((1, LANES), const),
                pl.BlockSpec((1, LANES), const),
                pl.BlockSpec((1, GDN_HEAD_DIM), const)]
    aliases = {}
    if s_prev is not None:
        args.append(s_prev)
        in_specs.append(pl.BlockSpec(memory_space=pl.ANY))
        aliases = {len(args) - 1: 1}
    return pl.pallas_call(
        functools.partial(_gdn_sample_kernel, bb=bb, dt=dt, has_prev=s_prev is not None),
        grid=(db // bb,),
        in_specs=in_specs,
        out_specs=[pl.BlockSpec((r, GDN_V_W), row), s_spec],
        out_shape=[jax.ShapeDtypeStruct(z.shape, F32), jax.ShapeDtypeStruct(s_all.shape, F32)],
        scratch_shapes=[pltpu.VMEM((r, GDN_CONV_CH), F32), pltpu.VMEM((r, GDN_CONV_CH), F32)],
        input_output_aliases=aliases,
        compiler_params=_params("arbitrary"),
        name="gdn_sample",
    )(*args)


def kernel(x_prompt, x_sample, cache_swa_k, cache_swa_v, state_gdn_conv, state_gdn_s, c_prompt, c_sample, norm_g,
           w_mod, b_mod, swa_w_in, swa_sinks, swa_w_out, gdn_w_in, gdn_conv_w, gdn_a_log, gdn_dt_bias, gdn_o_norm_g,
           gdn_w_out, final_norm_g):
    b, t, d = x_prompt.shape
    db, dt, _ = x_sample.shape
    assert d == D_MODEL and t % GDN_TILE == 0 and GDN_CONV - 1 <= dt <= SAMPLE_ROWS
    sr = SAMPLE_ROWS

    mod = _modulation(jnp.concatenate([c_prompt, c_sample], axis=0), w_mod, b_mod)
    xp = x_prompt.reshape(b * t, d)
    xs = jnp.pad(x_sample, ((0, 0), (0, sr - dt), (0, 0))).reshape(db * sr, d)

    gdn_in_t = jnp.transpose(gdn_w_in, (0, 2, 1))
    swa_out = swa_w_out.astype(BF16)
    gdn_out = gdn_w_out.astype(BF16)
    swa_splits = ((0, SWA_Q_W), (SWA_Q_W, SWA_KV_W), (SWA_Q_W + SWA_KV_W, SWA_KV_W), (SWA_Q_W + 2 * SWA_KV_W, SWA_Q_W))
    swa_scales = (SWA_SCALE, 1.0, 1.0, 1.0)
    gdn_splits = ((0, GDN_CONV_CH), (GDN_CONV_CH, GDN_V_W), (GDN_CONV_CH + GDN_V_W, 2 * GDN_V_HEADS))
    gdn_scales = (1.0, 1.0, 1.0)

    n_swa = cache_swa_k.shape[0]
    feature_major = lambda c: jnp.transpose(c, (0, 1, 3, 4, 2)).reshape(n_swa, db, SWA_KV_W, WINDOW)
    position_major = lambda c: jnp.transpose(c.reshape(n_swa, db, SWA_KV_HEADS, SWA_HEAD_DIM, WINDOW), (0, 1, 4, 2, 3))
    ck_all, cv_all = feature_major(cache_swa_k), feature_major(cache_swa_v)

    kp_out, vp_out, cp_out, sp_out, cs_out = [], [], [], [], []
    s_sample = nk_all = nv_all = None
    for layer in range(DEPTH):
        j = layer // 2
        last = layer == DEPTH - 1
        shift, scale, gate = (mod[layer, :, i * d:(i + 1) * d] for i in range(3))
        p3 = lambda a: a[:b].reshape(b, 1, d)
        s_rows = lambda a: a[b:].reshape(db, 1, d)
        g_l = norm_g[layer]
        if layer % 2 == 0:
            q, k, v, z = _inproj(xp, g_l, p3(shift), p3(scale), swa_w_in, j, swa_splits, swa_scales,
                                 (BF16, F32, F32, BF16), t, "swa_inproj_p")
            xp = _swa_prompt(swa_sinks[j], q, k, v, z, xp, p3(gate), swa_out[j], b, t)
            keep = min(WINDOW, t)
            tail = lambda a: a.reshape(b, t, SWA_KV_W)[:, -keep:].reshape(b, keep, SWA_KV_HEADS, SWA_HEAD_DIM)
            kp_out.append(tail(k))
            vp_out.append(tail(v))

            q, k, v, z = _inproj(xs, g_l, s_rows(shift), s_rows(scale), swa_w_in, j, swa_splits, swa_scales,
                                 (F32,) * 4, sr, "swa_inproj_s")
            a, nk_all, nv_all = _swa_sample(swa_sinks[j], q, k, v, z, ck_all, cv_all, nk_all, nv_all, j, dt)
            xs = _outproj(a, swa_out[j], xs, s_rows(gate), final_norm_g, False, "swa_outproj_s")
        else:
            cw, alog, dtb, ong = gdn_conv_w[j], gdn_a_log[j], gdn_dt_bias[j], gdn_o_norm_g[j]
            qkv, z, ab, tail = _gdn_front(xp, g_l, p3(shift), p3(scale), gdn_in_t, j, cw, t)
            s0 = jnp.zeros((b, GDN_V_HEADS, GDN_HEAD_DIM, GDN_HEAD_DIM), F32)
            seq = lambda a: a.reshape(b, t, a.shape[-1])
            xp, s_new = _gdn(seq(qkv), seq(z), seq(ab), seq(xp), p3(gate), s0, alog, dtb, ong, gdn_out[j],
                             final_norm_g, last)
            xp = xp.reshape(b * t, d)
            cp_out.append(tail[:, SUBLANES - (GDN_CONV - 1):])
            sp_out.append(s_new)

            xqkv, z, ab = _inproj(xs, g_l, s_rows(shift), s_rows(scale), gdn_in_t, j, gdn_splits, gdn_scales,
                                  (F32,) * 3, sr, "gdn_inproj_s")
            a, s_sample = _gdn_sample(xqkv, z, ab, state_gdn_conv[j], state_gdn_s, s_sample, j, cw, alog, dtb, ong, dt)
            xs = _outproj(a, gdn_out[j], xs, s_rows(gate), final_norm_g, last, "gdn_outproj_s")
            cs_out.append(xqkv.reshape(db, sr, GDN_CONV_CH)[:, dt - (GDN_CONV - 1):dt])

    y_prompt = xp.reshape(b, t, d)
    y_sample = xs.reshape(db, sr, d)[:, :dt]
    return (y_prompt, y_sample, jnp.stack(kp_out), jnp.stack(vp_out), position_major(nk_all), position_major(nv_all),
            jnp.stack(cp_out), jnp.stack(sp_out), jnp.stack(cs_out), s_sample)
```

```python
import functools
import itertools

import jax
import jax.numpy as jnp
from jax import lax
from jax.experimental import pallas as pl
from jax.experimental.pallas import tpu as pltpu

F32 = jnp.float32
BF16 = jnp.bfloat16

D_MODEL = 1024
DEPTH = 4
EPS = 1e-6
SWA_HEADS = 16
SWA_KV_HEADS = 4
SWA_HEAD_DIM = 64
SWA_GROUP = SWA_HEADS // SWA_KV_HEADS
WINDOW = 128
SWA_Q_W = SWA_HEADS * SWA_HEAD_DIM
SWA_KV_W = SWA_KV_HEADS * SWA_HEAD_DIM
SWA_SCALE = SWA_HEAD_DIM ** -0.5
GDN_QK_HEADS = 8
GDN_V_HEADS = 16
GDN_HEAD_DIM = 128
GDN_K_W = GDN_QK_HEADS * GDN_HEAD_DIM
GDN_V_W = GDN_V_HEADS * GDN_HEAD_DIM
GDN_CONV = 4
GDN_CONV_CH = 2 * GDN_K_W + GDN_V_W
GDN_CHUNK = 64
LANES = 128
SUBLANES = 8
HALF = LANES // 2
GDN_TILE = 2 * GDN_CHUNK
SAMPLE_ROWS = SUBLANES
VMEM_LIMIT = 56 * 1024 * 1024
PROMPT_ROWS = 512
SAMPLE_TILE_ROWS = 256
PROJ_COLS = 512
SWA_SAMPLE_SEQS = 16
GDN_SAMPLE_SEQS = 8
GDN_STEP_TILES = 2

_NT = (((1,), (1,)), ((), ()))


def _dot(a, b):
    return jnp.dot(a.astype(BF16), b.astype(BF16), preferred_element_type=F32)


def _dot_nt(a, b):
    return lax.dot_general(a.astype(BF16), b.astype(BF16), _NT, preferred_element_type=F32)


def _dot_f32(a, b):
    return jnp.dot(a, b, preferred_element_type=F32, precision=lax.Precision.HIGHEST)


def _silu(x):
    return x * jax.nn.sigmoid(x)


def _iota(shape, dim):
    return lax.broadcasted_iota(jnp.int32, shape, dim)


def _rms(x, g):
    return x * lax.rsqrt(jnp.mean(x * x, axis=-1, keepdims=True) + EPS) * g


def _params(*sem):
    return pltpu.CompilerParams(dimension_semantics=sem, vmem_limit_bytes=VMEM_LIMIT)


def _mod_kernel(c_ref, w_ref, b_ref, o_ref):
    o_ref[...] = _dot(_silu(c_ref[...]), w_ref[...]) + b_ref[...]


def _modulation(c_all, w_mod, b_mod):
    r, d = c_all.shape
    n = w_mod.shape[-1]
    tn = PROJ_COLS
    return pl.pallas_call(
        _mod_kernel,
        grid=(DEPTH, n // tn),
        in_specs=[pl.BlockSpec((r, d), lambda l, j: (0, 0)),
                  pl.BlockSpec((None, d, tn), lambda l, j: (l, 0, j)),
                  pl.BlockSpec((None, 1, tn), lambda l, j: (l, 0, j))],
        out_specs=pl.BlockSpec((None, r, tn), lambda l, j: (l, 0, j)),
        out_shape=jax.ShapeDtypeStruct((DEPTH, r, n), F32),
        compiler_params=_params("arbitrary", "arbitrary"),
        name="adaln_modulation",
    )(c_all, w_mod, b_mod.reshape(DEPTH, 1, n))


def _per_row(ref, rows):
    a = ref[...]
    if a.ndim == 2:
        return a
    g, _, d = a.shape
    return jnp.broadcast_to(a, (g, rows // g, d)).reshape(rows, d)


def _inproj_kernel(x_ref, g_ref, sh_ref, sc_ref, w_ref, *o_refs, splits, scales, w_rows_out):
    tm = x_ref.shape[0]
    h = _rms(x_ref[...], g_ref[...]) * (1.0 + _per_row(sc_ref, tm)) + _per_row(sh_ref, tm)
    hb = h.astype(BF16)
    for o_ref, (c0, n), s in zip(o_refs, splits, scales):
        for cc in range(0, n, PROJ_COLS):
            w = min(PROJ_COLS, n - cc)
            if w_rows_out:
                r = _dot_nt(hb, w_ref[c0 + cc:c0 + cc + w, :])
            else:
                r = jnp.dot(hb, w_ref[:, c0 + cc:c0 + cc + w].astype(BF16), preferred_element_type=F32)
            o_ref[:, cc:cc + w] = (r * s if s != 1.0 else r).astype(o_ref.dtype)
        if n < o_ref.shape[1]:
            o_ref[:, n:] = jnp.zeros((o_ref.shape[0], o_ref.shape[1] - n), o_ref.dtype)


def _inproj(x, g, shift, scale, w_all, j, splits, scales, dtypes, rows_per_mod, name):
    m, d = x.shape
    assert w_all.shape[1] != w_all.shape[2] and d in w_all.shape[1:]
    pad = lambda n: -(-n // LANES) * LANES
    tm = min(PROMPT_ROWS, rows_per_mod) if rows_per_mod >= PROMPT_ROWS else min(SAMPLE_TILE_ROWS, m)
    if rows_per_mod >= tm:
        per = rows_per_mod // tm
        mod_spec = pl.BlockSpec((None, 1, d), lambda i: (i // per, 0, 0))
    else:
        mod_spec = pl.BlockSpec((tm // rows_per_mod, 1, d), lambda i: (i, 0, 0))
    return pl.pallas_call(
        functools.partial(_inproj_kernel, splits=splits, scales=scales, w_rows_out=w_all.shape[2] == d),
        grid=(m // tm,),
        in_specs=[pl.BlockSpec((tm, d), lambda i: (i, 0)),
                  pl.BlockSpec((1, d), lambda i: (0, 0)),
                  mod_spec, mod_spec,
                  pl.BlockSpec((None,) + w_all.shape[1:], lambda i: (j, 0, 0), pipeline_mode=pl.Buffered(1))],
        out_specs=[pl.BlockSpec((tm, pad(n)), lambda i: (i, 0)) for _, n in splits],
        out_shape=[jax.ShapeDtypeStruct((m, pad(n)), dtype) for (_, n), dtype in zip(splits, dtypes)],
        compiler_params=_params("arbitrary"),
        name=name,
    )(x, g.reshape(1, d), shift, scale, w_all)


def _gdn_front_kernel(x_ref, g_ref, sh_ref, sc_ref, w_ref, cw_ref, qkv_ref, z_ref, ab_ref, tail_ref, carry, rbuf,
                      *, tiles_per_seq):
    tm = x_ref.shape[0]
    head = GDN_CONV - 1
    base = SUBLANES - head
    width = 4 * LANES

    @pl.when(pl.program_id(0) % tiles_per_seq == 0)
    def _():
        carry[...] = jnp.zeros(carry.shape, F32)

    h = _rms(x_ref[...], g_ref[...]) * (1.0 + sc_ref[...]) + sh_ref[...]
    hb = h.astype(BF16)
    proj = lambda c0: _dot_nt(hb, w_ref[c0:c0 + width, :])
    n_qkv, n_z = GDN_CONV_CH // width, GDN_V_W // width
    r_next = proj(0)
    for cb in range(n_qkv):
        cols = slice(cb * width, (cb + 1) * width)
        r = r_next
        if cb + 1 < n_qkv:
            r_next = proj((cb + 1) * width)
        if cb % 2 == 1:
            zc = cb // 2
            z_ref[:, zc * width:(zc + 1) * width] = proj(GDN_CONV_CH + zc * width).astype(z_ref.dtype)
        rbuf[0:SUBLANES, :] = carry[:, cols]
        rbuf[SUBLANES:SUBLANES + tm, :] = r
        y = rbuf[base:base + tm, :] * cw_ref[0:1, cols]
        for w in range(1, GDN_CONV):
            y = y + rbuf[base + w:base + w + tm, :] * cw_ref[w:w + 1, cols]
        carry[:, cols] = r[tm - SUBLANES:tm]
        tail_ref[:, cols] = r[tm - SUBLANES:tm]
        y = _silu(y)
        if cb * width < 2 * GDN_K_W:
            parts = []
            for p in range(width // GDN_HEAD_DIM):
                blk = y[:, p * GDN_HEAD_DIM:(p + 1) * GDN_HEAD_DIM]
                inv = lax.rsqrt(jnp.sum(blk * blk, axis=-1, keepdims=True) + EPS)
                parts.append(blk * (inv * GDN_HEAD_DIM ** -0.5 if cb * width < GDN_K_W else inv))
            y = jnp.concatenate(parts, axis=1)
        qkv_ref[:, cols] = y.astype(qkv_ref.dtype)
    assert n_z == n_qkv // 2
    nab = 2 * GDN_V_HEADS
    ab_ref[:, 0:nab] = _dot_nt(hb, w_ref[GDN_CONV_CH + GDN_V_W:GDN_CONV_CH + GDN_V_W + nab, :])
    ab_ref[:, nab:] = jnp.zeros((tm, LANES - nab), F32)


def _gdn_front(x, g, shift, scale, w_all_t, j, cw, t):
    m, d = x.shape
    tm = min(PROMPT_ROWS, t)
    per = t // tm
    mod_spec = pl.BlockSpec((None, 1, d), lambda i: (i // per, 0, 0))
    row = lambda w: pl.BlockSpec((tm, w), lambda i: (i, 0))
    const = lambda i: (0, 0)
    return pl.pallas_call(
        functools.partial(_gdn_front_kernel, tiles_per_seq=per),
        grid=(m // tm,),
        in_specs=[row(d), pl.BlockSpec((1, d), const), mod_spec, mod_spec,
                  pl.BlockSpec((None,) + w_all_t.shape[1:], lambda i: (j, 0, 0), pipeline_mode=pl.Buffered(1)),
                  pl.BlockSpec(cw.shape, const)],
        out_specs=[row(GDN_CONV_CH), row(GDN_V_W), row(LANES),
                   pl.BlockSpec((None, SUBLANES, GDN_CONV_CH), lambda i: (i // per, 0, 0))],
        out_shape=[jax.ShapeDtypeStruct((m, GDN_CONV_CH), BF16), jax.ShapeDtypeStruct((m, GDN_V_W), BF16),
                   jax.ShapeDtypeStruct((m, LANES), F32), jax.ShapeDtypeStruct((m // t, SUBLANES, GDN_CONV_CH), F32)],
        scratch_shapes=[pltpu.VMEM((SUBLANES, GDN_CONV_CH), F32), pltpu.VMEM((SUBLANES + tm, 4 * LANES), F32)],
        compiler_params=_params("arbitrary"),
        name="gdn_front_p",
    )(x, g.reshape(1, d), shift, scale, w_all_t, cw)


def _finish(a, w_ref, x, gate, fg_ref, final_norm):
    y = x + gate * _dot(a, w_ref[...])
    return _rms(y, fg_ref[...]) if final_norm else y


def _outproj_kernel(a_ref, w_ref, x_ref, gate_ref, fg_ref, o_ref, *, final_norm):
    gate = _per_row(gate_ref, x_ref.shape[0])
    o_ref[...] = _finish(a_ref[...], w_ref, x_ref[...], gate, fg_ref, final_norm)


def _outproj(a, w, x, gate, fg, final_norm, name):
    m, k = a.shape
    d = x.shape[-1]
    tm = min(PROMPT_ROWS, m)
    return pl.pallas_call(
        functools.partial(_outproj_kernel, final_norm=final_norm),
        grid=(m // tm,),
        in_specs=[pl.BlockSpec((tm, k), lambda i: (i, 0)),
                  pl.BlockSpec((k, d), lambda i: (0, 0)),
                  pl.BlockSpec((tm, d), lambda i: (i, 0)),
                  pl.BlockSpec((tm // SAMPLE_ROWS, 1, d), lambda i: (i, 0, 0)),
                  pl.BlockSpec((1, d), lambda i: (0, 0))],
        out_specs=pl.BlockSpec((tm, d), lambda i: (i, 0)),
        out_shape=jax.ShapeDtypeStruct((m, d), F32),
        compiler_params=_params("arbitrary"),
        name=name,
    )(a, w, x, gate, fg.reshape(1, d))


def _swa_block(q, kcat, vcat, mask4, sinks_ref):
    nq = q.shape[0]
    lo = _iota((1, LANES), 1) < HALF
    hi = jnp.logical_not(lo)
    row = _iota((SWA_GROUP * nq, 1), 0)
    kvs = range(SWA_KV_HEADS)
    blk = lambda a, pb: a[:, pb * LANES:(pb + 1) * LANES]
    q4s = []
    for kh in kvs:
        parts = []
        for g in range(SWA_GROUP):
            qg = blk(q, 2 * kh + g // 2)
            if g % 2 != kh % 2:
                qg = pltpu.roll(qg, HALF, axis=1)
            parts.append(jnp.where(lo if kh % 2 == 0 else hi, qg, 0.0))
        q4s.append(jnp.concatenate(parts, axis=0))
    logits = [_dot_nt(q4, blk(kcat, kh // 2)) for kh, q4 in zip(kvs, q4s)]
    es, invs = [], []
    for kh, lg in zip(kvs, logits):
        sink = jnp.full((SWA_GROUP * nq, 1), sinks_ref[SWA_GROUP * kh], F32)
        for g in range(1, SWA_GROUP):
            sink = jnp.where(row >= g * nq, sinks_ref[SWA_GROUP * kh + g], sink)
        lg = jnp.where(mask4, lg, -jnp.inf)
        m = jnp.maximum(jnp.max(lg, axis=-1, keepdims=True), sink)
        e = jnp.exp(lg - m)
        es.append(e)
        invs.append(1.0 / (jnp.sum(e, axis=-1, keepdims=True) + jnp.exp(sink - m)))
    o4s = [_dot(e, blk(vcat, kh // 2)) * inv for kh, e, inv in zip(kvs, es, invs)]
    outs = [None] * (SWA_HEADS // 2)
    for kh, o4 in zip(kvs, o4s):
        for g in range(SWA_GROUP):
            piece = o4[g * nq:(g + 1) * nq]
            if g % 2 != kh % 2:
                piece = pltpu.roll(piece, HALF, axis=1)
            piece = jnp.where(lo if g % 2 == 0 else hi, piece, 0.0)
            qb = 2 * kh + g // 2
            outs[qb] = piece if outs[qb] is None else outs[qb] + piece
    return jnp.concatenate(outs, axis=1)


def _band_mask(nq, nk, jmin):
    i = _iota((SWA_GROUP * nq, nk), 0) % nq
    j = _iota((SWA_GROUP * nq, nk), 1)
    return (j >= i) & (j <= i + WINDOW) & (j >= jmin)


def _swa_prompt_kernel(sinks_ref, q_ref, kp_ref, vp_ref, k_ref, v_ref, z_ref, x_ref, gate_ref, w_ref,
                       o_ref, kbuf, vbuf, abuf, *, tq):
    t = pl.program_id(1)
    kbuf[0:WINDOW] = kp_ref[...]
    kbuf[WINDOW:WINDOW + tq] = k_ref[...]
    vbuf[0:WINDOW] = vp_ref[...]
    vbuf[WINDOW:WINDOW + tq] = v_ref[...]

    def body(sb, carry):
        r = pl.multiple_of(sb * WINDOW, WINDOW)
        jmin = jnp.where((t == 0) & (sb == 0), WINDOW, 0)
        mask4 = _band_mask(WINDOW, 2 * WINDOW, jmin)
        abuf[pl.ds(r, WINDOW), :] = _swa_block(q_ref[pl.ds(r, WINDOW), :].astype(F32), kbuf[pl.ds(r, 2 * WINDOW), :],
                                               vbuf[pl.ds(r, 2 * WINDOW), :], mask4, sinks_ref)
        return carry

    lax.fori_loop(0, tq // WINDOW, body, 0)
    a = abuf[...] * _silu(z_ref[...].astype(F32))
    o_ref[...] = x_ref[...] + gate_ref[...] * _dot(a, w_ref[...])


def _swa_prompt(sinks, q, k, v, z, x, gate, w_out, b, t):
    d = x.shape[-1]
    tq = min(PROMPT_ROWS, t)
    nt = t // tq
    sub = tq // WINDOW
    row = lambda bi, ti: (bi * nt + ti, 0)
    prev = lambda bi, ti: (bi * nt * sub + jnp.maximum(ti * sub - 1, 0), 0)
    const = lambda bi, ti: (0, 0)
    return pl.pallas_call(
        functools.partial(_swa_prompt_kernel, tq=tq),
        grid=(b, nt),
        in_specs=[pl.BlockSpec(memory_space=pltpu.SMEM),
                  pl.BlockSpec((tq, SWA_Q_W), row),
                  pl.BlockSpec((WINDOW, SWA_KV_W), prev),
                  pl.BlockSpec((WINDOW, SWA_KV_W), prev),
                  pl.BlockSpec((tq, SWA_KV_W), row),
                  pl.BlockSpec((tq, SWA_KV_W), row),
                  pl.BlockSpec((tq, SWA_Q_W), row),
                  pl.BlockSpec((tq, d), row),
                  pl.BlockSpec((None, 1, d), lambda bi, ti: (bi, 0, 0)),
                  pl.BlockSpec(w_out.shape, const)],
        out_specs=pl.BlockSpec((tq, d), row),
        out_shape=jax.ShapeDtypeStruct(x.shape, F32),
        scratch_shapes=[pltpu.VMEM((WINDOW + tq, SWA_KV_W), F32),
                        pltpu.VMEM((WINDOW + tq, SWA_KV_W), F32),
                        pltpu.VMEM((tq, SWA_Q_W), F32)],
        compiler_params=_params("arbitrary", "arbitrary"),
        name="swa_prompt",
    )(sinks, q, k, v, k, v, z, x, gate, w_out)


def _swa_sample_kernel(sinks_ref, q_ref, k_ref, v_ref, z_ref, ck_ref, cv_ref, *rest, bb, dt, has_prev):
    a_ref, nk_ref, nv_ref = rest[2:] if has_prev else rest
    nq, grp = SAMPLE_ROWS, SWA_GROUP
    lane = _iota((1, LANES), 1)
    lo = lane < HALF
    hi = jnp.logical_not(lo)
    row = _iota((grp * nq, 1), 0)
    qi = row % nq
    see_cache, see_own = lane >= qi, lane <= qi
    zpad = jnp.zeros((WINDOW - nq, LANES), F32)
    rows = lambda bi: slice(bi * nq, (bi + 1) * nq)
    blk = lambda pb: slice(pb * LANES, (pb + 1) * LANES)
    units = [(bi, kh) for bi in range(bb) for kh in range(SWA_KV_HEADS)]

    q4s = []
    for bi, kh in units:
        keep = lo if kh % 2 == 0 else hi
        parts = []
        for g in range(grp):
            qg = q_ref[rows(bi), blk(2 * kh + g // 2)]
            if g % 2 != kh % 2:
                qg = pltpu.roll(qg, HALF, axis=1)
            parts.append(jnp.where(keep, qg, 0.0))
        q4s.append(jnp.concatenate(parts, axis=0))
    own = lambda ref, bi, pb: jnp.concatenate([ref[rows(bi), blk(pb)], zpad], axis=0)
    lcs = [_dot(q4, ck_ref[bi, blk(kh // 2), :]) for (bi, kh), q4 in zip(units, q4s)]
    los = [_dot_nt(q4, own(k_ref, bi, kh // 2)) for (bi, kh), q4 in zip(units, q4s)]
    ecs, eos, invs = [], [], []
    for (bi, kh), lc, lw in zip(units, lcs, los):
        sink = jnp.full((grp * nq, 1), sinks_ref[grp * kh], F32)
        for g in range(1, grp):
            sink = jnp.where(row >= g * nq, sinks_ref[grp * kh + g], sink)
        lc = jnp.where(see_cache, lc, -jnp.inf)
        lw = jnp.where(see_own, lw, -jnp.inf)
        m = jnp.maximum(jnp.maximum(jnp.max(lc, axis=-1, keepdims=True), jnp.max(lw, axis=-1, keepdims=True)), sink)
        ec, eo = jnp.exp(lc - m), jnp.exp(lw - m)
        den = jnp.sum(ec, axis=-1, keepdims=True) + jnp.sum(eo, axis=-1, keepdims=True) + jnp.exp(sink - m)
        ecs.append(ec)
        eos.append(eo)
        invs.append(1.0 / den)
    o4s = [(_dot_nt(ec, cv_ref[bi, blk(kh // 2), :]) + _dot(eo, own(v_ref, bi, kh // 2))) * inv
           for (bi, kh), ec, eo, inv in zip(units, ecs, eos, invs)]
    for bi in range(bb):
        outs = [None] * (SWA_HEADS // 2)
        for kh in range(SWA_KV_HEADS):
            o4 = o4s[bi * SWA_KV_HEADS + kh]
            for g in range(grp):
                piece = o4[g * nq:(g + 1) * nq]
                if g % 2 != kh % 2:
                    piece = pltpu.roll(piece, HALF, axis=1)
                piece = jnp.where(lo if g % 2 == 0 else hi, piece, 0.0)
                qb = 2 * kh + g // 2
                outs[qb] = piece if outs[qb] is None else outs[qb] + piece
        a_ref[rows(bi), :] = jnp.concatenate(outs, axis=1) * _silu(z_ref[rows(bi), :])

    fresh = lane >= WINDOW - dt
    for src, cache, dst in ((k_ref, ck_ref, nk_ref), (v_ref, cv_ref, nv_ref)):
        for pb in range(SWA_KV_W // LANES):
            new = src[:, blk(pb)]
            if bb * nq < LANES:
                new = jnp.concatenate([new, jnp.zeros((LANES - bb * nq, LANES), F32)], axis=0)
            new_t = new.T
            for bi in range(bb):
                cols = pltpu.roll(new_t, (WINDOW - dt - bi * nq) % LANES, axis=1)
                dst[bi, blk(pb), :] = jnp.where(fresh, cols, pltpu.roll(cache[bi, blk(pb), :], WINDOW - dt, axis=1))


def _swa_sample(sinks, q, k, v, z, ck_all, cv_all, nk_prev, nv_prev, j, dt):
    db = ck_all.shape[1]
    bb = min(SWA_SAMPLE_SEQS, db)
    nq = SAMPLE_ROWS
    assert 2 * dt <= nq
    row = lambda i: (i, 0)
    c_spec = pl.BlockSpec((None, bb, SWA_KV_W, WINDOW), lambda i: (j, i, 0, 0))
    args = [sinks, q, k, v, z, ck_all, cv_all]
    in_specs = [pl.BlockSpec(memory_space=pltpu.SMEM),
                pl.BlockSpec((bb * nq, SWA_Q_W), row),
                pl.BlockSpec((bb * nq, SWA_KV_W), row),
                pl.BlockSpec((bb * nq, SWA_KV_W), row),
                pl.BlockSpec((bb * nq, SWA_Q_W), row),
                c_spec, c_spec]
    aliases = {}
    if nk_prev is not None:
        args += [nk_prev, nv_prev]
        in_specs += [pl.BlockSpec(memory_space=pl.ANY)] * 2
        aliases = {len(args) - 2: 1, len(args) - 1: 2}
    return pl.pallas_call(
        functools.partial(_swa_sample_kernel, bb=bb, dt=dt, has_prev=nk_prev is not None),
        grid=(db // bb,),
        in_specs=in_specs,
        out_specs=[pl.BlockSpec((bb * nq, SWA_Q_W), row), c_spec, c_spec],
        out_shape=[jax.ShapeDtypeStruct(q.shape, F32),
                   jax.ShapeDtypeStruct(ck_all.shape, F32),
                   jax.ShapeDtypeStruct(cv_all.shape, F32)],
        input_output_aliases=aliases,
        compiler_params=_params("arbitrary"),
        name="swa_sample",
    )(*args)


def _block_mul(a, b):
    size, width = b.shape
    b16 = b.astype(BF16)
    zero = jnp.zeros_like(b16)
    blk = _iota((1, width), 1) // size
    return _dot(a, jnp.concatenate([jnp.where(blk == r, b16, zero) for r in range(width // size)], axis=0))


def _diag2(a, b):
    return jnp.concatenate([jnp.concatenate([a, jnp.zeros_like(b)], axis=1),
                            jnp.concatenate([jnp.zeros_like(a), b], axis=1)], axis=0)


def _gdn_kernel(qkv, z_ref, ab_ref, x_ref, gate_ref, s0_ref, alog_ref, dtb_ref, ong_ref, w_ref, fg_ref, o_ref, s_ref,
                obuf, *, nb, ns, final_norm):
    nchunks = GDN_TILE // GDN_CHUNK
    R, C, HD = GDN_TILE, GDN_CHUNK, GDN_HEAD_DIM
    t = pl.program_id(1)
    seqs = range(nb)
    nu = nb * ns
    seq = lambda u: u // ns
    off = lambda u: (u % ns) * R

    @pl.when(t == 0)
    def _():
        s_ref[...] = s0_ref[...]

    ri, ci = _iota((R, R), 0), _iota((R, R), 1)
    same = (ri // C) == (ci // C)
    tri, ones = jnp.where(same & (ci <= ri), 1.0, 0.0), jnp.where(same, 1.0, 0.0)
    gcs, betas, egcs, bgcs, gcTs, glTs, dlTs = ([None] * nu for _ in range(7))

    def prologue(sq):
        ab = ab_ref[seq(sq), off(sq):off(sq) + R]
        g = -jnp.exp(alog_ref[...]) * jax.nn.softplus(ab + dtb_ref[...])
        beta = pltpu.roll(jax.nn.sigmoid(ab), LANES - GDN_V_HEADS, axis=1)
        gc = _dot_f32(tri, g)
        gl = _dot_f32(ones, g)
        egc = jnp.exp(gc)
        gcT, glT = gc.T, gl.T
        gcs[sq], betas[sq], egcs[sq], bgcs[sq] = gc, beta, egc, beta * egc
        gcTs[sq], glTs[sq], dlTs[sq] = gcT, glT, glT - gcT
        yield

    lane = _iota((1, LANES), 1)
    lo = lane < HALF
    jj = lane % C
    ii = _iota((C, 1), 0)
    incl, strict = ii >= jj, ii > jj
    zc = jnp.zeros((C, HD), F32)

    def col2(arr, r0, h0):
        return jnp.where(lo, arr[r0:r0 + C, h0:h0 + 1], arr[r0:r0 + C, h0 + 1:h0 + 2])

    def pick(c, row_a, row_b):
        if c == 0:
            return jnp.where(lo, row_a, pltpu.roll(row_b, HALF, axis=1))
        return jnp.where(lo, pltpu.roll(row_a, HALF, axis=1), row_b)

    pairs = range(GDN_QK_HEADS)
    heads = range(GDN_V_HEADS)
    units = [(c, p) for c in range(nchunks) for p in pairs]
    quads = [(c, pp) for c in range(nchunks) for pp in range(GDN_QK_HEADS // 2)]
    rows = lambda sq, c: slice(off(sq) + c * C, off(sq) + (c + 1) * C)
    qsl = lambda sq, c, p: qkv[seq(sq), rows(sq, c), p * HD:(p + 1) * HD]
    ksl = lambda sq, c, p: qkv[seq(sq), rows(sq, c), GDN_K_W + p * HD:GDN_K_W + (p + 1) * HD]
    vsl = lambda sq, c, h: qkv[seq(sq), rows(sq, c), 2 * GDN_K_W + h * HD:2 * GDN_K_W + (h + 1) * HD]
    col = lambda arr, c, h: arr[c * C:(c + 1) * C, h:h + 1]
    jq = _iota((1, 2 * LANES), 1) % C
    kts, lows, qks, sols = ([None] * nu for _ in range(4))

    def products(sq):
        kts[sq] = [qkv[seq(sq), off(sq):off(sq) + R, GDN_K_W + p * HD:GDN_K_W + (p + 1) * HD].astype(F32).T
                   for p in pairs]
        yield
        ms = [_dot_nt(jnp.concatenate([ksl(sq, c, p), qsl(sq, c, p)], axis=0), jnp.concatenate([ksl(sq, c, p)] * 2, axis=0))
              for c, p in units]
        yield
        lows[sq], qks[sq] = [], []
        for (c, p), m in zip(units, ms):
            h0 = 2 * p
            row_gc = pick(c, gcTs[sq][h0:h0 + 1, :], gcTs[sq][h0 + 1:h0 + 2, :])
            decay = jnp.exp(jnp.where(incl, col2(gcs[sq], c * C, h0) - row_gc, -jnp.inf))
            lows[sq].append(jnp.where(strict, m[0:C] * decay, 0.0) * col2(betas[sq], c * C, h0))
            qks[sq].append(m[C:2 * C] * decay)
            if p % 4 == 3:
                yield

    def inverse(sq):
        low4 = [jnp.concatenate([lows[sq][c * GDN_QK_HEADS + 2 * pp], lows[sq][c * GDN_QK_HEADS + 2 * pp + 1]], axis=1)
                for c, pp in quads]
        invs = [jnp.where(ii == jq, 1.0, 0.0) - jnp.where((ii // 2) == (jq // 2), low, 0.0) for low in low4]
        yield
        s = 2
        while s < C:
            below = ((ii // s) % 2 == 1) & ((jq // s) == (ii // s) - 1)
            eps, new = [], []
            for i, (low, inv) in enumerate(zip(low4, invs)):
                eps.append(_block_mul(jnp.where(below, low, 0.0), inv))
                if i % 2:
                    yield
            for i, (inv, ep) in enumerate(zip(invs, eps)):
                new.append(inv - _block_mul(inv, ep))
                if i % 2:
                    yield
            invs = new
            s *= 2
        sols[sq] = []
        for (c, pp), inv in zip(quads, invs):
            rhs = []
            for idx in range(4):
                h = 4 * pp + idx
                piece = [vsl(sq, c, h) * col(betas[sq], c, h), ksl(sq, c, h // 2) * col(bgcs[sq], c, h)]
                rhs.append(jnp.concatenate([zc] * (2 * idx) + piece + [zc] * (6 - 2 * idx), axis=1))
            sols[sq].append(_dot(inv, jnp.concatenate(rhs, axis=0)))
            if pp % 2 == 1:
                yield

    def state(sq):
        for c in range(nchunks):
            in_chunk = (lane // C) == c
            quad = lambda h: sols[sq][c * (GDN_QK_HEADS // 2) + h // 4]
            w_v = lambda h: quad(h)[:, 2 * (h % 4) * HD:(2 * (h % 4) + 1) * HD]
            w_k = lambda h: quad(h)[:, (2 * (h % 4) + 1) * HD:(2 * (h % 4) + 2) * HD]
            a_s = []
            for p in pairs:
                top = jnp.concatenate([w_k(2 * p), w_k(2 * p + 1)], axis=1)
                bot = jnp.concatenate([qsl(sq, c, p) * col(egcs[sq], c, 2 * p), qsl(sq, c, p) * col(egcs[sq], c, 2 * p + 1)],
                                      axis=1)
                a_s.append(_dot(jnp.concatenate([top, bot], axis=0),
                                _diag2(s_ref[seq(sq), 2 * p], s_ref[seq(sq), 2 * p + 1])))
            yield
            us = [[w_v(2 * p + idx) - a[0:C, idx * HD:(idx + 1) * HD] for idx in range(2)] for p, a in zip(pairs, a_s)]
            for p, a, u in zip(pairs, a_s, us):
                obuf[seq(sq), rows(sq, c), 2 * p * HD:(2 * p + 2) * HD] = (
                    a[C:2 * C] + _dot(qks[sq][c * GDN_QK_HEADS + p], _diag2(*u)))
            yield
            for p, u in zip(pairs, us):
                kd, ue, gt = [], [], []
                for idx in range(2):
                    h = 2 * p + idx
                    kd.append(kts[sq][p] * jnp.exp(jnp.where(in_chunk, dlTs[sq][h:h + 1, :], -jnp.inf)))
                    ue.append(jnp.concatenate([u[idx], zc] if c == 0 else [zc, u[idx]], axis=0))
                    gt.append(jnp.exp(pick(c, glTs[sq][h:h + 1, :], glTs[sq][h:h + 1, :])))
                upd = _dot(jnp.concatenate(kd, axis=1), _diag2(*ue))
                for idx in range(2):
                    h = 2 * p + idx
                    s_ref[seq(sq), h] = s_ref[seq(sq), h] * gt[idx] + upd[:, idx * HD:(idx + 1) * HD]
                if p % 4 == 3:
                    yield

    def run(*gens):
        gens = list(gens)
        while gens:
            for gen in list(gens):
                if next(gen, "done") == "done":
                    gens.remove(gen)

    for k in range(nu + 2):
        live = [itertools.chain(prologue(k), products(k))] if k < nu else []
        live += [inverse(k - 1)] if 0 <= k - 1 < nu else []
        live += [state(k - 2)] if 0 <= k - 2 < nu else []
        run(*live)

    gated = []
    for sq in seqs:
        o = obuf[sq]
        parts = [_rms(o[:, h * HD:(h + 1) * HD], ong_ref[...]) for h in heads]
        gated.append(jnp.concatenate(parts, axis=1) * _silu(z_ref[sq].astype(F32)))
    out = _dot(jnp.concatenate(gated, axis=0), w_ref[...])
    for sq in seqs:
        y = x_ref[sq] + gate_ref[sq] * out[sq * ns * R:(sq + 1) * ns * R]
        o_ref[sq] = _rms(y, fg_ref[...]) if final_norm else y


def _gdn(qkv, z, ab, x, gate, s0, alog, dtb, ong, w_out, fg, final_norm):
    b, t, d = x.shape
    nb = 2 if b % 2 == 0 else 1
    ns = GDN_STEP_TILES if t % (GDN_STEP_TILES * GDN_TILE) == 0 else 1
    rows_in = ns * GDN_TILE
    tile = lambda w: pl.BlockSpec((nb, rows_in, w), lambda bi, ti: (bi, ti, 0))
    const = lambda bi, ti: (0, 0)
    per_b3 = lambda bi, ti: (bi, 0, 0)
    per_b4 = lambda bi, ti: (bi, 0, 0, 0)
    vec = lambda a: jnp.pad(a.astype(F32), (0, LANES - a.shape[0])).reshape(1, LANES)
    state = (GDN_V_HEADS, GDN_HEAD_DIM, GDN_HEAD_DIM)
    return pl.pallas_call(
        functools.partial(_gdn_kernel, nb=nb, ns=ns, final_norm=final_norm),
        grid=(b // nb, t // rows_in),
        in_specs=[tile(GDN_CONV_CH), tile(GDN_V_W), tile(LANES), tile(d),
                  pl.BlockSpec((nb, 1, d), per_b3),
                  pl.BlockSpec((nb,) + state, per_b4),
                  pl.BlockSpec((1, LANES), const),
                  pl.BlockSpec((1, LANES), const),
                  pl.BlockSpec((1, GDN_HEAD_DIM), const),
                  pl.BlockSpec(w_out.shape, const),
                  pl.BlockSpec((1, d), const)],
        out_specs=[tile(d), pl.BlockSpec((nb,) + state, per_b4)],
        out_shape=[jax.ShapeDtypeStruct(x.shape, F32),
                   jax.ShapeDtypeStruct((b,) + state, F32)],
        scratch_shapes=[pltpu.VMEM((nb, rows_in, GDN_V_W), F32)],
        compiler_params=_params("arbitrary", "arbitrary"),
        name="gdn_prompt",
    )(qkv, z, ab, x, gate, s0, vec(alog), vec(dtb), ong.reshape(1, GDN_HEAD_DIM), w_out, fg.reshape(1, d))


def _tile_roll(a, d):
    r, w = a.shape
    return pltpu.roll(a.reshape(r // SUBLANES, SUBLANES, w), d, axis=1).reshape(r, w)


def _gdn_sample_kernel(xqkv_ref, z_ref, ab_ref, conv_ref, s_ref, cw_ref, alog_ref, dtb_ref, ong_ref, *rest,
                       bb, dt, has_prev):
    a_ref, so_ref, st8, qkv = rest[1:] if has_prev else rest
    R, HD = bb * SUBLANES, GDN_HEAD_DIM
    tix = _iota((R, 1), 0) % SUBLANES
    live = tix < dt
    head = GDN_CONV - 1
    lane = _iota((1, LANES), 1)

    st8[...] = jnp.zeros(st8.shape, F32)
    for bi in range(bb):
        st8[(bi + 1) * SUBLANES - head:(bi + 1) * SUBLANES, :] = conv_ref[bi]
    for cb in range(GDN_CONV_CH // GDN_K_W):
        cols = slice(cb * GDN_K_W, (cb + 1) * GDN_K_W)
        x, st = xqkv_ref[:, cols], st8[:, cols]
        y = x * cw_ref[head:head + 1, cols]
        for d in range(1, GDN_CONV):
            y = y + jnp.where(tix >= d, _tile_roll(x, d), _tile_roll(st, d)) * cw_ref[head - d:head - d + 1, cols]
        y = _silu(y)
        if cb < 2:
            for p in range(GDN_QK_HEADS):
                blk = y[:, p * HD:(p + 1) * HD]
                inv = lax.rsqrt(jnp.sum(blk * blk, axis=-1, keepdims=True) + EPS)
                if cb == 0:
                    inv = inv * HD ** -0.5
                qkv[:, cb * GDN_K_W + p * HD:cb * GDN_K_W + (p + 1) * HD] = blk * inv
        else:
            qkv[:, cols] = y

    ab = ab_ref[...]
    g = jnp.where(live, -jnp.exp(alog_ref[...]) * jax.nn.softplus(ab + dtb_ref[...]), 0.0)
    beta = jnp.where(live, pltpu.roll(jax.nn.sigmoid(ab), LANES - GDN_V_HEADS, axis=1), 0.0)
    gc, gl = g, g
    for d in range(1, SUBLANES):
        gl = gl + _tile_roll(g, d)
        if d < dt:
            gc = gc + jnp.where(tix >= d, _tile_roll(g, d), 0.0)
    egc = jnp.exp(gc)
    bgc = beta * egc
    ekl = jnp.where(live, jnp.exp(jnp.where(live, gl - gc, 0.0)), 0.0)
    gtot = jnp.exp(gl)
    dec = [None] + [jnp.where(tix >= d, jnp.exp(jnp.where(tix >= d, gc - _tile_roll(gc, d), 0.0)), 0.0)
                    for d in range(1, dt)]

    zero = jnp.zeros((R, LANES), F32)
    kk, qk = [zero] * dt, [zero] * dt
    qsl = lambda p: qkv[:, p * HD:(p + 1) * HD]
    ksl = lambda p: qkv[:, GDN_K_W + p * HD:GDN_K_W + (p + 1) * HD]
    vsl = lambda h: qkv[:, 2 * GDN_K_W + h * HD:2 * GDN_K_W + (h + 1) * HD]
    for p in range(GDN_QK_HEADS):
        mine = (lane // 2) == p
        for d in range(dt):
            ks = ksl(p) if d == 0 else _tile_roll(ksl(p), d)
            qk[d] = jnp.where(mine, jnp.sum(qsl(p) * ks, axis=-1, keepdims=True), qk[d])
            if d:
                kk[d] = jnp.where(mine, jnp.sum(ksl(p) * ks, axis=-1, keepdims=True), kk[d])
    low = [None] + [beta * kk[d] * dec[d] for d in range(1, dt)]
    qkd = [qk[0]] + [qk[d] * dec[d] for d in range(1, dt)]
    l1s1, l1s2 = _tile_roll(low[1], 1), _tile_roll(low[1], 2)
    tinv = [None, -low[1], -low[2] + low[1] * l1s1,
            -low[3] + low[1] * _tile_roll(low[2], 1) + low[2] * l1s2 - low[1] * l1s1 * l1s2]
    heads = range(GDN_V_HEADS)
    spread = lambda arr: [jnp.broadcast_to(arr[:, h:h + 1], (R, LANES)) for h in heads]
    s_beta, s_bgc, s_egc, s_ekl, s_gtot = spread(beta), spread(bgc), spread(egc), spread(ekl), spread(gtot)
    s_tinv = [None] + [spread(tinv[d]) for d in range(1, dt)]
    s_qkd = [spread(qkd[d]) for d in range(dt)]

    xvs, m8s = [], []
    for h in heads:
        p = h // 2
        rv, rk = vsl(h) * s_beta[h], ksl(p) * s_bgc[h]
        xv, xk = rv, rk
        for d in range(1, dt):
            xv = xv + s_tinv[d][h] * _tile_roll(rv, d)
            xk = xk + s_tinv[d][h] * _tile_roll(rk, d)
        xvs.append(xv)
        m8s.append(jnp.where(live, xk, _tile_roll(qsl(p) * s_egc[h], dt)))
    tile = lambda a, bi: a[bi * SUBLANES:(bi + 1) * SUBLANES]
    aws = [jnp.concatenate([_dot(tile(m8s[h], bi), s_ref[bi, h]) for bi in range(bb)], axis=0) for h in heads]
    outs, uds = [], []
    for h in heads:
        u = jnp.where(live, xvs[h] - aws[h], 0.0)
        o = _tile_roll(aws[h], SUBLANES - dt) + s_qkd[0][h] * u
        for d in range(1, dt):
            o = o + s_qkd[d][h] * _tile_roll(u, d)
        outs.append(_rms(o, ong_ref[...]))
        uds.append(u * s_ekl[h])
    a_ref[...] = jnp.concatenate(outs, axis=1) * _silu(z_ref[...])
    tn = (((0,), (0,)), ((), ()))
    for h in heads:
        kp = ksl(h // 2)
        for bi in range(bb):
            upd = lax.dot_general(tile(kp, bi).astype(BF16), tile(uds[h], bi).astype(BF16), tn,
                                  preferred_element_type=F32)
            so_ref[bi, h] = s_ref[bi, h] * s_gtot[h][bi * SUBLANES:bi * SUBLANES + 1] + upd


def _gdn_sample(xqkv, z, ab, conv, s_all, s_prev, j, cw, alog, dtb, ong, dt):
    db = conv.shape[0]
    bb = min(GDN_SAMPLE_SEQS, db)
    r = bb * SUBLANES
    assert 2 * dt <= SUBLANES and dt == GDN_CONV
    row = lambda i: (i, 0)
    const = lambda i: (0, 0)
    state = (GDN_V_HEADS, GDN_HEAD_DIM, GDN_HEAD_DIM)
    s_spec = pl.BlockSpec((None, bb) + state, lambda i: (j, i, 0, 0, 0))
    vec = lambda a: jnp.pad(a.astype(F32), (0, LANES - a.shape[0])).reshape(1, LANES)
    args = [xqkv, z, ab, conv, s_all, cw, vec(alog), vec(dtb), ong.reshape(1, GDN_HEAD_DIM)]
    in_specs = [pl.BlockSpec((r, GDN_CONV_CH), row),
                pl.BlockSpec((r, GDN_V_W), row),
                pl.BlockSpec((r, LANES), row),
                pl.BlockSpec((bb, GDN_CONV - 1, GDN_CONV_CH), lambda i: (i, 0, 0)),
                s_spec,
                pl.BlockSpec(cw.shape, const),
                pl.BlockSpec((1, LANES), const),
                pl.BlockSpec((1, LANES), const),
                pl.BlockSpec((1, GDN_HEAD_DIM), const)]
    aliases = {}
    if s_prev is not None:
        args.append(s_prev)
        in_specs.append(pl.BlockSpec(memory_space=pl.ANY))
        aliases = {len(args) - 1: 1}
    return pl.pallas_call(
        functools.partial(_gdn_sample_kernel, bb=bb, dt=dt, has_prev=s_prev is not None),
        grid=(db // bb,),
        in_specs=in_specs,
        out_specs=[pl.BlockSpec((r, GDN_V_W), row), s_spec],
        out_shape=[jax.ShapeDtypeStruct(z.shape, F32), jax.ShapeDtypeStruct(s_all.shape, F32)],
        scratch_shapes=[pltpu.VMEM((r, GDN_CONV_CH), F32), pltpu.VMEM((r, GDN_CONV_CH), F32)],
        input_output_aliases=aliases,
        compiler_params=_params("arbitrary"),
        name="gdn_sample",
    )(*args)


def kernel(x_prompt, x_sample, cache_swa_k, cache_swa_v, state_gdn_conv, state_gdn_s, c_prompt, c_sample, norm_g,
           w_mod, b_mod, swa_w_in, swa_sinks, swa_w_out, gdn_w_in, gdn_conv_w, gdn_a_log, gdn_dt_bias, gdn_o_norm_g,
           gdn_w_out, final_norm_g):
    b, t, d = x_prompt.shape
    db, dt, _ = x_sample.shape
    assert d == D_MODEL and t % GDN_TILE == 0 and GDN_CONV - 1 <= dt <= SAMPLE_ROWS
    sr = SAMPLE_ROWS

    mod = _modulation(jnp.concatenate([c_prompt, c_sample], axis=0), w_mod, b_mod)
    xp = x_prompt.reshape(b * t, d)
    xs = jnp.pad(x_sample, ((0, 0), (0, sr - dt), (0, 0))).reshape(db * sr, d)

    gdn_in_t = jnp.transpose(gdn_w_in, (0, 2, 1)).astype(BF16)
    swa_out = swa_w_out.astype(BF16)
    gdn_out = gdn_w_out.astype(BF16)
    swa_splits = ((0, SWA_Q_W), (SWA_Q_W, SWA_KV_W), (SWA_Q_W + SWA_KV_W, SWA_KV_W), (SWA_Q_W + 2 * SWA_KV_W, SWA_Q_W))
    swa_scales = (SWA_SCALE, 1.0, 1.0, 1.0)
    gdn_splits = ((0, GDN_CONV_CH), (GDN_CONV_CH, GDN_V_W), (GDN_CONV_CH + GDN_V_W, 2 * GDN_V_HEADS))
    gdn_scales = (1.0, 1.0, 1.0)

    n_swa = cache_swa_k.shape[0]
    feature_major = lambda c: jnp.transpose(c, (0, 1, 3, 4, 2)).reshape(n_swa, db, SWA_KV_W, WINDOW)
    position_major = lambda c: jnp.transpose(c.reshape(n_swa, db, SWA_KV_HEADS, SWA_HEAD_DIM, WINDOW), (0, 1, 4, 2, 3))
    ck_all, cv_all = feature_major(cache_swa_k), feature_major(cache_swa_v)

    kp_out, vp_out, cp_out, sp_out, cs_out = [], [], [], [], []
    s_sample = nk_all = nv_all = None
    for layer in range(DEPTH):
        j = layer // 2
        last = layer == DEPTH - 1
        shift, scale, gate = (mod[layer, :, i * d:(i + 1) * d] for i in range(3))
        p3 = lambda a: a[:b].reshape(b, 1, d)
        s_rows = lambda a: a[b:].reshape(db, 1, d)
        g_l = norm_g[layer]
        if layer % 2 == 0:
            q, k, v, z = _inproj(xp, g_l, p3(shift), p3(scale), swa_w_in, j, swa_splits, swa_scales,
                                 (BF16, F32, F32, BF16), t, "swa_inproj_p")
            xp = _swa_prompt(swa_sinks[j], q, k, v, z, xp, p3(gate), swa_out[j], b, t)
            keep = min(WINDOW, t)
            tail = lambda a: a.reshape(b, t, SWA_KV_W)[:, -keep:].reshape(b, keep, SWA_KV_HEADS, SWA_HEAD_DIM)
            kp_out.append(tail(k))
            vp_out.append(tail(v))

            q, k, v, z = _inproj(xs, g_l, s_rows(shift), s_rows(scale), swa_w_in, j, swa_splits, swa_scales,
                                 (F32,) * 4, sr, "swa_inproj_s")
            a, nk_all, nv_all = _swa_sample(swa_sinks[j], q, k, v, z, ck_all, cv_all, nk_all, nv_all, j, dt)
            xs = _outproj(a, swa_out[j], xs, s_rows(gate), final_norm_g, False, "swa_outproj_s")
        else:
            cw, alog, dtb, ong = gdn_conv_w[j], gdn_a_log[j], gdn_dt_bias[j], gdn_o_norm_g[j]
            qkv, z, ab, tail = _gdn_front(xp, g_l, p3(shift), p3(scale), gdn_in_t, j, cw, t)
            s0 = jnp.zeros((b, GDN_V_HEADS, GDN_HEAD_DIM, GDN_HEAD_DIM), F32)
            seq = lambda a: a.reshape(b, t, a.shape[-1])
            xp, s_new = _gdn(seq(qkv), seq(z), seq(ab), seq(xp), p3(gate), s0, alog, dtb, ong, gdn_out[j],
                             final_norm_g, last)
            xp = xp.reshape(b * t, d)
            cp_out.append(tail[:, SUBLANES - (GDN_CONV - 1):])
            sp_out.append(s_new)

            xqkv, z, ab = _inproj(xs, g_l, s_rows(shift), s_rows(scale), gdn_in_t, j, gdn_splits, gdn_scales,
                                  (F32,) * 3, sr, "gdn_inproj_s")
            a, s_sample = _gdn_sample(xqkv, z, ab, state_gdn_conv[j], state_gdn_s, s_sample, j, cw, alog, dtb, ong, dt)
            xs = _outproj(a, gdn_out[j], xs, s_rows(gate), final_norm_g, last, "gdn_outproj_s")
            cs_out.append(xqkv.reshape(db, sr, GDN_CONV_CH)[:, dt - (GDN_CONV - 1):dt])

    y_prompt = xp.reshape(b, t, d)
    y_sample = xs.reshape(db, sr, d)[:, :dt]
    return (y_prompt, y_sample, jnp.stack(kp_out), jnp.stack(vp_out), position_major(nk_all), position_major(nv_all),
            jnp.stack(cp_out), jnp.stack(sp_out), jnp.stack(cs_out), s_sample)
```

```python
import functools
import itertools

import jax
import jax.numpy as jnp
from jax import lax
from jax.experimental import pallas as pl
from jax.experimental.pallas import tpu as pltpu

F32 = jnp.float32
BF16 = jnp.bfloat16

D_MODEL = 1024
DEPTH = 4
EPS = 1e-6
SWA_HEADS = 16
SWA_KV_HEADS = 4
SWA_HEAD_DIM = 64
SWA_GROUP = SWA_HEADS // SWA_KV_HEADS
WINDOW = 128
SWA_Q_W = SWA_HEADS * SWA_HEAD_DIM
SWA_KV_W = SWA_KV_HEADS * SWA_HEAD_DIM
SWA_SCALE = SWA_HEAD_DIM ** -0.5
GDN_QK_HEADS = 8
GDN_V_HEADS = 16
GDN_HEAD_DIM = 128
GDN_K_W = GDN_QK_HEADS * GDN_HEAD_DIM
GDN_V_W = GDN_V_HEADS * GDN_HEAD_DIM
GDN_CONV = 4
GDN_CONV_CH = 2 * GDN_K_W + GDN_V_W
GDN_CHUNK = 64
LANES = 128
SUBLANES = 8
HALF = LANES // 2
GDN_TILE = 2 * GDN_CHUNK
SAMPLE_ROWS = SUBLANES
VMEM_LIMIT = 56 * 1024 * 1024
PROMPT_ROWS = 512
SAMPLE_TILE_ROWS = 256
PROJ_COLS = 512
SWA_SAMPLE_SEQS = 16
GDN_SAMPLE_SEQS = 8
GDN_STEP_TILES = 2

_NT = (((1,), (1,)), ((), ()))


def _dot(a, b):
    return jnp.dot(a.astype(BF16), b.astype(BF16), preferred_element_type=F32)


def _dot_nt(a, b):
    return lax.dot_general(a.astype(BF16), b.astype(BF16), _NT, preferred_element_type=F32)


def _dot_f32(a, b):
    return jnp.dot(a, b, preferred_element_type=F32, precision=lax.Precision.HIGHEST)


def _silu(x):
    return x * jax.nn.sigmoid(x)


def _iota(shape, dim):
    return lax.broadcasted_iota(jnp.int32, shape, dim)


def _rms(x, g):
    return x * lax.rsqrt(jnp.mean(x * x, axis=-1, keepdims=True) + EPS) * g


def _params(*sem):
    return pltpu.CompilerParams(dimension_semantics=sem, vmem_limit_bytes=VMEM_LIMIT)


def _mod_kernel(c_ref, w_ref, b_ref, o_ref):
    o_ref[...] = _dot(_silu(c_ref[...]), w_ref[...]) + b_ref[...]


def _modulation(c_all, w_mod, b_mod):
    r, d = c_all.shape
    n = w_mod.shape[-1]
    tn = n // 2 if n % (2 * LANES) == 0 else n
    return pl.pallas_call(
        _mod_kernel,
        grid=(DEPTH, n // tn),
        in_specs=[pl.BlockSpec((r, d), lambda l, j: (0, 0)),
                  pl.BlockSpec((None, d, tn), lambda l, j: (l, 0, j)),
                  pl.BlockSpec((None, 1, tn), lambda l, j: (l, 0, j))],
        out_specs=pl.BlockSpec((None, r, tn), lambda l, j: (l, 0, j)),
        out_shape=jax.ShapeDtypeStruct((DEPTH, r, n), F32),
        compiler_params=_params("arbitrary", "arbitrary"),
        name="adaln_modulation",
    )(c_all, w_mod, b_mod.reshape(DEPTH, 1, n))


def _per_row(ref, rows):
    a = ref[...]
    if a.ndim == 2:
        return a
    g, _, d = a.shape
    return jnp.broadcast_to(a, (g, rows // g, d)).reshape(rows, d)


def _inproj_kernel(x_ref, g_ref, sh_ref, sc_ref, w_ref, *o_refs, splits, scales, w_rows_out):
    tm = x_ref.shape[0]
    h = _rms(x_ref[...], g_ref[...]) * (1.0 + _per_row(sc_ref, tm)) + _per_row(sh_ref, tm)
    hb = h.astype(BF16)
    for o_ref, (c0, n), s in zip(o_refs, splits, scales):
        for cc in range(0, n, PROJ_COLS):
            w = min(PROJ_COLS, n - cc)
            if w_rows_out:
                r = _dot_nt(hb, w_ref[c0 + cc:c0 + cc + w, :])
            else:
                r = jnp.dot(hb, w_ref[:, c0 + cc:c0 + cc + w].astype(BF16), preferred_element_type=F32)
            o_ref[:, cc:cc + w] = (r * s if s != 1.0 else r).astype(o_ref.dtype)
        if n < o_ref.shape[1]:
            o_ref[:, n:] = jnp.zeros((o_ref.shape[0], o_ref.shape[1] - n), o_ref.dtype)


def _inproj(x, g, shift, scale, w_all, j, splits, scales, dtypes, rows_per_mod, name):
    m, d = x.shape
    assert w_all.shape[1] != w_all.shape[2] and d in w_all.shape[1:]
    pad = lambda n: -(-n // LANES) * LANES
    tm = min(PROMPT_ROWS, rows_per_mod) if rows_per_mod >= PROMPT_ROWS else min(SAMPLE_TILE_ROWS, m)
    if rows_per_mod >= tm:
        per = rows_per_mod // tm
        mod_spec = pl.BlockSpec((None, 1, d), lambda i: (i // per, 0, 0))
    else:
        mod_spec = pl.BlockSpec((tm // rows_per_mod, 1, d), lambda i: (i, 0, 0))
    return pl.pallas_call(
        functools.partial(_inproj_kernel, splits=splits, scales=scales, w_rows_out=w_all.shape[2] == d),
        grid=(m // tm,),
        in_specs=[pl.BlockSpec((tm, d), lambda i: (i, 0)),
                  pl.BlockSpec((1, d), lambda i: (0, 0)),
                  mod_spec, mod_spec,
                  pl.BlockSpec((None,) + w_all.shape[1:], lambda i: (j, 0, 0), pipeline_mode=pl.Buffered(1))],
        out_specs=[pl.BlockSpec((tm, pad(n)), lambda i: (i, 0)) for _, n in splits],
        out_shape=[jax.ShapeDtypeStruct((m, pad(n)), dtype) for (_, n), dtype in zip(splits, dtypes)],
        compiler_params=_params("arbitrary"),
        name=name,
    )(x, g.reshape(1, d), shift, scale, w_all)


def _gdn_front_kernel(x_ref, g_ref, sh_ref, sc_ref, w_ref, cw_ref, qkv_ref, z_ref, ab_ref, tail_ref, carry, rbuf,
                      *, tiles_per_seq):
    tm = x_ref.shape[0]
    head = GDN_CONV - 1
    base = SUBLANES - head
    width = 4 * LANES

    @pl.when(pl.program_id(0) % tiles_per_seq == 0)
    def _():
        carry[...] = jnp.zeros(carry.shape, F32)

    h = _rms(x_ref[...], g_ref[...]) * (1.0 + sc_ref[...]) + sh_ref[...]
    hb = h.astype(BF16)
    proj = lambda c0: _dot_nt(hb, w_ref[c0:c0 + width, :])
    n_qkv, n_z = GDN_CONV_CH // width, GDN_V_W // width
    r_next = proj(0)
    for cb in range(n_qkv):
        cols = slice(cb * width, (cb + 1) * width)
        r = r_next
        if cb + 1 < n_qkv:
            r_next = proj((cb + 1) * width)
        if cb % 2 == 1:
            zc = cb // 2
            z_ref[:, zc * width:(zc + 1) * width] = proj(GDN_CONV_CH + zc * width).astype(z_ref.dtype)
        rbuf[0:SUBLANES, :] = carry[:, cols]
        rbuf[SUBLANES:SUBLANES + tm, :] = r
        y = rbuf[base:base + tm, :] * cw_ref[0:1, cols]
        for w in range(1, GDN_CONV):
            y = y + rbuf[base + w:base + w + tm, :] * cw_ref[w:w + 1, cols]
        carry[:, cols] = r[tm - SUBLANES:tm]
        tail_ref[:, cols] = r[tm - SUBLANES:tm]
        y = _silu(y)
        if cb * width < 2 * GDN_K_W:
            parts = []
            for p in range(width // GDN_HEAD_DIM):
                blk = y[:, p * GDN_HEAD_DIM:(p + 1) * GDN_HEAD_DIM]
                inv = lax.rsqrt(jnp.sum(blk * blk, axis=-1, keepdims=True) + EPS)
                parts.append(blk * (inv * GDN_HEAD_DIM ** -0.5 if cb * width < GDN_K_W else inv))
            y = jnp.concatenate(parts, axis=1)
        qkv_ref[:, cols] = y.astype(qkv_ref.dtype)
    assert n_z == n_qkv // 2
    nab = 2 * GDN_V_HEADS
    ab_ref[:, 0:nab] = _dot_nt(hb, w_ref[GDN_CONV_CH + GDN_V_W:GDN_CONV_CH + GDN_V_W + nab, :])
    ab_ref[:, nab:] = jnp.zeros((tm, LANES - nab), F32)


def _gdn_front(x, g, shift, scale, w_all_t, j, cw, t):
    m, d = x.shape
    tm = min(PROMPT_ROWS, t)
    per = t // tm
    mod_spec = pl.BlockSpec((None, 1, d), lambda i: (i // per, 0, 0))
    row = lambda w: pl.BlockSpec((tm, w), lambda i: (i, 0))
    const = lambda i: (0, 0)
    return pl.pallas_call(
        functools.partial(_gdn_front_kernel, tiles_per_seq=per),
        grid=(m // tm,),
        in_specs=[row(d), pl.BlockSpec((1, d), const), mod_spec, mod_spec,
                  pl.BlockSpec((None,) + w_all_t.shape[1:], lambda i: (j, 0, 0), pipeline_mode=pl.Buffered(1)),
                  pl.BlockSpec(cw.shape, const)],
        out_specs=[row(GDN_CONV_CH), row(GDN_V_W), row(LANES),
                   pl.BlockSpec((None, SUBLANES, GDN_CONV_CH), lambda i: (i // per, 0, 0))],
        out_shape=[jax.ShapeDtypeStruct((m, GDN_CONV_CH), BF16), jax.ShapeDtypeStruct((m, GDN_V_W), BF16),
                   jax.ShapeDtypeStruct((m, LANES), F32), jax.ShapeDtypeStruct((m // t, SUBLANES, GDN_CONV_CH), F32)],
        scratch_shapes=[pltpu.VMEM((SUBLANES, GDN_CONV_CH), F32), pltpu.VMEM((SUBLANES + tm, 4 * LANES), F32)],
        compiler_params=_params("arbitrary"),
        name="gdn_front_p",
    )(x, g.reshape(1, d), shift, scale, w_all_t, cw)


def _finish(a, w_ref, x, gate, fg_ref, final_norm):
    y = x + gate * _dot(a, w_ref[...])
    return _rms(y, fg_ref[...]) if final_norm else y


def _outproj_kernel(a_ref, w_ref, x_ref, gate_ref, fg_ref, o_ref, *, final_norm):
    gate = _per_row(gate_ref, x_ref.shape[0])
    o_ref[...] = _finish(a_ref[...], w_ref, x_ref[...], gate, fg_ref, final_norm)


def _outproj(a, w, x, gate, fg, final_norm, name):
    m, k = a.shape
    d = x.shape[-1]
    tm = min(PROMPT_ROWS, m)
    return pl.pallas_call(
        functools.partial(_outproj_kernel, final_norm=final_norm),
        grid=(m // tm,),
        in_specs=[pl.BlockSpec((tm, k), lambda i: (i, 0)),
                  pl.BlockSpec((k, d), lambda i: (0, 0)),
                  pl.BlockSpec((tm, d), lambda i: (i, 0)),
                  pl.BlockSpec((tm // SAMPLE_ROWS, 1, d), lambda i: (i, 0, 0)),
                  pl.BlockSpec((1, d), lambda i: (0, 0))],
        out_specs=pl.BlockSpec((tm, d), lambda i: (i, 0)),
        out_shape=jax.ShapeDtypeStruct((m, d), F32),
        compiler_params=_params("arbitrary"),
        name=name,
    )(a, w, x, gate, fg.reshape(1, d))


def _swa_block(q, kcat, vcat, mask4, sinks_ref):
    nq = q.shape[0]
    lo = _iota((1, LANES), 1) < HALF
    hi = jnp.logical_not(lo)
    row = _iota((SWA_GROUP * nq, 1), 0)
    kvs = range(SWA_KV_HEADS)
    blk = lambda a, pb: a[:, pb * LANES:(pb + 1) * LANES]
    q4s = []
    for kh in kvs:
        parts = []
        for g in range(SWA_GROUP):
            qg = blk(q, 2 * kh + g // 2)
            if g % 2 != kh % 2:
                qg = pltpu.roll(qg, HALF, axis=1)
            parts.append(jnp.where(lo if kh % 2 == 0 else hi, qg, 0.0))
        q4s.append(jnp.concatenate(parts, axis=0))
    logits = [_dot_nt(q4, blk(kcat, kh // 2)) for kh, q4 in zip(kvs, q4s)]
    es, invs = [], []
    for kh, lg in zip(kvs, logits):
        sink = jnp.full((SWA_GROUP * nq, 1), sinks_ref[SWA_GROUP * kh], F32)
        for g in range(1, SWA_GROUP):
            sink = jnp.where(row >= g * nq, sinks_ref[SWA_GROUP * kh + g], sink)
        lg = jnp.where(mask4, lg, -jnp.inf)
        m = jnp.maximum(jnp.max(lg, axis=-1, keepdims=True), sink)
        e = jnp.exp(lg - m)
        es.append(e)
        invs.append(1.0 / (jnp.sum(e, axis=-1, keepdims=True) + jnp.exp(sink - m)))
    o4s = [_dot(e, blk(vcat, kh // 2)) * inv for kh, e, inv in zip(kvs, es, invs)]
    outs = [None] * (SWA_HEADS // 2)
    for kh, o4 in zip(kvs, o4s):
        for g in range(SWA_GROUP):
            piece = o4[g * nq:(g + 1) * nq]
            if g % 2 != kh % 2:
                piece = pltpu.roll(piece, HALF, axis=1)
            piece = jnp.where(lo if g % 2 == 0 else hi, piece, 0.0)
            qb = 2 * kh + g // 2
            outs[qb] = piece if outs[qb] is None else outs[qb] + piece
    return jnp.concatenate(outs, axis=1)


def _band_mask(nq, nk, jmin):
    i = _iota((SWA_GROUP * nq, nk), 0) % nq
    j = _iota((SWA_GROUP * nq, nk), 1)
    return (j >= i) & (j <= i + WINDOW) & (j >= jmin)


def _swa_prompt_kernel(sinks_ref, q_ref, kp_ref, vp_ref, k_ref, v_ref, z_ref, x_ref, gate_ref, w_ref,
                       o_ref, kbuf, vbuf, abuf, *, tq):
    t = pl.program_id(1)
    kbuf[0:WINDOW] = kp_ref[...]
    kbuf[WINDOW:WINDOW + tq] = k_ref[...]
    vbuf[0:WINDOW] = vp_ref[...]
    vbuf[WINDOW:WINDOW + tq] = v_ref[...]

    def body(sb, carry):
        r = pl.multiple_of(sb * WINDOW, WINDOW)
        jmin = jnp.where((t == 0) & (sb == 0), WINDOW, 0)
        mask4 = _band_mask(WINDOW, 2 * WINDOW, jmin)
        abuf[pl.ds(r, WINDOW), :] = _swa_block(q_ref[pl.ds(r, WINDOW), :].astype(F32), kbuf[pl.ds(r, 2 * WINDOW), :],
                                               vbuf[pl.ds(r, 2 * WINDOW), :], mask4, sinks_ref)
        return carry

    lax.fori_loop(0, tq // WINDOW, body, 0)
    a = abuf[...] * _silu(z_ref[...].astype(F32))
    o_ref[...] = x_ref[...] + gate_ref[...] * _dot(a, w_ref[...])


def _swa_prompt(sinks, q, k, v, z, x, gate, w_out, b, t):
    d = x.shape[-1]
    tq = min(PROMPT_ROWS, t)
    nt = t // tq
    sub = tq // WINDOW
    row = lambda bi, ti: (bi * nt + ti, 0)
    prev = lambda bi, ti: (bi * nt * sub + jnp.maximum(ti * sub - 1, 0), 0)
    const = lambda bi, ti: (0, 0)
    return pl.pallas_call(
        functools.partial(_swa_prompt_kernel, tq=tq),
        grid=(b, nt),
        in_specs=[pl.BlockSpec(memory_space=pltpu.SMEM),
                  pl.BlockSpec((tq, SWA_Q_W), row),
                  pl.BlockSpec((WINDOW, SWA_KV_W), prev),
                  pl.BlockSpec((WINDOW, SWA_KV_W), prev),
                  pl.BlockSpec((tq, SWA_KV_W), row),
                  pl.BlockSpec((tq, SWA_KV_W), row),
                  pl.BlockSpec((tq, SWA_Q_W), row),
                  pl.BlockSpec((tq, d), row),
                  pl.BlockSpec((None, 1, d), lambda bi, ti: (bi, 0, 0)),
                  pl.BlockSpec(w_out.shape, const)],
        out_specs=pl.BlockSpec((tq, d), row),
        out_shape=jax.ShapeDtypeStruct(x.shape, F32),
        scratch_shapes=[pltpu.VMEM((WINDOW + tq, SWA_KV_W), F32),
                        pltpu.VMEM((WINDOW + tq, SWA_KV_W), F32),
                        pltpu.VMEM((tq, SWA_Q_W), F32)],
        compiler_params=_params("arbitrary", "arbitrary"),
        name="swa_prompt",
    )(sinks, q, k, v, k, v, z, x, gate, w_out)


def _swa_sample_kernel(sinks_ref, q_ref, k_ref, v_ref, z_ref, ck_ref, cv_ref, *rest, bb, dt, has_prev):
    a_ref, nk_ref, nv_ref = rest[2:] if has_prev else rest
    nq, grp = SAMPLE_ROWS, SWA_GROUP
    lane = _iota((1, LANES), 1)
    lo = lane < HALF
    hi = jnp.logical_not(lo)
    row = _iota((grp * nq, 1), 0)
    qi = row % nq
    see_cache, see_own = lane >= qi, lane <= qi
    zpad = jnp.zeros((WINDOW - nq, LANES), F32)
    rows = lambda bi: slice(bi * nq, (bi + 1) * nq)
    blk = lambda pb: slice(pb * LANES, (pb + 1) * LANES)
    units = [(bi, kh) for bi in range(bb) for kh in range(SWA_KV_HEADS)]

    q4s = []
    for bi, kh in units:
        keep = lo if kh % 2 == 0 else hi
        parts = []
        for g in range(grp):
            qg = q_ref[rows(bi), blk(2 * kh + g // 2)]
            if g % 2 != kh % 2:
                qg = pltpu.roll(qg, HALF, axis=1)
            parts.append(jnp.where(keep, qg, 0.0))
        q4s.append(jnp.concatenate(parts, axis=0))
    own = lambda ref, bi, pb: jnp.concatenate([ref[rows(bi), blk(pb)], zpad], axis=0)
    lcs = [_dot(q4, ck_ref[bi, blk(kh // 2), :]) for (bi, kh), q4 in zip(units, q4s)]
    los = [_dot_nt(q4, own(k_ref, bi, kh // 2)) for (bi, kh), q4 in zip(units, q4s)]
    ecs, eos, invs = [], [], []
    for (bi, kh), lc, lw in zip(units, lcs, los):
        sink = jnp.full((grp * nq, 1), sinks_ref[grp * kh], F32)
        for g in range(1, grp):
            sink = jnp.where(row >= g * nq, sinks_ref[grp * kh + g], sink)
        lc = jnp.where(see_cache, lc, -jnp.inf)
        lw = jnp.where(see_own, lw, -jnp.inf)
        m = jnp.maximum(jnp.maximum(jnp.max(lc, axis=-1, keepdims=True), jnp.max(lw, axis=-1, keepdims=True)), sink)
        ec, eo = jnp.exp(lc - m), jnp.exp(lw - m)
        den = jnp.sum(ec, axis=-1, keepdims=True) + jnp.sum(eo, axis=-1, keepdims=True) + jnp.exp(sink - m)
        ecs.append(ec)
        eos.append(eo)
        invs.append(1.0 / den)
    o4s = [(_dot_nt(ec, cv_ref[bi, blk(kh // 2), :]) + _dot(eo, own(v_ref, bi, kh // 2))) * inv
           for (bi, kh), ec, eo, inv in zip(units, ecs, eos, invs)]
    for bi in range(bb):
        outs = [None] * (SWA_HEADS // 2)
        for kh in range(SWA_KV_HEADS):
            o4 = o4s[bi * SWA_KV_HEADS + kh]
            for g in range(grp):
                piece = o4[g * nq:(g + 1) * nq]
                if g % 2 != kh % 2:
                    piece = pltpu.roll(piece, HALF, axis=1)
                piece = jnp.where(lo if g % 2 == 0 else hi, piece, 0.0)
                qb = 2 * kh + g // 2
                outs[qb] = piece if outs[qb] is None else outs[qb] + piece
        a_ref[rows(bi), :] = jnp.concatenate(outs, axis=1) * _silu(z_ref[rows(bi), :])

    fresh = lane >= WINDOW - dt
    for src, cache, dst in ((k_ref, ck_ref, nk_ref), (v_ref, cv_ref, nv_ref)):
        for pb in range(SWA_KV_W // LANES):
            new = src[:, blk(pb)]
            if bb * nq < LANES:
                new = jnp.concatenate([new, jnp.zeros((LANES - bb * nq, LANES), F32)], axis=0)
            new_t = new.T
            for bi in range(bb):
                cols = pltpu.roll(new_t, (WINDOW - dt - bi * nq) % LANES, axis=1)
                dst[bi, blk(pb), :] = jnp.where(fresh, cols, pltpu.roll(cache[bi, blk(pb), :], WINDOW - dt, axis=1))


def _swa_sample(sinks, q, k, v, z, ck_all, cv_all, nk_prev, nv_prev, j, dt):
    db = ck_all.shape[1]
    bb = min(SWA_SAMPLE_SEQS, db)
    nq = SAMPLE_ROWS
    assert 2 * dt <= nq
    row = lambda i: (i, 0)
    c_spec = pl.BlockSpec((None, bb, SWA_KV_W, WINDOW), lambda i: (j, i, 0, 0))
    args = [sinks, q, k, v, z, ck_all, cv_all]
    in_specs = [pl.BlockSpec(memory_space=pltpu.SMEM),
                pl.BlockSpec((bb * nq, SWA_Q_W), row),
                pl.BlockSpec((bb * nq, SWA_KV_W), row),
                pl.BlockSpec((bb * nq, SWA_KV_W), row),
                pl.BlockSpec((bb * nq, SWA_Q_W), row),
                c_spec, c_spec]
    aliases = {}
    if nk_prev is not None:
        args += [nk_prev, nv_prev]
        in_specs += [pl.BlockSpec(memory_space=pl.ANY)] * 2
        aliases = {len(args) - 2: 1, len(args) - 1: 2}
    return pl.pallas_call(
        functools.partial(_swa_sample_kernel, bb=bb, dt=dt, has_prev=nk_prev is not None),
        grid=(db // bb,),
        in_specs=in_specs,
        out_specs=[pl.BlockSpec((bb * nq, SWA_Q_W), row), c_spec, c_spec],
        out_shape=[jax.ShapeDtypeStruct(q.shape, F32),
                   jax.ShapeDtypeStruct(ck_all.shape, F32),
                   jax.ShapeDtypeStruct(cv_all.shape, F32)],
        input_output_aliases=aliases,
        compiler_params=_params("arbitrary"),
        name="swa_sample",
    )(*args)


def _block_mul(a, b):
    size, width = b.shape
    b16 = b.astype(BF16)
    zero = jnp.zeros_like(b16)
    blk = _iota((1, width), 1) // size
    return _dot(a, jnp.concatenate([jnp.where(blk == r, b16, zero) for r in range(width // size)], axis=0))


def _diag2(a, b):
    return jnp.concatenate([jnp.concatenate([a, jnp.zeros_like(b)], axis=1),
                            jnp.concatenate([jnp.zeros_like(a), b], axis=1)], axis=0)


def _gdn_kernel(qkv, z_ref, ab_ref, x_ref, gate_ref, s0_ref, alog_ref, dtb_ref, ong_ref, w_ref, fg_ref, o_ref, s_ref,
                obuf, *, nb, ns, final_norm):
    nchunks = GDN_TILE // GDN_CHUNK
    R, C, HD = GDN_TILE, GDN_CHUNK, GDN_HEAD_DIM
    t = pl.program_id(1)
    seqs = range(nb)
    nu = nb * ns
    seq = lambda u: u // ns
    off = lambda u: (u % ns) * R

    @pl.when(t == 0)
    def _():
        s_ref[...] = s0_ref[...]

    ri, ci = _iota((R, R), 0), _iota((R, R), 1)
    same = (ri // C) == (ci // C)
    tri, ones = jnp.where(same & (ci <= ri), 1.0, 0.0), jnp.where(same, 1.0, 0.0)
    gcs, betas, egcs, bgcs, gcTs, glTs, dlTs = ([None] * nu for _ in range(7))

    def prologue(sq):
        ab = ab_ref[seq(sq), off(sq):off(sq) + R]
        g = -jnp.exp(alog_ref[...]) * jax.nn.softplus(ab + dtb_ref[...])
        beta = pltpu.roll(jax.nn.sigmoid(ab), LANES - GDN_V_HEADS, axis=1)
        gc = _dot_f32(tri, g)
        gl = _dot_f32(ones, g)
        egc = jnp.exp(gc)
        gcT, glT = gc.T, gl.T
        gcs[sq], betas[sq], egcs[sq], bgcs[sq] = gc, beta, egc, beta * egc
        gcTs[sq], glTs[sq], dlTs[sq] = gcT, glT, glT - gcT
        yield

    lane = _iota((1, LANES), 1)
    lo = lane < HALF
    jj = lane % C
    ii = _iota((C, 1), 0)
    incl, strict = ii >= jj, ii > jj
    zc = jnp.zeros((C, HD), F32)

    def col2(arr, r0, h0):
        return jnp.where(lo, arr[r0:r0 + C, h0:h0 + 1], arr[r0:r0 + C, h0 + 1:h0 + 2])

    def pick(c, row_a, row_b):
        if c == 0:
            return jnp.where(lo, row_a, pltpu.roll(row_b, HALF, axis=1))
        return jnp.where(lo, pltpu.roll(row_a, HALF, axis=1), row_b)

    pairs = range(GDN_QK_HEADS)
    heads = range(GDN_V_HEADS)
    units = [(c, p) for c in range(nchunks) for p in pairs]
    quads = [(c, pp) for c in range(nchunks) for pp in range(GDN_QK_HEADS // 2)]
    rows = lambda sq, c: slice(off(sq) + c * C, off(sq) + (c + 1) * C)
    qsl = lambda sq, c, p: qkv[seq(sq), rows(sq, c), p * HD:(p + 1) * HD]
    ksl = lambda sq, c, p: qkv[seq(sq), rows(sq, c), GDN_K_W + p * HD:GDN_K_W + (p + 1) * HD]
    vsl = lambda sq, c, h: qkv[seq(sq), rows(sq, c), 2 * GDN_K_W + h * HD:2 * GDN_K_W + (h + 1) * HD]
    col = lambda arr, c, h: arr[c * C:(c + 1) * C, h:h + 1]
    jq = _iota((1, 2 * LANES), 1) % C
    kts, lows, qks, sols = ([None] * nu for _ in range(4))

    def products(sq):
        kts[sq] = [qkv[seq(sq), off(sq):off(sq) + R, GDN_K_W + p * HD:GDN_K_W + (p + 1) * HD].astype(F32).T
                   for p in pairs]
        yield
        ms = [_dot_nt(jnp.concatenate([ksl(sq, c, p), qsl(sq, c, p)], axis=0), jnp.concatenate([ksl(sq, c, p)] * 2, axis=0))
              for c, p in units]
        yield
        lows[sq], qks[sq] = [], []
        for (c, p), m in zip(units, ms):
            h0 = 2 * p
            row_gc = pick(c, gcTs[sq][h0:h0 + 1, :], gcTs[sq][h0 + 1:h0 + 2, :])
            decay = jnp.exp(jnp.where(incl, col2(gcs[sq], c * C, h0) - row_gc, -jnp.inf))
            lows[sq].append(jnp.where(strict, m[0:C] * decay, 0.0) * col2(betas[sq], c * C, h0))
            qks[sq].append(m[C:2 * C] * decay)
            if p % 4 == 3:
                yield

    def inverse(sq):
        low4 = [jnp.concatenate([lows[sq][c * GDN_QK_HEADS + 2 * pp], lows[sq][c * GDN_QK_HEADS + 2 * pp + 1]], axis=1)
                for c, pp in quads]
        invs = [jnp.where(ii == jq, 1.0, 0.0) - jnp.where((ii // 2) == (jq // 2), low, 0.0) for low in low4]
        yield
        s = 2
        while s < C:
            below = ((ii // s) % 2 == 1) & ((jq // s) == (ii // s) - 1)
            eps, new = [], []
            for i, (low, inv) in enumerate(zip(low4, invs)):
                eps.append(_block_mul(jnp.where(below, low, 0.0), inv))
                if i % 2:
                    yield
            for i, (inv, ep) in enumerate(zip(invs, eps)):
                new.append(inv - _block_mul(inv, ep))
                if i % 2:
                    yield
            invs = new
            s *= 2
        sols[sq] = []
        for (c, pp), inv in zip(quads, invs):
            rhs = []
            for idx in range(4):
                h = 4 * pp + idx
                piece = [vsl(sq, c, h) * col(betas[sq], c, h), ksl(sq, c, h // 2) * col(bgcs[sq], c, h)]
                rhs.append(jnp.concatenate([zc] * (2 * idx) + piece + [zc] * (6 - 2 * idx), axis=1))
            sols[sq].append(_dot(inv, jnp.concatenate(rhs, axis=0)))
            if pp % 2 == 1:
                yield

    def state(sq):
        for c in range(nchunks):
            in_chunk = (lane // C) == c
            quad = lambda h: sols[sq][c * (GDN_QK_HEADS // 2) + h // 4]
            w_v = lambda h: quad(h)[:, 2 * (h % 4) * HD:(2 * (h % 4) + 1) * HD]
            w_k = lambda h: quad(h)[:, (2 * (h % 4) + 1) * HD:(2 * (h % 4) + 2) * HD]
            a_s = []
            for p in pairs:
                top = jnp.concatenate([w_k(2 * p), w_k(2 * p + 1)], axis=1)
                bot = jnp.concatenate([qsl(sq, c, p) * col(egcs[sq], c, 2 * p), qsl(sq, c, p) * col(egcs[sq], c, 2 * p + 1)],
                                      axis=1)
                a_s.append(_dot(jnp.concatenate([top, bot], axis=0),
                                _diag2(s_ref[seq(sq), 2 * p], s_ref[seq(sq), 2 * p + 1])))
            yield
            us = [[w_v(2 * p + idx) - a[0:C, idx * HD:(idx + 1) * HD] for idx in range(2)] for p, a in zip(pairs, a_s)]
            for p, a, u in zip(pairs, a_s, us):
                obuf[seq(sq), rows(sq, c), 2 * p * HD:(2 * p + 2) * HD] = (
                    a[C:2 * C] + _dot(qks[sq][c * GDN_QK_HEADS + p], _diag2(*u)))
            yield
            for p, u in zip(pairs, us):
                kd, ue, gt = [], [], []
                for idx in range(2):
                    h = 2 * p + idx
                    kd.append(kts[sq][p] * jnp.exp(jnp.where(in_chunk, dlTs[sq][h:h + 1, :], -jnp.inf)))
                    ue.append(jnp.concatenate([u[idx], zc] if c == 0 else [zc, u[idx]], axis=0))
                    gt.append(jnp.exp(pick(c, glTs[sq][h:h + 1, :], glTs[sq][h:h + 1, :])))
                upd = _dot(jnp.concatenate(kd, axis=1), _diag2(*ue))
                for idx in range(2):
                    h = 2 * p + idx
                    s_ref[seq(sq), h] = s_ref[seq(sq), h] * gt[idx] + upd[:, idx * HD:(idx + 1) * HD]
                if p % 4 == 3:
                    yield

    def run(*gens):
        gens = list(gens)
        while gens:
            for gen in list(gens):
                if next(gen, "done") == "done":
                    gens.remove(gen)

    for k in range(nu + 2):
        live = [itertools.chain(prologue(k), products(k))] if k < nu else []
        live += [inverse(k - 1)] if 0 <= k - 1 < nu else []
        live += [state(k - 2)] if 0 <= k - 2 < nu else []
        run(*live)

    gated = []
    for sq in seqs:
        o = obuf[sq]
        parts = [_rms(o[:, h * HD:(h + 1) * HD], ong_ref[...]) for h in heads]
        gated.append(jnp.concatenate(parts, axis=1) * _silu(z_ref[sq].astype(F32)))
    out = _dot(jnp.concatenate(gated, axis=0), w_ref[...])
    for sq in seqs:
        y = x_ref[sq] + gate_ref[sq] * out[sq * ns * R:(sq + 1) * ns * R]
        o_ref[sq] = _rms(y, fg_ref[...]) if final_norm else y


def _gdn(qkv, z, ab, x, gate, s0, alog, dtb, ong, w_out, fg, final_norm):
    b, t, d = x.shape
    nb = 2 if b % 2 == 0 else 1
    ns = GDN_STEP_TILES if t % (GDN_STEP_TILES * GDN_TILE) == 0 else 1
    rows_in = ns * GDN_TILE
    tile = lambda w: pl.BlockSpec((nb, rows_in, w), lambda bi, ti: (bi, ti, 0))
    const = lambda bi, ti: (0, 0)
    per_b3 = lambda bi, ti: (bi, 0, 0)
    per_b4 = lambda bi, ti: (bi, 0, 0, 0)
    vec = lambda a: jnp.pad(a.astype(F32), (0, LANES - a.shape[0])).reshape(1, LANES)
    state = (GDN_V_HEADS, GDN_HEAD_DIM, GDN_HEAD_DIM)
    return pl.pallas_call(
        functools.partial(_gdn_kernel, nb=nb, ns=ns, final_norm=final_norm),
        grid=(b // nb, t // rows_in),
        in_specs=[tile(GDN_CONV_CH), tile(GDN_V_W), tile(LANES), tile(d),
                  pl.BlockSpec((nb, 1, d), per_b3),
                  pl.BlockSpec((nb,) + state, per_b4),
                  pl.BlockSpec((1, LANES), const),
                  pl.BlockSpec((1, LANES), const),
                  pl.BlockSpec((1, GDN_HEAD_DIM), const),
                  pl.BlockSpec(w_out.shape, const),
                  pl.BlockSpec((1, d), const)],
        out_specs=[tile(d), pl.BlockSpec((nb,) + state, per_b4)],
        out_shape=[jax.ShapeDtypeStruct(x.shape, F32),
                   jax.ShapeDtypeStruct((b,) + state, F32)],
        scratch_shapes=[pltpu.VMEM((nb, rows_in, GDN_V_W), F32)],
        compiler_params=_params("arbitrary", "arbitrary"),
        name="gdn_prompt",
    )(qkv, z, ab, x, gate, s0, vec(alog), vec(dtb), ong.reshape(1, GDN_HEAD_DIM), w_out, fg.reshape(1, d))


def _tile_roll(a, d):
    r, w = a.shape
    return pltpu.roll(a.reshape(r // SUBLANES, SUBLANES, w), d, axis=1).reshape(r, w)


def _gdn_sample_kernel(xqkv_ref, z_ref, ab_ref, conv_ref, s_ref, cw_ref, alog_ref, dtb_ref, ong_ref, *rest,
                       bb, dt, has_prev):
    a_ref, so_ref, st8, qkv = rest[1:] if has_prev else rest
    R, HD = bb * SUBLANES, GDN_HEAD_DIM
    tix = _iota((R, 1), 0) % SUBLANES
    live = tix < dt
    head = GDN_CONV - 1
    lane = _iota((1, LANES), 1)

    st8[...] = jnp.zeros(st8.shape, F32)
    for bi in range(bb):
        for tap in range(head):
            row = (bi + 1) * SUBLANES - head + tap
            st8[row:row + 1, :] = conv_ref[tap, bi:bi + 1, :]
    for cb in range(GDN_CONV_CH // GDN_K_W):
        cols = slice(cb * GDN_K_W, (cb + 1) * GDN_K_W)
        x, st = xqkv_ref[:, cols], st8[:, cols]
        y = x * cw_ref[head:head + 1, cols]
        for d in range(1, GDN_CONV):
            y = y + jnp.where(tix >= d, _tile_roll(x, d), _tile_roll(st, d)) * cw_ref[head - d:head - d + 1, cols]
        y = _silu(y)
        if cb < 2:
            for p in range(GDN_QK_HEADS):
                blk = y[:, p * HD:(p + 1) * HD]
                inv = lax.rsqrt(jnp.sum(blk * blk, axis=-1, keepdims=True) + EPS)
                if cb == 0:
                    inv = inv * HD ** -0.5
                qkv[:, cb * GDN_K_W + p * HD:cb * GDN_K_W + (p + 1) * HD] = blk * inv
        else:
            qkv[:, cols] = y

    ab = ab_ref[...]
    g = jnp.where(live, -jnp.exp(alog_ref[...]) * jax.nn.softplus(ab + dtb_ref[...]), 0.0)
    beta = jnp.where(live, pltpu.roll(jax.nn.sigmoid(ab), LANES - GDN_V_HEADS, axis=1), 0.0)
    gc, gl = g, g
    for d in range(1, SUBLANES):
        gl = gl + _tile_roll(g, d)
        if d < dt:
            gc = gc + jnp.where(tix >= d, _tile_roll(g, d), 0.0)
    egc = jnp.exp(gc)
    bgc = beta * egc
    ekl = jnp.where(live, jnp.exp(jnp.where(live, gl - gc, 0.0)), 0.0)
    gtot = jnp.exp(gl)
    dec = [None] + [jnp.where(tix >= d, jnp.exp(jnp.where(tix >= d, gc - _tile_roll(gc, d), 0.0)), 0.0)
                    for d in range(1, dt)]

    zero = jnp.zeros((R, LANES), F32)
    kk, qk = [zero] * dt, [zero] * dt
    qsl = lambda p: qkv[:, p * HD:(p + 1) * HD]
    ksl = lambda p: qkv[:, GDN_K_W + p * HD:GDN_K_W + (p + 1) * HD]
    vsl = lambda h: qkv[:, 2 * GDN_K_W + h * HD:2 * GDN_K_W + (h + 1) * HD]
    for p in range(GDN_QK_HEADS):
        mine = (lane // 2) == p
        for d in range(dt):
            ks = ksl(p) if d == 0 else _tile_roll(ksl(p), d)
            qk[d] = jnp.where(mine, jnp.sum(qsl(p) * ks, axis=-1, keepdims=True), qk[d])
            if d:
                kk[d] = jnp.where(mine, jnp.sum(ksl(p) * ks, axis=-1, keepdims=True), kk[d])
    low = [None] + [beta * kk[d] * dec[d] for d in range(1, dt)]
    qkd = [qk[0]] + [qk[d] * dec[d] for d in range(1, dt)]
    l1s1, l1s2 = _tile_roll(low[1], 1), _tile_roll(low[1], 2)
    tinv = [None, -low[1], -low[2] + low[1] * l1s1,
            -low[3] + low[1] * _tile_roll(low[2], 1) + low[2] * l1s2 - low[1] * l1s1 * l1s2]
    heads = range(GDN_V_HEADS)
    spread = lambda arr: [jnp.broadcast_to(arr[:, h:h + 1], (R, LANES)) for h in heads]
    s_beta, s_bgc, s_egc, s_ekl, s_gtot = spread(beta), spread(bgc), spread(egc), spread(ekl), spread(gtot)
    s_tinv = [None] + [spread(tinv[d]) for d in range(1, dt)]
    s_qkd = [spread(qkd[d]) for d in range(dt)]

    xvs, m8s = [], []
    for h in heads:
        p = h // 2
        rv, rk = vsl(h) * s_beta[h], ksl(p) * s_bgc[h]
        xv, xk = rv, rk
        for d in range(1, dt):
            xv = xv + s_tinv[d][h] * _tile_roll(rv, d)
            xk = xk + s_tinv[d][h] * _tile_roll(rk, d)
        xvs.append(xv)
        m8s.append(jnp.where(live, xk, _tile_roll(qsl(p) * s_egc[h], dt)))
    tile = lambda a, bi: a[bi * SUBLANES:(bi + 1) * SUBLANES]
    aws = [jnp.concatenate([_dot(tile(m8s[h], bi), s_ref[bi, h]) for bi in range(bb)], axis=0) for h in heads]
    outs, uds = [], []
    for h in heads:
        u = jnp.where(live, xvs[h] - aws[h], 0.0)
        o = _tile_roll(aws[h], SUBLANES - dt) + s_qkd[0][h] * u
        for d in range(1, dt):
            o = o + s_qkd[d][h] * _tile_roll(u, d)
        outs.append(_rms(o, ong_ref[...]))
        uds.append(u * s_ekl[h])
    a_ref[...] = jnp.concatenate(outs, axis=1) * _silu(z_ref[...])
    tn = (((0,), (0,)), ((), ()))
    for h in heads:
        kp = ksl(h // 2)
        for bi in range(bb):
            upd = lax.dot_general(tile(kp, bi).astype(BF16), tile(uds[h], bi).astype(BF16), tn,
                                  preferred_element_type=F32)
            so_ref[bi, h] = s_ref[bi, h] * s_gtot[h][bi * SUBLANES:bi * SUBLANES + 1] + upd


def _gdn_sample(xqkv, z, ab, conv_all, s_all, s_prev, j, cw, alog, dtb, ong, dt):
    db = s_all.shape[1]
    bb = min(GDN_SAMPLE_SEQS, db)
    r = bb * SUBLANES
    assert 2 * dt <= SUBLANES and dt == GDN_CONV
    row = lambda i: (i, 0)
    const = lambda i: (0, 0)
    state = (GDN_V_HEADS, GDN_HEAD_DIM, GDN_HEAD_DIM)
    s_spec = pl.BlockSpec((None, bb) + state, lambda i: (j, i, 0, 0, 0))
    vec = lambda a: jnp.pad(a.astype(F32), (0, LANES - a.shape[0])).reshape(1, LANES)
    args = [xqkv, z, ab, conv_all, s_all, cw, vec(alog), vec(dtb), ong.reshape(1, GDN_HEAD_DIM)]
    in_specs = [pl.BlockSpec((r, GDN_CONV_CH), row),
                pl.BlockSpec((r, GDN_V_W), row),
                pl.BlockSpec((r, LANES), row),
                pl.BlockSpec((None, GDN_CONV - 1, bb, GDN_CONV_CH), lambda i: (j, 0, i, 0)),
                s_spec,
                pl.BlockSpec(cw.shape, const),
                pl.BlockSpec((1, LANES), const),
                pl.BlockSpec((1, LANES), const),
                pl.BlockSpec((1, GDN_HEAD_DIM), const)]
    aliases = {}
    if s_prev is not None:
        args.append(s_prev)
        in_specs.append(pl.BlockSpec(memory_space=pl.ANY))
        aliases = {len(args) - 1: 1}
    return pl.pallas_call(
        functools.partial(_gdn_sample_kernel, bb=bb, dt=dt, has_prev=s_prev is not None),
        grid=(db // bb,),
        in_specs=in_specs,
        out_specs=[pl.BlockSpec((r, GDN_V_W), row), s_spec],
        out_shape=[jax.ShapeDtypeStruct(z.shape, F32), jax.ShapeDtypeStruct(s_all.shape, F32)],
        scratch_shapes=[pltpu.VMEM((r, GDN_CONV_CH), F32), pltpu.VMEM((r, GDN_CONV_CH), F32)],
        input_output_aliases=aliases,
        compiler_params=_params("arbitrary"),
        name="gdn_sample",
    )(*args)


def kernel(x_prompt, x_sample, cache_swa_k, cache_swa_v, state_gdn_conv, state_gdn_s, c_prompt, c_sample, norm_g,
           w_mod, b_mod, swa_w_in, swa_sinks, swa_w_out, gdn_w_in, gdn_conv_w, gdn_a_log, gdn_dt_bias, gdn_o_norm_g,
           gdn_w_out, final_norm_g):
    b, t, d = x_prompt.shape
    db, dt, _ = x_sample.shape
    assert d == D_MODEL and t % GDN_TILE == 0 and GDN_CONV - 1 <= dt <= SAMPLE_ROWS
    sr = SAMPLE_ROWS

    mod = _modulation(jnp.concatenate([c_prompt, c_sample], axis=0), w_mod, b_mod)
    xp = x_prompt.reshape(b * t, d)
    xs = jnp.pad(x_sample, ((0, 0), (0, sr - dt), (0, 0))).reshape(db * sr, d)

    gdn_in_t = jnp.transpose(gdn_w_in, (0, 2, 1)).astype(BF16)
    swa_out = swa_w_out.astype(BF16)
    gdn_out = gdn_w_out.astype(BF16)
    swa_splits = ((0, SWA_Q_W), (SWA_Q_W, SWA_KV_W), (SWA_Q_W + SWA_KV_W, SWA_KV_W), (SWA_Q_W + 2 * SWA_KV_W, SWA_Q_W))
    swa_scales = (SWA_SCALE, 1.0, 1.0, 1.0)
    gdn_splits = ((0, GDN_CONV_CH), (GDN_CONV_CH, GDN_V_W), (GDN_CONV_CH + GDN_V_W, 2 * GDN_V_HEADS))
    gdn_scales = (1.0, 1.0, 1.0)

    n_swa = cache_swa_k.shape[0]
    feature_major = lambda c: jnp.transpose(c, (0, 1, 3, 4, 2)).reshape(n_swa, db, SWA_KV_W, WINDOW)
    position_major = lambda c: jnp.transpose(c.reshape(n_swa, db, SWA_KV_HEADS, SWA_HEAD_DIM, WINDOW), (0, 1, 4, 2, 3))
    ck_all, cv_all = feature_major(cache_swa_k), feature_major(cache_swa_v)
    conv_all = jnp.transpose(state_gdn_conv, (0, 2, 1, 3))

    kp_out, vp_out, cp_out, sp_out, cs_out = [], [], [], [], []
    s_sample = nk_all = nv_all = None
    for layer in range(DEPTH):
        j = layer // 2
        last = layer == DEPTH - 1
        shift, scale, gate = (mod[layer, :, i * d:(i + 1) * d] for i in range(3))
        p3 = lambda a: a[:b].reshape(b, 1, d)
        s_rows = lambda a: a[b:].reshape(db, 1, d)
        g_l = norm_g[layer]
        if layer % 2 == 0:
            q, k, v, z = _inproj(xp, g_l, p3(shift), p3(scale), swa_w_in, j, swa_splits, swa_scales,
                                 (BF16, F32, F32, BF16), t, "swa_inproj_p")
            xp = _swa_prompt(swa_sinks[j], q, k, v, z, xp, p3(gate), swa_out[j], b, t)
            keep = min(WINDOW, t)
            tail = lambda a: a.reshape(b, t, SWA_KV_W)[:, -keep:].reshape(b, keep, SWA_KV_HEADS, SWA_HEAD_DIM)
            kp_out.append(tail(k))
            vp_out.append(tail(v))

            q, k, v, z = _inproj(xs, g_l, s_rows(shift), s_rows(scale), swa_w_in, j, swa_splits, swa_scales,
                                 (F32,) * 4, sr, "swa_inproj_s")
            a, nk_all, nv_all = _swa_sample(swa_sinks[j], q, k, v, z, ck_all, cv_all, nk_all, nv_all, j, dt)
            xs = _outproj(a, swa_out[j], xs, s_rows(gate), final_norm_g, False, "swa_outproj_s")
        else:
            cw, alog, dtb, ong = gdn_conv_w[j], gdn_a_log[j], gdn_dt_bias[j], gdn_o_norm_g[j]
            qkv, z, ab, tail = _gdn_front(xp, g_l, p3(shift), p3(scale), gdn_in_t, j, cw, t)
            s0 = jnp.zeros((b, GDN_V_HEADS, GDN_HEAD_DIM, GDN_HEAD_DIM), F32)
            seq = lambda a: a.reshape(b, t, a.shape[-1])
            xp, s_new = _gdn(seq(qkv), seq(z), seq(ab), seq(xp), p3(gate), s0, alog, dtb, ong, gdn_out[j],
                             final_norm_g, last)
            xp = xp.reshape(b * t, d)
            cp_out.append(tail[:, SUBLANES - (GDN_CONV - 1):])
            sp_out.append(s_new)

            xqkv, z, ab = _inproj(xs, g_l, s_rows(shift), s_rows(scale), gdn_in_t, j, gdn_splits, gdn_scales,
                                  (F32,) * 3, sr, "gdn_inproj_s")
            a, s_sample = _gdn_sample(xqkv, z, ab, conv_all, state_gdn_s, s_sample, j, cw, alog, dtb, ong, dt)
            xs = _outproj(a, gdn_out[j], xs, s_rows(gate), final_norm_g, last, "gdn_outproj_s")
            cs_out.append(xqkv.reshape(db, sr, GDN_CONV_CH)[:, dt - (GDN_CONV - 1):dt])

    y_prompt = xp.reshape(b, t, d)
    y_sample = xs.reshape(db, sr, d)[:, :dt]
    return (y_prompt, y_sample, jnp.stack(kp_out), jnp.stack(vp_out), position_major(nk_all), position_major(nv_all),
            jnp.stack(cp_out), jnp.stack(sp_out), jnp.stack(cs_out), s_sample)
```

```python
import functools
import itertools

import jax
import jax.numpy as jnp
from jax import lax
from jax.experimental import pallas as pl
from jax.experimental.pallas import tpu as pltpu

F32 = jnp.float32
BF16 = jnp.bfloat16

D_MODEL = 1024
DEPTH = 4
EPS = 1e-6
SWA_HEADS = 16
SWA_KV_HEADS = 4
SWA_HEAD_DIM = 64
SWA_GROUP = SWA_HEADS // SWA_KV_HEADS
WINDOW = 128
SWA_Q_W = SWA_HEADS * SWA_HEAD_DIM
SWA_KV_W = SWA_KV_HEADS * SWA_HEAD_DIM
SWA_SCALE = SWA_HEAD_DIM ** -0.5
LOG2E = 1.4426950408889634
GDN_QK_HEADS = 8
GDN_V_HEADS = 16
GDN_HEAD_DIM = 128
GDN_K_W = GDN_QK_HEADS * GDN_HEAD_DIM
GDN_V_W = GDN_V_HEADS * GDN_HEAD_DIM
GDN_CONV = 4
GDN_CONV_CH = 2 * GDN_K_W + GDN_V_W
GDN_CHUNK = 64
LANES = 128
SUBLANES = 8
HALF = LANES // 2
GDN_TILE = 2 * GDN_CHUNK
SAMPLE_ROWS = SUBLANES
VMEM_LIMIT = 56 * 1024 * 1024
PROMPT_ROWS = 512
SAMPLE_TILE_ROWS = 256
PROJ_COLS = 512
SWA_SAMPLE_SEQS = 16
GDN_SAMPLE_SEQS = 8
GDN_STEP_TILES = 2

_NT = (((1,), (1,)), ((), ()))


def _dot(a, b):
    return jnp.dot(a.astype(BF16), b.astype(BF16), preferred_element_type=F32)


def _dot_nt(a, b):
    return lax.dot_general(a.astype(BF16), b.astype(BF16), _NT, preferred_element_type=F32)


def _dot_f32(a, b):
    return jnp.dot(a, b, preferred_element_type=F32, precision=lax.Precision.HIGHEST)


def _silu(x):
    return x * jax.nn.sigmoid(x)


def _iota(shape, dim):
    return lax.broadcasted_iota(jnp.int32, shape, dim)


def _rms(x, g):
    return x * lax.rsqrt(jnp.mean(x * x, axis=-1, keepdims=True) + EPS) * g


def _params(*sem):
    return pltpu.CompilerParams(dimension_semantics=sem, vmem_limit_bytes=VMEM_LIMIT)


def _mod_kernel(c_ref, w_ref, b_ref, o_ref):
    o_ref[...] = _dot(_silu(c_ref[...]), w_ref[...]) + b_ref[...]


def _modulation(c_all, w_mod, b_mod):
    r, d = c_all.shape
    n = w_mod.shape[-1]
    tn = n // 2 if n % (2 * LANES) == 0 else n
    return pl.pallas_call(
        _mod_kernel,
        grid=(DEPTH, n // tn),
        in_specs=[pl.BlockSpec((r, d), lambda l, j: (0, 0)),
                  pl.BlockSpec((None, d, tn), lambda l, j: (l, 0, j)),
                  pl.BlockSpec((None, 1, tn), lambda l, j: (l, 0, j))],
        out_specs=pl.BlockSpec((None, r, tn), lambda l, j: (l, 0, j)),
        out_shape=jax.ShapeDtypeStruct((DEPTH, r, n), F32),
        compiler_params=_params("arbitrary", "arbitrary"),
        name="adaln_modulation",
    )(c_all, w_mod, b_mod.reshape(DEPTH, 1, n))


def _per_row(ref, rows):
    a = ref[...]
    if a.ndim == 2:
        return a
    g, _, d = a.shape
    return jnp.broadcast_to(a, (g, rows // g, d)).reshape(rows, d)


def _inproj_kernel(x_ref, g_ref, sh_ref, sc_ref, w_ref, *o_refs, splits, scales, w_rows_out):
    tm = x_ref.shape[0]
    h = _rms(x_ref[...], g_ref[...]) * (1.0 + _per_row(sc_ref, tm)) + _per_row(sh_ref, tm)
    hb = h.astype(BF16)
    for o_ref, (c0, n), s in zip(o_refs, splits, scales):
        for cc in range(0, n, PROJ_COLS):
            w = min(PROJ_COLS, n - cc)
            if w_rows_out:
                r = _dot_nt(hb, w_ref[c0 + cc:c0 + cc + w, :])
            else:
                r = jnp.dot(hb, w_ref[:, c0 + cc:c0 + cc + w].astype(BF16), preferred_element_type=F32)
            o_ref[:, cc:cc + w] = (r * s if s != 1.0 else r).astype(o_ref.dtype)
        if n < o_ref.shape[1]:
            o_ref[:, n:] = jnp.zeros((o_ref.shape[0], o_ref.shape[1] - n), o_ref.dtype)


def _inproj(x, g, shift, scale, w_all, j, splits, scales, dtypes, rows_per_mod, name):
    m, d = x.shape
    assert w_all.shape[1] != w_all.shape[2] and d in w_all.shape[1:]
    pad = lambda n: -(-n // LANES) * LANES
    tm = min(PROMPT_ROWS, rows_per_mod) if rows_per_mod >= PROMPT_ROWS else min(SAMPLE_TILE_ROWS, m)
    if rows_per_mod >= tm:
        per = rows_per_mod // tm
        mod_spec = pl.BlockSpec((None, 1, d), lambda i: (i // per, 0, 0))
    else:
        mod_spec = pl.BlockSpec((tm // rows_per_mod, 1, d), lambda i: (i, 0, 0))
    return pl.pallas_call(
        functools.partial(_inproj_kernel, splits=splits, scales=scales, w_rows_out=w_all.shape[2] == d),
        grid=(m // tm,),
        in_specs=[pl.BlockSpec((tm, d), lambda i: (i, 0)),
                  pl.BlockSpec((1, d), lambda i: (0, 0)),
                  mod_spec, mod_spec,
                  pl.BlockSpec((None,) + w_all.shape[1:], lambda i: (j, 0, 0), pipeline_mode=pl.Buffered(1))],
        out_specs=[pl.BlockSpec((tm, pad(n)), lambda i: (i, 0)) for _, n in splits],
        out_shape=[jax.ShapeDtypeStruct((m, pad(n)), dtype) for (_, n), dtype in zip(splits, dtypes)],
        compiler_params=_params("arbitrary"),
        name=name,
    )(x, g.reshape(1, d), shift, scale, w_all)


def _gdn_front_kernel(x_ref, g_ref, sh_ref, sc_ref, w_ref, cw_ref, qkv_ref, z_ref, ab_ref, tail_ref, carry, rbuf,
                      *, tiles_per_seq):
    tm = x_ref.shape[0]
    head = GDN_CONV - 1
    base = SUBLANES - head
    width = 4 * LANES

    @pl.when(pl.program_id(0) % tiles_per_seq == 0)
    def _():
        carry[...] = jnp.zeros(carry.shape, F32)

    h = _rms(x_ref[...], g_ref[...]) * (1.0 + sc_ref[...]) + sh_ref[...]
    hb = h.astype(BF16)
    proj = lambda c0: _dot_nt(hb, w_ref[c0:c0 + width, :])
    n_qkv, n_z = GDN_CONV_CH // width, GDN_V_W // width
    r_next = proj(0)
    for cb in range(n_qkv):
        cols = slice(cb * width, (cb + 1) * width)
        r = r_next
        if cb + 1 < n_qkv:
            r_next = proj((cb + 1) * width)
        if cb % 2 == 1:
            zc = cb // 2
            z_ref[:, zc * width:(zc + 1) * width] = proj(GDN_CONV_CH + zc * width).astype(z_ref.dtype)
        rbuf[0:SUBLANES, :] = carry[:, cols]
        rbuf[SUBLANES:SUBLANES + tm, :] = r
        y = rbuf[base:base + tm, :] * cw_ref[0:1, cols]
        for w in range(1, GDN_CONV):
            y = y + rbuf[base + w:base + w + tm, :] * cw_ref[w:w + 1, cols]
        carry[:, cols] = r[tm - SUBLANES:tm]
        tail_ref[:, cols] = r[tm - SUBLANES:tm]
        y = _silu(y)
        if cb * width < 2 * GDN_K_W:
            parts = []
            for p in range(width // GDN_HEAD_DIM):
                blk = y[:, p * GDN_HEAD_DIM:(p + 1) * GDN_HEAD_DIM]
                inv = lax.rsqrt(jnp.sum(blk * blk, axis=-1, keepdims=True) + EPS)
                parts.append(blk * (inv * GDN_HEAD_DIM ** -0.5 if cb * width < GDN_K_W else inv))
            y = jnp.concatenate(parts, axis=1)
        qkv_ref[:, cols] = y.astype(qkv_ref.dtype)
    assert n_z == n_qkv // 2
    nab = 2 * GDN_V_HEADS
    ab_ref[:, 0:nab] = _dot_nt(hb, w_ref[GDN_CONV_CH + GDN_V_W:GDN_CONV_CH + GDN_V_W + nab, :])
    ab_ref[:, nab:] = jnp.zeros((tm, LANES - nab), F32)


def _gdn_front(x, g, shift, scale, w_all_t, j, cw, t):
    m, d = x.shape
    tm = min(PROMPT_ROWS, t)
    per = t // tm
    mod_spec = pl.BlockSpec((None, 1, d), lambda i: (i // per, 0, 0))
    row = lambda w: pl.BlockSpec((tm, w), lambda i: (i, 0))
    const = lambda i: (0, 0)
    return pl.pallas_call(
        functools.partial(_gdn_front_kernel, tiles_per_seq=per),
        grid=(m // tm,),
        in_specs=[row(d), pl.BlockSpec((1, d), const), mod_spec, mod_spec,
                  pl.BlockSpec((None,) + w_all_t.shape[1:], lambda i: (j, 0, 0), pipeline_mode=pl.Buffered(1)),
                  pl.BlockSpec(cw.shape, const)],
        out_specs=[row(GDN_CONV_CH), row(GDN_V_W), row(LANES),
                   pl.BlockSpec((None, SUBLANES, GDN_CONV_CH), lambda i: (i // per, 0, 0))],
        out_shape=[jax.ShapeDtypeStruct((m, GDN_CONV_CH), BF16), jax.ShapeDtypeStruct((m, GDN_V_W), BF16),
                   jax.ShapeDtypeStruct((m, LANES), F32), jax.ShapeDtypeStruct((m // t, SUBLANES, GDN_CONV_CH), F32)],
        scratch_shapes=[pltpu.VMEM((SUBLANES, GDN_CONV_CH), F32), pltpu.VMEM((SUBLANES + tm, 4 * LANES), F32)],
        compiler_params=_params("arbitrary"),
        name="gdn_front_p",
    )(x, g.reshape(1, d), shift, scale, w_all_t, cw)


def _finish(a, w_ref, x, gate, fg_ref, final_norm):
    y = x + gate * _dot(a, w_ref[...])
    return _rms(y, fg_ref[...]) if final_norm else y


def _outproj_kernel(a_ref, w_ref, x_ref, gate_ref, fg_ref, o_ref, *, final_norm):
    gate = _per_row(gate_ref, x_ref.shape[0])
    o_ref[...] = _finish(a_ref[...], w_ref, x_ref[...], gate, fg_ref, final_norm)


def _outproj(a, w, x, gate, fg, final_norm, name):
    m, k = a.shape
    d = x.shape[-1]
    tm = min(PROMPT_ROWS, m)
    return pl.pallas_call(
        functools.partial(_outproj_kernel, final_norm=final_norm),
        grid=(m // tm,),
        in_specs=[pl.BlockSpec((tm, k), lambda i: (i, 0)),
                  pl.BlockSpec((k, d), lambda i: (0, 0)),
                  pl.BlockSpec((tm, d), lambda i: (i, 0)),
                  pl.BlockSpec((tm // SAMPLE_ROWS, 1, d), lambda i: (i, 0, 0)),
                  pl.BlockSpec((1, d), lambda i: (0, 0))],
        out_specs=pl.BlockSpec((tm, d), lambda i: (i, 0)),
        out_shape=jax.ShapeDtypeStruct((m, d), F32),
        compiler_params=_params("arbitrary"),
        name=name,
    )(a, w, x, gate, fg.reshape(1, d))


def _swa_block(q, kcat, vcat, mask4, sinks_ref):
    nq = q.shape[0]
    lo = _iota((1, LANES), 1) < HALF
    hi = jnp.logical_not(lo)
    row = _iota((SWA_GROUP * nq, 1), 0)
    kvs = range(SWA_KV_HEADS)
    blk = lambda a, pb: a[:, pb * LANES:(pb + 1) * LANES]
    q4s = []
    for kh in kvs:
        parts = []
        for g in range(SWA_GROUP):
            qg = blk(q, 2 * kh + g // 2)
            if g % 2 != kh % 2:
                qg = pltpu.roll(qg, HALF, axis=1)
            parts.append(jnp.where(lo if kh % 2 == 0 else hi, qg, 0.0))
        q4s.append(jnp.concatenate(parts, axis=0))
    logits = [_dot_nt(q4, blk(kcat, kh // 2)) for kh, q4 in zip(kvs, q4s)]
    es, invs = [], []
    for kh, lg in zip(kvs, logits):
        sink = jnp.full((SWA_GROUP * nq, 1), sinks_ref[SWA_GROUP * kh] * LOG2E, F32)
        for g in range(1, SWA_GROUP):
            sink = jnp.where(row >= g * nq, sinks_ref[SWA_GROUP * kh + g] * LOG2E, sink)
        lg = jnp.where(mask4, lg, -jnp.inf)
        m = jnp.maximum(jnp.max(lg, axis=-1, keepdims=True), sink)
        e = jnp.exp2(lg - m)
        es.append(e)
        invs.append(1.0 / (jnp.sum(e, axis=-1, keepdims=True) + jnp.exp2(sink - m)))
    o4s = [_dot(e, blk(vcat, kh // 2)) * inv for kh, e, inv in zip(kvs, es, invs)]
    outs = [None] * (SWA_HEADS // 2)
    for kh, o4 in zip(kvs, o4s):
        for g in range(SWA_GROUP):
            piece = o4[g * nq:(g + 1) * nq]
            if g % 2 != kh % 2:
                piece = pltpu.roll(piece, HALF, axis=1)
            piece = jnp.where(lo if g % 2 == 0 else hi, piece, 0.0)
            qb = 2 * kh + g // 2
            outs[qb] = piece if outs[qb] is None else outs[qb] + piece
    return jnp.concatenate(outs, axis=1)


def _band_mask(nq, nk, jmin):
    i = _iota((SWA_GROUP * nq, nk), 0) % nq
    j = _iota((SWA_GROUP * nq, nk), 1)
    return (j >= i) & (j <= i + WINDOW) & (j >= jmin)


def _swa_prompt_kernel(sinks_ref, q_ref, kp_ref, vp_ref, k_ref, v_ref, z_ref, x_ref, gate_ref, w_ref,
                       o_ref, kbuf, vbuf, abuf, *, tq):
    t = pl.program_id(1)
    kbuf[0:WINDOW] = kp_ref[...]
    kbuf[WINDOW:WINDOW + tq] = k_ref[...]
    vbuf[0:WINDOW] = vp_ref[...]
    vbuf[WINDOW:WINDOW + tq] = v_ref[...]

    def body(sb, carry):
        r = pl.multiple_of(sb * WINDOW, WINDOW)
        jmin = jnp.where((t == 0) & (sb == 0), WINDOW, 0)
        mask4 = _band_mask(WINDOW, 2 * WINDOW, jmin)
        abuf[pl.ds(r, WINDOW), :] = _swa_block(q_ref[pl.ds(r, WINDOW), :].astype(F32), kbuf[pl.ds(r, 2 * WINDOW), :],
                                               vbuf[pl.ds(r, 2 * WINDOW), :], mask4, sinks_ref)
        return carry

    lax.fori_loop(0, tq // WINDOW, body, 0)
    a = abuf[...] * _silu(z_ref[...].astype(F32))
    o_ref[...] = x_ref[...] + gate_ref[...] * _dot(a, w_ref[...])


def _swa_prompt(sinks, q, k, v, z, x, gate, w_out, b, t):
    d = x.shape[-1]
    tq = min(PROMPT_ROWS, t)
    nt = t // tq
    sub = tq // WINDOW
    row = lambda bi, ti: (bi * nt + ti, 0)
    prev = lambda bi, ti: (bi * nt * sub + jnp.maximum(ti * sub - 1, 0), 0)
    const = lambda bi, ti: (0, 0)
    return pl.pallas_call(
        functools.partial(_swa_prompt_kernel, tq=tq),
        grid=(b, nt),
        in_specs=[pl.BlockSpec(memory_space=pltpu.SMEM),
                  pl.BlockSpec((tq, SWA_Q_W), row),
                  pl.BlockSpec((WINDOW, SWA_KV_W), prev),
                  pl.BlockSpec((WINDOW, SWA_KV_W), prev),
                  pl.BlockSpec((tq, SWA_KV_W), row),
                  pl.BlockSpec((tq, SWA_KV_W), row),
                  pl.BlockSpec((tq, SWA_Q_W), row),
                  pl.BlockSpec((tq, d), row),
                  pl.BlockSpec((None, 1, d), lambda bi, ti: (bi, 0, 0)),
                  pl.BlockSpec(w_out.shape, const)],
        out_specs=pl.BlockSpec((tq, d), row),
        out_shape=jax.ShapeDtypeStruct(x.shape, F32),
        scratch_shapes=[pltpu.VMEM((WINDOW + tq, SWA_KV_W), F32),
                        pltpu.VMEM((WINDOW + tq, SWA_KV_W), F32),
                        pltpu.VMEM((tq, SWA_Q_W), F32)],
        compiler_params=_params("arbitrary", "arbitrary"),
        name="swa_prompt",
    )(sinks, q, k, v, k, v, z, x, gate, w_out)


def _swa_sample_kernel(sinks_ref, q_ref, k_ref, v_ref, z_ref, ck_ref, cv_ref, *rest, bb, dt, has_prev):
    a_ref, nk_ref, nv_ref = rest[2:] if has_prev else rest
    nq, grp = SAMPLE_ROWS, SWA_GROUP
    lane = _iota((1, LANES), 1)
    lo = lane < HALF
    hi = jnp.logical_not(lo)
    row = _iota((grp * nq, 1), 0)
    qi = row % nq
    see_cache, see_own = lane >= qi, lane <= qi
    zpad = jnp.zeros((WINDOW - nq, LANES), F32)
    rows = lambda bi: slice(bi * nq, (bi + 1) * nq)
    blk = lambda pb: slice(pb * LANES, (pb + 1) * LANES)
    units = [(bi, kh) for bi in range(bb) for kh in range(SWA_KV_HEADS)]

    q4s = []
    for bi, kh in units:
        keep = lo if kh % 2 == 0 else hi
        parts = []
        for g in range(grp):
            qg = q_ref[rows(bi), blk(2 * kh + g // 2)]
            if g % 2 != kh % 2:
                qg = pltpu.roll(qg, HALF, axis=1)
            parts.append(jnp.where(keep, qg, 0.0))
        q4s.append(jnp.concatenate(parts, axis=0))
    own = lambda ref, bi, pb: jnp.concatenate([ref[rows(bi), blk(pb)], zpad], axis=0)
    lcs = [_dot(q4, ck_ref[bi, blk(kh // 2), :]) for (bi, kh), q4 in zip(units, q4s)]
    los = [_dot_nt(q4, own(k_ref, bi, kh // 2)) for (bi, kh), q4 in zip(units, q4s)]
    ecs, eos, invs = [], [], []
    for (bi, kh), lc, lw in zip(units, lcs, los):
        sink = jnp.full((grp * nq, 1), sinks_ref[grp * kh] * LOG2E, F32)
        for g in range(1, grp):
            sink = jnp.where(row >= g * nq, sinks_ref[grp * kh + g] * LOG2E, sink)
        lc = jnp.where(see_cache, lc, -jnp.inf)
        lw = jnp.where(see_own, lw, -jnp.inf)
        m = jnp.maximum(jnp.maximum(jnp.max(lc, axis=-1, keepdims=True), jnp.max(lw, axis=-1, keepdims=True)), sink)
        ec, eo = jnp.exp2(lc - m), jnp.exp2(lw - m)
        den = jnp.sum(ec, axis=-1, keepdims=True) + jnp.sum(eo, axis=-1, keepdims=True) + jnp.exp2(sink - m)
        ecs.append(ec)
        eos.append(eo)
        invs.append(1.0 / den)
    o4s = [(_dot_nt(ec, cv_ref[bi, blk(kh // 2), :]) + _dot(eo, own(v_ref, bi, kh // 2))) * inv
           for (bi, kh), ec, eo, inv in zip(units, ecs, eos, invs)]
    for bi in range(bb):
        outs = [None] * (SWA_HEADS // 2)
        for kh in range(SWA_KV_HEADS):
            o4 = o4s[bi * SWA_KV_HEADS + kh]
            for g in range(grp):
                piece = o4[g * nq:(g + 1) * nq]
                if g % 2 != kh % 2:
                    piece = pltpu.roll(piece, HALF, axis=1)
                piece = jnp.where(lo if g % 2 == 0 else hi, piece, 0.0)
                qb = 2 * kh + g // 2
                outs[qb] = piece if outs[qb] is None else outs[qb] + piece
        a_ref[rows(bi), :] = jnp.concatenate(outs, axis=1) * _silu(z_ref[rows(bi), :])

    fresh = lane >= WINDOW - dt
    for src, cache, dst in ((k_ref, ck_ref, nk_ref), (v_ref, cv_ref, nv_ref)):
        for pb in range(SWA_KV_W // LANES):
            new = src[:, blk(pb)]
            if bb * nq < LANES:
                new = jnp.concatenate([new, jnp.zeros((LANES - bb * nq, LANES), F32)], axis=0)
            new_t = new.T
            for bi in range(bb):
                cols = pltpu.roll(new_t, (WINDOW - dt - bi * nq) % LANES, axis=1)
                dst[bi, blk(pb), :] = jnp.where(fresh, cols, pltpu.roll(cache[bi, blk(pb), :], WINDOW - dt, axis=1))


def _swa_sample(sinks, q, k, v, z, ck_all, cv_all, nk_prev, nv_prev, j, dt):
    db = ck_all.shape[1]
    bb = min(SWA_SAMPLE_SEQS, db)
    nq = SAMPLE_ROWS
    assert 2 * dt <= nq
    row = lambda i: (i, 0)
    c_spec = pl.BlockSpec((None, bb, SWA_KV_W, WINDOW), lambda i: (j, i, 0, 0))
    args = [sinks, q, k, v, z, ck_all, cv_all]
    in_specs = [pl.BlockSpec(memory_space=pltpu.SMEM),
                pl.BlockSpec((bb * nq, SWA_Q_W), row),
                pl.BlockSpec((bb * nq, SWA_KV_W), row),
                pl.BlockSpec((bb * nq, SWA_KV_W), row),
                pl.BlockSpec((bb * nq, SWA_Q_W), row),
                c_spec, c_spec]
    aliases = {}
    if nk_prev is not None:
        args += [nk_prev, nv_prev]
        in_specs += [pl.BlockSpec(memory_space=pl.ANY)] * 2
        aliases = {len(args) - 2: 1, len(args) - 1: 2}
    return pl.pallas_call(
        functools.partial(_swa_sample_kernel, bb=bb, dt=dt, has_prev=nk_prev is not None),
        grid=(db // bb,),
        in_specs=in_specs,
        out_specs=[pl.BlockSpec((bb * nq, SWA_Q_W), row), c_spec, c_spec],
        out_shape=[jax.ShapeDtypeStruct(q.shape, F32),
                   jax.ShapeDtypeStruct(ck_all.shape, F32),
                   jax.ShapeDtypeStruct(cv_all.shape, F32)],
        input_output_aliases=aliases,
        compiler_params=_params("arbitrary"),
        name="swa_sample",
    )(*args)


def _block_mul(a, b):
    size, width = b.shape
    b16 = b.astype(BF16)
    zero = jnp.zeros_like(b16)
    blk = _iota((1, width), 1) // size
    return _dot(a, jnp.concatenate([jnp.where(blk == r, b16, zero) for r in range(width // size)], axis=0))


def _diag2(a, b):
    return jnp.concatenate([jnp.concatenate([a, jnp.zeros_like(b)], axis=1),
                            jnp.concatenate([jnp.zeros_like(a), b], axis=1)], axis=0)


def _gdn_kernel(qkv, z_ref, ab_ref, x_ref, gate_ref, s0_ref, alog_ref, dtb_ref, ong_ref, w_ref, fg_ref, o_ref, s_ref,
                obuf, *, nb, ns, final_norm):
    nchunks = GDN_TILE // GDN_CHUNK
    R, C, HD = GDN_TILE, GDN_CHUNK, GDN_HEAD_DIM
    t = pl.program_id(1)
    seqs = range(nb)
    nu = nb * ns
    seq = lambda u: u // ns
    off = lambda u: (u % ns) * R

    @pl.when(t == 0)
    def _():
        s_ref[...] = s0_ref[...]

    ri, ci = _iota((R, R), 0), _iota((R, R), 1)
    same = (ri // C) == (ci // C)
    tri, ones = jnp.where(same & (ci <= ri), 1.0, 0.0), jnp.where(same, 1.0, 0.0)
    gcs, betas, egcs, bgcs, gcTs, glTs, dlTs = ([None] * nu for _ in range(7))

    def prologue(sq):
        ab = ab_ref[seq(sq), off(sq):off(sq) + R]
        g = -jnp.exp(alog_ref[...]) * jax.nn.softplus(ab + dtb_ref[...])
        beta = pltpu.roll(jax.nn.sigmoid(ab), LANES - GDN_V_HEADS, axis=1)
        gc = _dot_f32(tri, g)
        gl = _dot_f32(ones, g)
        egc = jnp.exp(gc)
        gcT, glT = gc.T, gl.T
        gcs[sq], betas[sq], egcs[sq], bgcs[sq] = gc, beta, egc, beta * egc
        gcTs[sq], glTs[sq], dlTs[sq] = gcT, glT, glT - gcT
        yield

    lane = _iota((1, LANES), 1)
    lo = lane < HALF
    jj = lane % C
    ii = _iota((C, 1), 0)
    incl, strict = ii >= jj, ii > jj
    zc = jnp.zeros((C, HD), F32)

    def col2(arr, r0, h0):
        return jnp.where(lo, arr[r0:r0 + C, h0:h0 + 1], arr[r0:r0 + C, h0 + 1:h0 + 2])

    def pick(c, row_a, row_b):
        if c == 0:
            return jnp.where(lo, row_a, pltpu.roll(row_b, HALF, axis=1))
        return jnp.where(lo, pltpu.roll(row_a, HALF, axis=1), row_b)

    pairs = range(GDN_QK_HEADS)
    heads = range(GDN_V_HEADS)
    units = [(c, p) for c in range(nchunks) for p in pairs]
    quads = [(c, pp) for c in range(nchunks) for pp in range(GDN_QK_HEADS // 2)]
    rows = lambda sq, c: slice(off(sq) + c * C, off(sq) + (c + 1) * C)
    qsl = lambda sq, c, p: qkv[seq(sq), rows(sq, c), p * HD:(p + 1) * HD]
    ksl = lambda sq, c, p: qkv[seq(sq), rows(sq, c), GDN_K_W + p * HD:GDN_K_W + (p + 1) * HD]
    vsl = lambda sq, c, h: qkv[seq(sq), rows(sq, c), 2 * GDN_K_W + h * HD:2 * GDN_K_W + (h + 1) * HD]
    col = lambda arr, c, h: arr[c * C:(c + 1) * C, h:h + 1]
    jq = _iota((1, 2 * LANES), 1) % C
    kts, lows, qks, sols = ([None] * nu for _ in range(4))

    def products(sq):
        kts[sq] = [qkv[seq(sq), off(sq):off(sq) + R, GDN_K_W + p * HD:GDN_K_W + (p + 1) * HD].astype(F32).T
                   for p in pairs]
        yield
        ms = [_dot_nt(jnp.concatenate([ksl(sq, c, p), qsl(sq, c, p)], axis=0), jnp.concatenate([ksl(sq, c, p)] * 2, axis=0))
              for c, p in units]
        yield
        lows[sq], qks[sq] = [], []
        for (c, p), m in zip(units, ms):
            h0 = 2 * p
            row_gc = pick(c, gcTs[sq][h0:h0 + 1, :], gcTs[sq][h0 + 1:h0 + 2, :])
            decay = jnp.exp(jnp.where(incl, col2(gcs[sq], c * C, h0) - row_gc, -jnp.inf))
            lows[sq].append(jnp.where(strict, m[0:C] * decay, 0.0) * col2(betas[sq], c * C, h0))
            qks[sq].append(m[C:2 * C] * decay)
            if p % 4 == 3:
                yield

    def inverse(sq):
        low4 = [jnp.concatenate([lows[sq][c * GDN_QK_HEADS + 2 * pp], lows[sq][c * GDN_QK_HEADS + 2 * pp + 1]], axis=1)
                for c, pp in quads]
        invs = [jnp.where(ii == jq, 1.0, 0.0) - jnp.where((ii // 2) == (jq // 2), low, 0.0) for low in low4]
        yield
        s = 2
        while s < C:
            below = ((ii // s) % 2 == 1) & ((jq // s) == (ii // s) - 1)
            eps, new = [], []
            for i, (low, inv) in enumerate(zip(low4, invs)):
                eps.append(_block_mul(jnp.where(below, low, 0.0), inv))
                if i % 2:
                    yield
            for i, (inv, ep) in enumerate(zip(invs, eps)):
                new.append(inv - _block_mul(inv, ep))
                if i % 2:
                    yield
            invs = new
            s *= 2
        sols[sq] = []
        for (c, pp), inv in zip(quads, invs):
            rhs = []
            for idx in range(4):
                h = 4 * pp + idx
                piece = [vsl(sq, c, h) * col(betas[sq], c, h), ksl(sq, c, h // 2) * col(bgcs[sq], c, h)]
                rhs.append(jnp.concatenate([zc] * (2 * idx) + piece + [zc] * (6 - 2 * idx), axis=1))
            sols[sq].append(_dot(inv, jnp.concatenate(rhs, axis=0)))
            if pp % 2 == 1:
                yield

    def state(sq):
        for c in range(nchunks):
            in_chunk = (lane // C) == c
            quad = lambda h: sols[sq][c * (GDN_QK_HEADS // 2) + h // 4]
            w_v = lambda h: quad(h)[:, 2 * (h % 4) * HD:(2 * (h % 4) + 1) * HD]
            w_k = lambda h: quad(h)[:, (2 * (h % 4) + 1) * HD:(2 * (h % 4) + 2) * HD]
            a_s = []
            for p in pairs:
                top = jnp.concatenate([w_k(2 * p), w_k(2 * p + 1)], axis=1)
                bot = jnp.concatenate([qsl(sq, c, p) * col(egcs[sq], c, 2 * p), qsl(sq, c, p) * col(egcs[sq], c, 2 * p + 1)],
                                      axis=1)
                a_s.append(_dot(jnp.concatenate([top, bot], axis=0),
                                _diag2(s_ref[seq(sq), 2 * p], s_ref[seq(sq), 2 * p + 1])))
            yield
            us = [[w_v(2 * p + idx) - a[0:C, idx * HD:(idx + 1) * HD] for idx in range(2)] for p, a in zip(pairs, a_s)]
            for p, a, u in zip(pairs, a_s, us):
                obuf[seq(sq), rows(sq, c), 2 * p * HD:(2 * p + 2) * HD] = (
                    a[C:2 * C] + _dot(qks[sq][c * GDN_QK_HEADS + p], _diag2(*u)))
            yield
            for p, u in zip(pairs, us):
                kd, ue, gt = [], [], []
                for idx in range(2):
                    h = 2 * p + idx
                    kd.append(kts[sq][p] * jnp.exp(jnp.where(in_chunk, dlTs[sq][h:h + 1, :], -jnp.inf)))
                    ue.append(jnp.concatenate([u[idx], zc] if c == 0 else [zc, u[idx]], axis=0))
                    gt.append(jnp.exp(pick(c, glTs[sq][h:h + 1, :], glTs[sq][h:h + 1, :])))
                upd = _dot(jnp.concatenate(kd, axis=1), _diag2(*ue))
                for idx in range(2):
                    h = 2 * p + idx
                    s_ref[seq(sq), h] = s_ref[seq(sq), h] * gt[idx] + upd[:, idx * HD:(idx + 1) * HD]
                if p % 4 == 3:
                    yield

    def run(*gens):
        gens = list(gens)
        while gens:
            for gen in list(gens):
                if next(gen, "done") == "done":
                    gens.remove(gen)

    for k in range(nu + 2):
        live = [itertools.chain(prologue(k), products(k))] if k < nu else []
        live += [inverse(k - 1)] if 0 <= k - 1 < nu else []
        live += [state(k - 2)] if 0 <= k - 2 < nu else []
        run(*live)

    gated = []
    for sq in seqs:
        o = obuf[sq]
        parts = [_rms(o[:, h * HD:(h + 1) * HD], ong_ref[...]) for h in heads]
        gated.append(jnp.concatenate(parts, axis=1) * _silu(z_ref[sq].astype(F32)))
    out = _dot(jnp.concatenate(gated, axis=0), w_ref[...])
    for sq in seqs:
        y = x_ref[sq] + gate_ref[sq] * out[sq * ns * R:(sq + 1) * ns * R]
        o_ref[sq] = _rms(y, fg_ref[...]) if final_norm else y


def _gdn(qkv, z, ab, x, gate, s0, alog, dtb, ong, w_out, fg, final_norm):
    b, t, d = x.shape
    nb = 2 if b % 2 == 0 else 1
    ns = GDN_STEP_TILES if t % (GDN_STEP_TILES * GDN_TILE) == 0 else 1
    rows_in = ns * GDN_TILE
    tile = lambda w: pl.BlockSpec((nb, rows_in, w), lambda bi, ti: (bi, ti, 0))
    const = lambda bi, ti: (0, 0)
    per_b3 = lambda bi, ti: (bi, 0, 0)
    per_b4 = lambda bi, ti: (bi, 0, 0, 0)
    vec = lambda a: jnp.pad(a.astype(F32), (0, LANES - a.shape[0])).reshape(1, LANES)
    state = (GDN_V_HEADS, GDN_HEAD_DIM, GDN_HEAD_DIM)
    return pl.pallas_call(
        functools.partial(_gdn_kernel, nb=nb, ns=ns, final_norm=final_norm),
        grid=(b // nb, t // rows_in),
        in_specs=[tile(GDN_CONV_CH), tile(GDN_V_W), tile(LANES), tile(d),
                  pl.BlockSpec((nb, 1, d), per_b3),
                  pl.BlockSpec((nb,) + state, per_b4),
                  pl.BlockSpec((1, LANES), const),
                  pl.BlockSpec((1, LANES), const),
                  pl.BlockSpec((1, GDN_HEAD_DIM), const),
                  pl.BlockSpec(w_out.shape, const),
                  pl.BlockSpec((1, d), const)],
        out_specs=[tile(d), pl.BlockSpec((nb,) + state, per_b4)],
        out_shape=[jax.ShapeDtypeStruct(x.shape, F32),
                   jax.ShapeDtypeStruct((b,) + state, F32)],
        scratch_shapes=[pltpu.VMEM((nb, rows_in, GDN_V_W), F32)],
        compiler_params=_params("arbitrary", "arbitrary"),
        name="gdn_prompt",
    )(qkv, z, ab, x, gate, s0, vec(alog), vec(dtb), ong.reshape(1, GDN_HEAD_DIM), w_out, fg.reshape(1, d))


def _tile_roll(a, d):
    r, w = a.shape
    return pltpu.roll(a.reshape(r // SUBLANES, SUBLANES, w), d, axis=1).reshape(r, w)


def _gdn_sample_kernel(xqkv_ref, z_ref, ab_ref, conv_ref, s_ref, cw_ref, alog_ref, dtb_ref, ong_ref, *rest,
                       bb, dt, has_prev):
    a_ref, so_ref, st8, qkv = rest[1:] if has_prev else rest
    R, HD = bb * SUBLANES, GDN_HEAD_DIM
    tix = _iota((R, 1), 0) % SUBLANES
    live = tix < dt
    head = GDN_CONV - 1
    lane = _iota((1, LANES), 1)

    st8[...] = jnp.zeros(st8.shape, F32)
    for bi in range(bb):
        for tap in range(head):
            row = (bi + 1) * SUBLANES - head + tap
            st8[row:row + 1, :] = conv_ref[tap, bi:bi + 1, :]
    for cb in range(GDN_CONV_CH // GDN_K_W):
        cols = slice(cb * GDN_K_W, (cb + 1) * GDN_K_W)
        x, st = xqkv_ref[:, cols], st8[:, cols]
        y = x * cw_ref[head:head + 1, cols]
        for d in range(1, GDN_CONV):
            y = y + jnp.where(tix >= d, _tile_roll(x, d), _tile_roll(st, d)) * cw_ref[head - d:head - d + 1, cols]
        y = _silu(y)
        if cb < 2:
            for p in range(GDN_QK_HEADS):
                blk = y[:, p * HD:(p + 1) * HD]
                inv = lax.rsqrt(jnp.sum(blk * blk, axis=-1, keepdims=True) + EPS)
                if cb == 0:
                    inv = inv * HD ** -0.5
                qkv[:, cb * GDN_K_W + p * HD:cb * GDN_K_W + (p + 1) * HD] = blk * inv
        else:
            qkv[:, cols] = y

    ab = ab_ref[...]
    g = jnp.where(live, -jnp.exp(alog_ref[...]) * jax.nn.softplus(ab + dtb_ref[...]), 0.0)
    beta = jnp.where(live, pltpu.roll(jax.nn.sigmoid(ab), LANES - GDN_V_HEADS, axis=1), 0.0)
    gc, gl = g, g
    for d in range(1, SUBLANES):
        gl = gl + _tile_roll(g, d)
        if d < dt:
            gc = gc + jnp.where(tix >= d, _tile_roll(g, d), 0.0)
    egc = jnp.exp(gc)
    bgc = beta * egc
    ekl = jnp.where(live, jnp.exp(jnp.where(live, gl - gc, 0.0)), 0.0)
    gtot = jnp.exp(gl)
    dec = [None] + [jnp.where(tix >= d, jnp.exp(jnp.where(tix >= d, gc - _tile_roll(gc, d), 0.0)), 0.0)
                    for d in range(1, dt)]

    zero = jnp.zeros((R, LANES), F32)
    kk, qk = [zero] * dt, [zero] * dt
    qsl = lambda p: qkv[:, p * HD:(p + 1) * HD]
    ksl = lambda p: qkv[:, GDN_K_W + p * HD:GDN_K_W + (p + 1) * HD]
    vsl = lambda h: qkv[:, 2 * GDN_K_W + h * HD:2 * GDN_K_W + (h + 1) * HD]
    for p in range(GDN_QK_HEADS):
        mine = (lane // 2) == p
        for d in range(dt):
            ks = ksl(p) if d == 0 else _tile_roll(ksl(p), d)
            qk[d] = jnp.where(mine, jnp.sum(qsl(p) * ks, axis=-1, keepdims=True), qk[d])
            if d:
                kk[d] = jnp.where(mine, jnp.sum(ksl(p) * ks, axis=-1, keepdims=True), kk[d])
    low = [None] + [beta * kk[d] * dec[d] for d in range(1, dt)]
    qkd = [qk[0]] + [qk[d] * dec[d] for d in range(1, dt)]
    l1s1, l1s2 = _tile_roll(low[1], 1), _tile_roll(low[1], 2)
    tinv = [None, -low[1], -low[2] + low[1] * l1s1,
            -low[3] + low[1] * _tile_roll(low[2], 1) + low[2] * l1s2 - low[1] * l1s1 * l1s2]
    heads = range(GDN_V_HEADS)
    spread = lambda arr: [jnp.broadcast_to(arr[:, h:h + 1], (R, LANES)) for h in heads]
    s_beta, s_bgc, s_egc, s_ekl, s_gtot = spread(beta), spread(bgc), spread(egc), spread(ekl), spread(gtot)
    s_tinv = [None] + [spread(tinv[d]) for d in range(1, dt)]
    s_qkd = [spread(qkd[d]) for d in range(dt)]

    xvs, m8s = [], []
    for h in heads:
        p = h // 2
        rv, rk = vsl(h) * s_beta[h], ksl(p) * s_bgc[h]
        xv, xk = rv, rk
        for d in range(1, dt):
            xv = xv + s_tinv[d][h] * _tile_roll(rv, d)
            xk = xk + s_tinv[d][h] * _tile_roll(rk, d)
        xvs.append(xv)
        m8s.append(jnp.where(live, xk, _tile_roll(qsl(p) * s_egc[h], dt)))
    tile = lambda a, bi: a[bi * SUBLANES:(bi + 1) * SUBLANES]
    aws = [jnp.concatenate([_dot(tile(m8s[h], bi), s_ref[bi, h]) for bi in range(bb)], axis=0) for h in heads]
    outs, uds = [], []
    for h in heads:
        u = jnp.where(live, xvs[h] - aws[h], 0.0)
        o = _tile_roll(aws[h], SUBLANES - dt) + s_qkd[0][h] * u
        for d in range(1, dt):
            o = o + s_qkd[d][h] * _tile_roll(u, d)
        outs.append(_rms(o, ong_ref[...]))
        uds.append(u * s_ekl[h])
    a_ref[...] = jnp.concatenate(outs, axis=1) * _silu(z_ref[...])
    tn = (((0,), (0,)), ((), ()))
    for h in heads:
        kp = ksl(h // 2)
        for bi in range(bb):
            upd = lax.dot_general(tile(kp, bi).astype(BF16), tile(uds[h], bi).astype(BF16), tn,
                                  preferred_element_type=F32)
            so_ref[bi, h] = s_ref[bi, h] * s_gtot[h][bi * SUBLANES:bi * SUBLANES + 1] + upd


def _gdn_sample(xqkv, z, ab, conv_all, s_all, s_prev, j, cw, alog, dtb, ong, dt):
    db = s_all.shape[1]
    bb = min(GDN_SAMPLE_SEQS, db)
    r = bb * SUBLANES
    assert 2 * dt <= SUBLANES and dt == GDN_CONV
    row = lambda i: (i, 0)
    const = lambda i: (0, 0)
    state = (GDN_V_HEADS, GDN_HEAD_DIM, GDN_HEAD_DIM)
    s_spec = pl.BlockSpec((None, bb) + state, lambda i: (j, i, 0, 0, 0))
    vec = lambda a: jnp.pad(a.astype(F32), (0, LANES - a.shape[0])).reshape(1, LANES)
    args = [xqkv, z, ab, conv_all, s_all, cw, vec(alog), vec(dtb), ong.reshape(1, GDN_HEAD_DIM)]
    in_specs = [pl.BlockSpec((r, GDN_CONV_CH), row),
                pl.BlockSpec((r, GDN_V_W), row),
                pl.BlockSpec((r, LANES), row),
                pl.BlockSpec((None, GDN_CONV - 1, bb, GDN_CONV_CH), lambda i: (j, 0, i, 0)),
                s_spec,
                pl.BlockSpec(cw.shape, const),
                pl.BlockSpec((1, LANES), const),
                pl.BlockSpec((1, LANES), const),
                pl.BlockSpec((1, GDN_HEAD_DIM), const)]
    aliases = {}
    if s_prev is not None:
        args.append(s_prev)
        in_specs.append(pl.BlockSpec(memory_space=pl.ANY))
        aliases = {len(args) - 1: 1}
    return pl.pallas_call(
        functools.partial(_gdn_sample_kernel, bb=bb, dt=dt, has_prev=s_prev is not None),
        grid=(db // bb,),
        in_specs=in_specs,
        out_specs=[pl.BlockSpec((r, GDN_V_W), row), s_spec],
        out_shape=[jax.ShapeDtypeStruct(z.shape, F32), jax.ShapeDtypeStruct(s_all.shape, F32)],
        scratch_shapes=[pltpu.VMEM((r, GDN_CONV_CH), F32), pltpu.VMEM((r, GDN_CONV_CH), F32)],
        input_output_aliases=aliases,
        compiler_params=_params("arbitrary"),
        name="gdn_sample",
    )(*args)


def kernel(x_prompt, x_sample, cache_swa_k, cache_swa_v, state_gdn_conv, state_gdn_s, c_prompt, c_sample, norm_g,
           w_mod, b_mod, swa_w_in, swa_sinks, swa_w_out, gdn_w_in, gdn_conv_w, gdn_a_log, gdn_dt_bias, gdn_o_norm_g,
           gdn_w_out, final_norm_g):
    b, t, d = x_prompt.shape
    db, dt, _ = x_sample.shape
    assert d == D_MODEL and t % GDN_TILE == 0 and GDN_CONV - 1 <= dt <= SAMPLE_ROWS
    sr = SAMPLE_ROWS

    mod = _modulation(jnp.concatenate([c_prompt, c_sample], axis=0), w_mod, b_mod)
    xp = x_prompt.reshape(b * t, d)
    xs = jnp.pad(x_sample, ((0, 0), (0, sr - dt), (0, 0))).reshape(db * sr, d)

    gdn_in_t = jnp.transpose(gdn_w_in, (0, 2, 1)).astype(BF16)
    swa_out = swa_w_out.astype(BF16)
    gdn_out = gdn_w_out.astype(BF16)
    swa_splits = ((0, SWA_Q_W), (SWA_Q_W, SWA_KV_W), (SWA_Q_W + SWA_KV_W, SWA_KV_W), (SWA_Q_W + 2 * SWA_KV_W, SWA_Q_W))
    swa_scales = (SWA_SCALE * LOG2E, 1.0, 1.0, 1.0)
    gdn_splits = ((0, GDN_CONV_CH), (GDN_CONV_CH, GDN_V_W), (GDN_CONV_CH + GDN_V_W, 2 * GDN_V_HEADS))
    gdn_scales = (1.0, 1.0, 1.0)

    n_swa = cache_swa_k.shape[0]
    feature_major = lambda c: jnp.transpose(c, (0, 1, 3, 4, 2)).reshape(n_swa, db, SWA_KV_W, WINDOW)
    position_major = lambda c: jnp.transpose(c.reshape(n_swa, db, SWA_KV_HEADS, SWA_HEAD_DIM, WINDOW), (0, 1, 4, 2, 3))
    ck_all, cv_all = feature_major(cache_swa_k), feature_major(cache_swa_v)
    conv_all = jnp.transpose(state_gdn_conv, (0, 2, 1, 3))

    kp_out, vp_out, cp_out, sp_out, cs_out = [], [], [], [], []
    s_sample = nk_all = nv_all = None
    for layer in range(DEPTH):
        j = layer // 2
        last = layer == DEPTH - 1
        shift, scale, gate = (mod[layer, :, i * d:(i + 1) * d] for i in range(3))
        p3 = lambda a: a[:b].reshape(b, 1, d)
        s_rows = lambda a: a[b:].reshape(db, 1, d)
        g_l = norm_g[layer]
        if layer % 2 == 0:
            q, k, v, z = _inproj(xp, g_l, p3(shift), p3(scale), swa_w_in, j, swa_splits, swa_scales,
                                 (BF16, F32, F32, BF16), t, "swa_inproj_p")
            xp = _swa_prompt(swa_sinks[j], q, k, v, z, xp, p3(gate), swa_out[j], b, t)
            keep = min(WINDOW, t)
            tail = lambda a: a.reshape(b, t, SWA_KV_W)[:, -keep:].reshape(b, keep, SWA_KV_HEADS, SWA_HEAD_DIM)
            kp_out.append(tail(k))
            vp_out.append(tail(v))

            q, k, v, z = _inproj(xs, g_l, s_rows(shift), s_rows(scale), swa_w_in, j, swa_splits, swa_scales,
                                 (F32,) * 4, sr, "swa_inproj_s")
            a, nk_all, nv_all = _swa_sample(swa_sinks[j], q, k, v, z, ck_all, cv_all, nk_all, nv_all, j, dt)
            xs = _outproj(a, swa_out[j], xs, s_rows(gate), final_norm_g, False, "swa_outproj_s")
        else:
            cw, alog, dtb, ong = gdn_conv_w[j], gdn_a_log[j], gdn_dt_bias[j], gdn_o_norm_g[j]
            qkv, z, ab, tail = _gdn_front(xp, g_l, p3(shift), p3(scale), gdn_in_t, j, cw, t)
            s0 = jnp.zeros((b, GDN_V_HEADS, GDN_HEAD_DIM, GDN_HEAD_DIM), F32)
            seq = lambda a: a.reshape(b, t, a.shape[-1])
            xp, s_new = _gdn(seq(qkv), seq(z), seq(ab), seq(xp), p3(gate), s0, alog, dtb, ong, gdn_out[j],
                             final_norm_g, last)
            xp = xp.reshape(b * t, d)
            cp_out.append(tail[:, SUBLANES - (GDN_CONV - 1):])
            sp_out.append(s_new)

            xqkv, z, ab = _inproj(xs, g_l, s_rows(shift), s_rows(scale), gdn_in_t, j, gdn_splits, gdn_scales,
                                  (F32,) * 3, sr, "gdn_inproj_s")
            a, s_sample = _gdn_sample(xqkv, z, ab, conv_all, state_gdn_s, s_sample, j, cw, alog, dtb, ong, dt)
            xs = _outproj(a, gdn_out[j], xs, s_rows(gate), final_norm_g, last, "gdn_outproj_s")
            cs_out.append(xqkv.reshape(db, sr, GDN_CONV_CH)[:, dt - (GDN_CONV - 1):dt])

    y_prompt = xp.reshape(b, t, d)
    y_sample = xs.reshape(db, sr, d)[:, :dt]
    return (y_prompt, y_sample, jnp.stack(kp_out), jnp.stack(vp_out), position_major(nk_all), position_major(nv_all),
            jnp.stack(cp_out), jnp.stack(sp_out), jnp.stack(cs_out), s_sample)
```
